```python
import jax, jax.numpy as jnp
from jax import lax
import numpy as np

D_MODEL = 1024
BATCH = 4
SEQ = 8192
DEPTH = 4

GRID_W = 64
CTX_LEN = 256
N_EVEN = (DEPTH + 1) // 2
N_ODD = DEPTH // 2
RET_HEADS = 4
RET_DK = 128
RET_DV = 128
RET_CHUNK = 128
CONV_CH = D_MODEL // 2
CONV_WIDTH = 31
MLA_HEADS = 8
MLA_Q_LORA = D_MODEL // 2
MLA_KV_LORA = D_MODEL // 4
MLA_NOPE = 128
MLA_ROPE = 64
MLA_V = 128
MLA_QK = MLA_NOPE + MLA_ROPE
ATTN_BLOCK = 128
N_EXPERTS = 16
EC_CAPACITY_FACTOR = 2
EXPERT_FF = 1024
ROPE_BASE = 10000.0
EPS = 1e-6

A_QK = RET_HEADS * RET_DK
A_V = RET_HEADS * RET_DV
EVEN_SPLIT = (A_QK, 2 * A_QK, 2 * A_QK + A_V, 2 * A_QK + 2 * A_V, 2 * A_QK + 3 * A_V, 2 * A_QK + 3 * A_V + CONV_CH)
EVEN_IN = 2 * A_QK + 3 * A_V + 2 * CONV_CH
EVEN_MIX = A_V + CONV_CH
ODD_SPLIT = (MLA_Q_LORA, MLA_Q_LORA + MLA_KV_LORA)
ODD_IN = MLA_Q_LORA + MLA_KV_LORA + MLA_ROPE
ODD_MIX = MLA_HEADS * MLA_V

kernel_name = "hybrid_retention_conv_mla_ecmoe_dit"


def rms_norm(x, g):
    xf = x.astype(jnp.float32)
    y = xf * lax.rsqrt(jnp.mean(xf * xf, axis=-1, keepdims=True) + EPS)
    return (y * g.astype(jnp.float32)).astype(x.dtype)


def ada_modulate(x, g, shift, scale):
    return rms_norm(x, g) * (1 + scale) + shift


def grid_positions(n):
    n_rows = n // GRID_W
    row = jnp.repeat(jnp.arange(n_rows, dtype=jnp.float32), GRID_W)
    col = jnp.tile(jnp.arange(GRID_W, dtype=jnp.float32), n_rows)
    return row, col


def axial_rope(x, row, col):
    half = x.shape[-1] // 2
    nf = half // 2
    inv = ROPE_BASE ** (-(jnp.arange(nf, dtype=jnp.float32) / nf))
    xf = x.astype(jnp.float32)

    def rot(seg, pos):
        ang = pos[:, None] * inv[None, :]
        cos, sin = jnp.cos(ang), jnp.sin(ang)
        s1, s2 = seg[..., :nf], seg[..., nf:]
        return jnp.concatenate([s1 * cos - s2 * sin, s2 * cos + s1 * sin], axis=-1)

    return jnp.concatenate([rot(xf[..., :half], row), rot(xf[..., half:], col)], axis=-1).astype(x.dtype)


def to_heads(p, n_heads):
    b, n, _ = p.shape
    return p.reshape(b, n, n_heads, -1).transpose(0, 2, 1, 3)


def from_heads(o):
    b, h, n, d = o.shape
    return o.transpose(0, 2, 1, 3).reshape(b, n, h * d)


def flip_seq(t):
    return jnp.flip(t, axis=2)


def retention_chunkwise(q, k, v, log_gamma, state0):
    b, h, n, _ = q.shape
    dv = v.shape[-1]
    L = RET_CHUNK
    nc = n // L

    def chunks(t):
        return t.reshape(b, h, nc, L, t.shape[-1]).transpose(2, 0, 1, 3, 4)

    pos = jnp.arange(L, dtype=jnp.float32)
    lg = log_gamma.astype(jnp.float32)
    diff = pos[:, None] - pos[None, :]
    intra = jnp.where(diff >= 0, jnp.exp(lg[:, None, None] * jnp.maximum(diff, 0.0)), 0.0)
    xi = jnp.exp(lg[:, None] * (pos + 1.0))[:, :, None]
    zeta = jnp.exp(lg[:, None] * (L - 1.0 - pos))[:, :, None]
    decay_chunk = jnp.exp(lg * L)[:, None, None]

    def step(state, qkv):
        qc, kc, vc = qkv
        s = jnp.einsum('bhid,bhjd->bhij', qc, kc) * intra
        o = jnp.einsum('bhij,bhjv->bhiv', s, vc) + jnp.einsum('bhid,bhdv->bhiv', qc, state) * xi
        state = state * decay_chunk + jnp.einsum('bhjd,bhjv->bhdv', kc * zeta, vc)
        return state, o

    state, o = lax.scan(step, state0.astype(jnp.float32), (chunks(q), chunks(k), chunks(v)))
    return o.transpose(1, 2, 0, 3, 4).reshape(b, h, n, dv), state


def head_group_norm(o, g):
    mu = jnp.mean(o, axis=-1, keepdims=True)
    var = jnp.mean(jnp.square(o - mu), axis=-1, keepdims=True)
    return from_heads((o - mu) * lax.rsqrt(var + EPS)) * g.astype(jnp.float32)


def conformer_conv(bv, bg, conv_w, conv_b, ln_g, ln_b):
    u = bv * jax.nn.sigmoid(bg)
    y = lax.conv_general_dilated(
        u, conv_w[:, None, :].astype(u.dtype), window_strides=(1,),
        padding=[(CONV_WIDTH // 2, CONV_WIDTH // 2)],
        dimension_numbers=('NWC', 'WIO', 'NWC'), feature_group_count=CONV_CH) + conv_b
    yf = y.astype(jnp.float32)
    mu = jnp.mean(yf, axis=-1, keepdims=True)
    var = jnp.mean(jnp.square(yf - mu), axis=-1, keepdims=True)
    yn = (yf - mu) * lax.rsqrt(var + EPS) * ln_g.astype(jnp.float32) + ln_b.astype(jnp.float32)
    return jax.nn.silu(yn).astype(bv.dtype)


def even_mixer(hx, hc, w_in, logit_f, logit_b, gn_g, conv_w, conv_b, ln_g, ln_b, w_out, row, col, need_ctx):
    f32 = jnp.float32
    lg_f = jax.nn.log_sigmoid(logit_f.astype(f32))
    lg_b = jax.nn.log_sigmoid(logit_b.astype(f32))

    def branch(h, rope):
        q, k, v, gf, gb, bv, bg = jnp.split(h @ w_in, EVEN_SPLIT, axis=-1)
        q = to_heads(q, RET_HEADS).astype(f32)
        k = to_heads(k, RET_HEADS).astype(f32) * (RET_DK ** -0.5)
        v = to_heads(v, RET_HEADS).astype(f32)
        if rope:
            q = axial_rope(q, row, col)
            k = axial_rope(k, row, col)
        return q, k, v, gf, gb, bv, bg

    def merge(o_f, o_b, gf, gb, bv, bg, dtype):
        ret = head_group_norm(o_f, gn_g) * jax.nn.silu(gf.astype(f32)) \
            + head_group_norm(o_b, gn_g) * jax.nn.silu(gb.astype(f32))
        conv = conformer_conv(bv, bg, conv_w, conv_b, ln_g, ln_b)
        return jnp.concatenate([ret.astype(dtype), conv.astype(dtype)], axis=-1) @ w_out

    qc, kc, vc, gfc, gbc, bvc, bgc = branch(hc, False)
    b = hc.shape[0]
    zero = jnp.zeros((b, RET_HEADS, RET_DK, RET_DV), f32)
    oc_f, st_f = retention_chunkwise(qc, kc, vc, lg_f, zero)
    oc_b, st_b = retention_chunkwise(flip_seq(qc), flip_seq(kc), flip_seq(vc), lg_b, zero)
    qx, kx, vx, gfx, gbx, bvx, bgx = branch(hx, True)
    ox_f, _ = retention_chunkwise(qx, kx, vx, lg_f, st_f)
    ox_b, _ = retention_chunkwise(flip_seq(qx), flip_seq(kx), flip_seq(vx), lg_b, st_b)

    out_x = merge(ox_f, flip_seq(ox_b), gfx, gbx, bvx, bgx, hx.dtype)
    out_c = merge(oc_f, flip_seq(oc_b), gfc, gbc, bvc, bgc, hc.dtype) if need_ctx else None
    return out_x, out_c


def mla_project(h, w_in, q_a_g, kv_a_g, w_uq, w_ukv, qk_g_q, qk_g_k, pos):
    b, n, _ = h.shape
    c_q, c_kv, k_pe = jnp.split(h @ w_in, ODD_SPLIT, axis=-1)
    q = to_heads(rms_norm(c_q, q_a_g) @ w_uq, MLA_HEADS)
    kv = to_heads(rms_norm(c_kv, kv_a_g) @ w_ukv, MLA_HEADS)
    k_nope, v = kv[..., :MLA_NOPE], kv[..., MLA_NOPE:]
    k_pe = jnp.broadcast_to(k_pe[:, None], (b, MLA_HEADS, n, MLA_ROPE))
    k = jnp.concatenate([k_nope, k_pe], axis=-1)
    q = rms_norm(q, qk_g_q)
    k = rms_norm(k, qk_g_k)
    if pos is not None:
        row, col = pos
        q = jnp.concatenate([q[..., :MLA_NOPE], axial_rope(q[..., MLA_NOPE:], row, col)], axis=-1)
        k = jnp.concatenate([k[..., :MLA_NOPE], axial_rope(k[..., MLA_NOPE:], row, col)], axis=-1)
    return q, k, v


def blocked_attention(q, k, v):
    b, h, n, d = q.shape
    nb = n // ATTN_BLOCK
    scale = d ** -0.5
    qb = q.reshape(b, h, nb, ATTN_BLOCK, d).transpose(2, 0, 1, 3, 4)

    def one(qblk):
        s = jnp.einsum('bhqd,bhkd->bhqk', qblk, k).astype(jnp.float32) * scale
        p = jax.nn.softmax(s, axis=-1)
        return jnp.einsum('bhqk,bhkd->bhqd', p.astype(v.dtype), v)

    o = lax.map(one, qb)
    return o.transpose(1, 2, 0, 3, 4).reshape(b, h, n, v.shape[-1])


def odd_mixer(hx, hc, w_in, q_a_g, kv_a_g, w_uq, w_ukv, qk_g_q, qk_g_k, w_out, row, col, need_ctx):
    qx, kx, vx = mla_project(hx, w_in, q_a_g, kv_a_g, w_uq, w_ukv, qk_g_q, qk_g_k, (row, col))
    qc, kc, vc = mla_project(hc, w_in, q_a_g, kv_a_g, w_uq, w_ukv, qk_g_q, qk_g_k, None)
    ox = blocked_attention(qx, jnp.concatenate([kx, kc], axis=2), jnp.concatenate([vx, vc], axis=2))
    out_x = from_heads(ox) @ w_out
    out_c = from_heads(blocked_attention(qc, kc, vc)) @ w_out if need_ctx else None
    return out_x, out_c


def ec_moe(h, router_w, w_gate, w_up, w_down):
    b, n, d = h.shape
    cap = max(1, (EC_CAPACITY_FACTOR * n) // N_EXPERTS)
    aff = jax.nn.softmax(jnp.einsum('bnd,de->bne', h, router_w).astype(jnp.float32), axis=-1)
    gates, idx = lax.top_k(aff.transpose(0, 2, 1), cap)
    xs = jax.vmap(lambda hb, ib: hb[ib])(h, idx)
    a = jnp.einsum('becd,edf->becf', xs, w_gate)
    u = jnp.einsum('becd,edf->becf', xs, w_up)
    y = jnp.einsum('becf,efd->becd', jax.nn.silu(a) * u, w_down) * gates[..., None].astype(h.dtype)
    return jax.vmap(lambda yb, ib: jnp.zeros((n, d), h.dtype).at[ib.reshape(-1)].add(yb.reshape(-1, d)))(y, idx)


def setup_inputs(seed: int = 0) -> dict:
    key = jax.random.key(seed)
    ks = jax.random.split(key, 32)
    f32 = jnp.float32

    def nrm(k, shape, scale):
        return jax.random.normal(k, shape, f32) * scale

    def gain(k, shape):
        return 1.0 + 0.02 * jax.random.normal(k, shape, f32)

    decay_logit = jnp.asarray(np.log(2.0 ** (5.0 + np.arange(RET_HEADS)) - 1.0).astype(np.float32))
    return {
        "x": nrm(ks[0], (BATCH, SEQ, D_MODEL), 1.0),
        "c": nrm(ks[1], (BATCH, D_MODEL), 1.0),
        "ctx": nrm(ks[2], (BATCH, CTX_LEN, D_MODEL), 1.0),
        "c_ctx": nrm(ks[3], (D_MODEL,), 1.0),
        "mod_w": nrm(ks[4], (DEPTH, D_MODEL, 6 * D_MODEL), 0.5 * D_MODEL ** -0.5),
        "mod_b": nrm(ks[5], (DEPTH, 6 * D_MODEL), 0.01),
        "norm1_g": gain(ks[6], (DEPTH, D_MODEL)),
        "norm2_g": gain(ks[7], (DEPTH, D_MODEL)),
        "ev_w_in": nrm(ks[8], (N_EVEN, D_MODEL, EVEN_IN), D_MODEL ** -0.5),
        "ret_decay_f": decay_logit[None] + nrm(ks[9], (N_EVEN, RET_HEADS), 0.01),
        "ret_decay_b": decay_logit[None] + nrm(ks[10], (N_EVEN, RET_HEADS), 0.01),
        "ret_gn_g": gain(ks[11], (N_EVEN, A_V)),
        "conv_w": nrm(ks[12], (N_EVEN, CONV_WIDTH, CONV_CH), CONV_WIDTH ** -0.5),
        "conv_b": nrm(ks[13], (N_EVEN, CONV_CH), 0.01),
        "conv_ln_g": gain(ks[14], (N_EVEN, CONV_CH)),
        "conv_ln_b": nrm(ks[15], (N_EVEN, CONV_CH), 0.01),
        "ev_w_out": nrm(ks[16], (N_EVEN, EVEN_MIX, D_MODEL), EVEN_MIX ** -0.5),
        "od_w_in": nrm(ks[17], (N_ODD, D_MODEL, ODD_IN), D_MODEL ** -0.5),
        "mla_q_a_g": gain(ks[18], (N_ODD, MLA_Q_LORA)),
        "mla_kv_a_g": gain(ks[19], (N_ODD, MLA_KV_LORA)),
        "mla_w_uq": nrm(ks[20], (N_ODD, MLA_Q_LORA, MLA_HEADS * MLA_QK), MLA_Q_LORA ** -0.5),
        "mla_w_ukv": nrm(ks[21], (N_ODD, MLA_KV_LORA, MLA_HEADS * (MLA_NOPE + MLA_V)), MLA_KV_LORA ** -0.5),
        "mla_qk_g_q": gain(ks[22], (N_ODD, MLA_QK)),
        "mla_qk_g_k": gain(ks[23], (N_ODD, MLA_QK)),
        "od_w_out": nrm(ks[24], (N_ODD, ODD_MIX, D_MODEL), ODD_MIX ** -0.5),
        "router_w": nrm(ks[25], (DEPTH, D_MODEL, N_EXPERTS), D_MODEL ** -0.5),
        "moe_w_gate": nrm(ks[26], (DEPTH, N_EXPERTS, D_MODEL, EXPERT_FF), D_MODEL ** -0.5),
        "moe_w_up": nrm(ks[27], (DEPTH, N_EXPERTS, D_MODEL, EXPERT_FF), D_MODEL ** -0.5),
        "moe_w_down": nrm(ks[28], (DEPTH, N_EXPERTS, EXPERT_FF, D_MODEL), EXPERT_FF ** -0.5),
    }


def reference(x, c, ctx, c_ctx, mod_w, mod_b, norm1_g, norm2_g, ev_w_in, ret_decay_f, ret_decay_b,
              ret_gn_g, conv_w, conv_b, conv_ln_g, conv_ln_b, ev_w_out, od_w_in, mla_q_a_g, mla_kv_a_g,
              mla_w_uq, mla_w_ukv, mla_qk_g_q, mla_qk_g_k, od_w_out, router_w, moe_w_gate, moe_w_up,
              moe_w_down):
    row, col = grid_positions(x.shape[1])
    silu_c = jax.nn.silu(c)
    silu_cc = jax.nn.silu(c_ctx)
    for l in range(DEPTH):
        need_ctx = l < DEPTH - 1
        mod_x = (silu_c @ mod_w[l] + mod_b[l])[:, None, :]
        mod_c = silu_cc @ mod_w[l] + mod_b[l]
        sh1, sc1, g1, sh2, sc2, g2 = jnp.split(mod_x, 6, axis=-1)
        sh1c, sc1c, g1c, sh2c, sc2c, g2c = jnp.split(mod_c, 6, axis=-1)
        hx = ada_modulate(x, norm1_g[l], sh1, sc1)
        hc = ada_modulate(ctx, norm1_g[l], sh1c, sc1c)
        i = l // 2
        if l % 2 == 0:
            ox, oc = even_mixer(hx, hc, ev_w_in[i], ret_decay_f[i], ret_decay_b[i], ret_gn_g[i], conv_w[i],
                                conv_b[i], conv_ln_g[i], conv_ln_b[i], ev_w_out[i], row, col, need_ctx)
        else:
            ox, oc = odd_mixer(hx, hc, od_w_in[i], mla_q_a_g[i], mla_kv_a_g[i], mla_w_uq[i], mla_w_ukv[i],
                               mla_qk_g_q[i], mla_qk_g_k[i], od_w_out[i], row, col, need_ctx)
        x = x + g1 * ox
        hx = ada_modulate(x, norm2_g[l], sh2, sc2)
        x = x + g2 * ec_moe(hx, router_w[l], moe_w_gate[l], moe_w_up[l], moe_w_down[l])
        if need_ctx:
            ctx = ctx + g1c * oc
            hc = ada_modulate(ctx, norm2_g[l], sh2c, sc2c)
            ctx = ctx + g2c * ec_moe(hc, router_w[l], moe_w_gate[l], moe_w_up[l], moe_w_down[l])
    return x
```

```python
import functools

import jax
import jax.numpy as jnp
from jax import lax
from jax.experimental import pallas as pl
from jax.experimental.pallas import tpu as pltpu

F32 = jnp.float32
BF16 = jnp.bfloat16
I32 = jnp.int32

D_MODEL = 1024
GRID_W = 64
RET_HEADS = 4
RET_DK = 128
RET_CHUNK = 128
CONV_CH = 512
CONV_WIDTH = 31
CONV_HALO = 16
MLA_HEADS = 8
MLA_Q_LORA = 512
MLA_KV_LORA = 256
MLA_NOPE = 128
MLA_ROPE = 64
MLA_V = 128
MLA_QK = MLA_NOPE + MLA_ROPE
MLA_QK_PAD = 256
N_EXPERTS = 16
EC_CAPACITY_FACTOR = 2
EXPERT_FF = 1024
ROPE_BASE = 10000.0
EPS = 1e-6

TM = 256
ATT_TQ = 512
FF_TILE = 512
VMEM_LIMIT = 56 * 2 ** 20

_NT = (((1,), (1,)), ((), ()))


def _cparams(sem, vmem=None):
    return pltpu.CompilerParams(dimension_semantics=sem, vmem_limit_bytes=vmem)


def _dot(a, b):
    return jnp.dot(a, b, preferred_element_type=F32)


def _split3(a):
    a1 = a.astype(BF16)
    r = a - a1.astype(F32)
    a2 = r.astype(BF16)
    a3 = (r - a2.astype(F32)).astype(BF16)
    return a1, a2, a3


def _dot_hi(a, b, dn):
    a1, a2, a3 = _split3(a)
    b1, b2, b3 = _split3(b)
    d = lambda x, y: lax.dot_general(x, y, dn, preferred_element_type=F32)
    return ((d(a3, b1) + d(a2, b2) + d(a1, b3)) + (d(a2, b1) + d(a1, b2))) + d(a1, b1)


def _silu(a):
    return a * jax.nn.sigmoid(a)


def _rms(x, g):
    return x * lax.rsqrt(jnp.mean(x * x, axis=-1, keepdims=True) + EPS) * g


def _modulate(x, g, shift, scale):
    return _rms(x, g) * (1.0 + scale) + shift


def _mod_kernel(c_ref, w_ref, b_ref, o_ref):
    s = _silu(c_ref[...])
    o_ref[0] = _dot_hi(s, w_ref[0], (((1,), (0,)), ((), ()))) + b_ref[0]


def _mod_all(cc, mod_w, mod_b):
    depth, d, n6 = mod_w.shape
    tn = 1536
    return pl.pallas_call(
        _mod_kernel,
        grid=(depth, n6 // tn),
        in_specs=[pl.BlockSpec((8, d), lambda l, j: (0, 0)),
                  pl.BlockSpec((1, d, tn), lambda l, j: (l, 0, j)),
                  pl.BlockSpec((1, 1, tn), lambda l, j: (l, 0, j))],
        out_specs=pl.BlockSpec((1, 8, tn), lambda l, j: (l, 0, j)),
        out_shape=jax.ShapeDtypeStruct((depth, 8, n6), F32),
        compiler_params=_cparams(("arbitrary", "arbitrary"), VMEM_LIMIT),
        name="mod_all",
    )(cc, mod_w, mod_b.reshape(depth, 1, n6))


def _epilogue(x, out, m, n2g, rwt, xo_ref, aff_ref):
    d = D_MODEL
    xn = x + m[:, 2 * d:3 * d] * out
    xo_ref[0] = xn
    h2 = _modulate(xn, n2g, m[:, 3 * d:4 * d], m[:, 4 * d:5 * d])
    logits_t = _dot_hi(rwt, h2, _NT)
    e = jnp.exp(logits_t - jnp.max(logits_t, axis=0, keepdims=True))
    aff_ref[0] = e / jnp.sum(e, axis=0, keepdims=True)


def _even_in_kernel(x_ref, m_ref, g_ref, w_ref, cos_ref, sin_ref, o_ref):
    d = D_MODEL
    m = m_ref[0]
    h = _modulate(x_ref[0], g_ref[...], m[:, 0:d], m[:, d:2 * d]).astype(BF16)

    def grp(j):
        return _dot(h, w_ref[:, j * 512:(j + 1) * 512])

    c = cos_ref[...]
    s = sin_ref[...]
    o_ref[0, :, 0:512] = (grp(0) * c + grp(7) * s).astype(BF16)
    o_ref[0, :, 512:1024] = ((grp(1) * c + grp(8) * s) * (RET_DK ** -0.5)).astype(BF16)
    for j in range(2, 7):
        o_ref[0, :, j * 512:(j + 1) * 512] = grp(j).astype(BF16)


def _ret_kernel(lg_ref, qf_ref, kf_ref, vf_ref, qb_ref, kb_ref, vb_ref, of_ref, ob_ref,
                st_ref, intra_ref, xi_ref, zeta_ref, dc_ref):
    L = RET_CHUNK
    nc = TM // L

    @pl.when(pl.program_id(1) == 0)
    def _():
        st_ref[...] = jnp.zeros(st_ref.shape, F32)
        ii = lax.broadcasted_iota(I32, (L, L), 0).astype(F32)
        jj = lax.broadcasted_iota(I32, (L, L), 1).astype(F32)
        for dr in range(2):
            for h in range(RET_HEADS):
                x = lg_ref[dr, h]
                lg = jnp.minimum(x, 0.0) - jnp.log(1.0 + jnp.exp(-jnp.abs(x)))
                if dr == 0:
                    diff, xpos, zpos = ii - jj, ii + 1.0, (L - 1.0) - ii
                else:
                    diff, xpos, zpos = jj - ii, L - ii, ii
                intra_ref[dr, h] = jnp.where(diff >= 0, jnp.exp(lg * jnp.maximum(diff, 0.0)), 0.0)
                xi_ref[dr, h] = jnp.exp(lg * xpos)
                zeta_ref[dr, h] = jnp.exp(lg * zpos)
                dc_ref[dr, h] = jnp.exp(lg * float(L))

    for dr, (q_ref, k_ref, v_ref, o_ref) in enumerate(((qf_ref, kf_ref, vf_ref, of_ref),
                                                      (qb_ref, kb_ref, vb_ref, ob_ref))):
        order = range(nc) if dr == 0 else range(nc - 1, -1, -1)
        for c in order:
            for h in range(RET_HEADS):
                rs = slice(c * L, (c + 1) * L)
                cs = slice(h * 128, (h + 1) * 128)
                qc = q_ref[0, rs, cs]
                kc = k_ref[0, rs, cs]
                vc = v_ref[0, rs, cs]
                st = st_ref[dr, h]
                sc = lax.dot_general(qc, kc, _NT, preferred_element_type=F32) * intra_ref[dr, h]
                o = _dot(sc.astype(BF16), vc) + _dot(qc, st.astype(BF16)) * xi_ref[dr, h]
                kz = (kc.astype(F32) * zeta_ref[dr, h]).T.astype(BF16)
                st_ref[dr, h] = st * dc_ref[dr, h] + _dot(kz, vc)
                o_ref[0, rs, cs] = o.astype(BF16)


def _even_out_kernel(x_ref, m_ref, of_ref, ob_ref, gf_ref, gb_ref, bv_ref, bg_ref,
                     pv_ref, pg_ref, nv_ref, ng_ref, gn_ref, cw_ref, cb_ref, lng_ref, lnb_ref,
                     wo_ref, n2_ref, rwt_ref, xo_ref, aff_ref, u_scr, *, n_lat_tiles):
    i = pl.program_id(1)
    hl = CONV_HALO

    def glu(v_ref, g_ref):
        return v_ref[0].astype(F32) * jax.nn.sigmoid(g_ref[0].astype(F32))

    prev_ok = jnp.logical_and(i >= 1, i < n_lat_tiles)
    next_ok = i < n_lat_tiles - 1
    u_scr[0:hl, :] = jnp.where(prev_ok, glu(pv_ref, pg_ref), 0.0)
    u_scr[hl:hl + TM, :] = glu(bv_ref, bg_ref)
    u_scr[hl + TM:2 * hl + TM, :] = jnp.where(next_ok, glu(nv_ref, ng_ref), 0.0)
    y = jnp.zeros((TM, CONV_CH), F32) + cb_ref[...]
    off = hl - CONV_WIDTH // 2
    for k in range(CONV_WIDTH):
        y = y + cw_ref[k:k + 1, :] * u_scr[k + off:k + off + TM, :]
    mu = jnp.mean(y, axis=-1, keepdims=True)
    yc = y - mu
    var = jnp.mean(yc * yc, axis=-1, keepdims=True)
    conv = _silu(yc * lax.rsqrt(var + EPS) * lng_ref[...] + lnb_ref[...])

    def gnorm(o_ref, g_ref):
        parts = []
        for h in range(RET_HEADS):
            cs = slice(h * 128, (h + 1) * 128)
            o = o_ref[0, :, cs].astype(F32)
            mu_h = jnp.mean(o, axis=-1, keepdims=True)
            oc = o - mu_h
            var_h = jnp.mean(oc * oc, axis=-1, keepdims=True)
            parts.append(oc * lax.rsqrt(var_h + EPS) * gn_ref[:, cs] * _silu(g_ref[0, :, cs].astype(F32)))
        return parts

    pf = gnorm(of_ref, gf_ref)
    pb = gnorm(ob_ref, gb_ref)
    out = _dot(conv.astype(BF16), wo_ref[512:1024, :])
    for h in range(RET_HEADS):
        out = out + _dot((pf[h] + pb[h]).astype(BF16), wo_ref[h * 128:(h + 1) * 128, :])
    _epilogue(x_ref[0], out, m_ref[0], n2_ref[...], rwt_ref[...], xo_ref, aff_ref)


def _odd_in_kernel(x_ref, m_ref, g_ref, win_ref, qag_ref, kvag_ref, wuq_ref, wukv_ref,
                   gq_ref, gk_ref, cos_ref, sin_ref, q_ref, k_ref, v_ref):
    d = D_MODEL
    nh = MLA_HEADS
    m = m_ref[0]
    h = _modulate(x_ref[0], g_ref[...], m[:, 0:d], m[:, d:2 * d]).astype(BF16)
    c = _dot(h, win_ref[...])
    cq = _rms(c[:, 0:512], qag_ref[...]).astype(BF16)
    ckv = _rms(c[:, 512:768], kvag_ref[...]).astype(BF16)
    kpe = c[:, 768:896]
    kpe_sw = c[:, 896:1024]
    qa = _dot(cq, wuq_ref[...])
    kva = _dot(ckv, wukv_ref[...])
    cos = cos_ref[...]
    sin = sin_ref[...]
    gq = gq_ref[...]
    gk = gk_ref[...]
    lane = lax.broadcasted_iota(I32, (TM, 128), 1)
    lo = lane < MLA_ROPE
    inv_d = 1.0 / MLA_QK

    ss_kpe = 0.5 * jnp.sum(kpe * kpe, axis=-1, keepdims=True)
    kr = kpe * gk[1:2] * cos + kpe_sw * gk[2:3] * sin
    for p in range(nh // 2):
        r = qa[:, 1024 + p * 128:1024 + (p + 1) * 128]
        r_sw = qa[:, 1536 + p * 128:1536 + (p + 1) * 128]
        r2 = r * r
        ss_r = (jnp.sum(jnp.where(lo, r2, 0.0), axis=-1, keepdims=True),
                jnp.sum(jnp.where(lo, 0.0, r2), axis=-1, keepdims=True))
        qr = r * gq[1:2] * cos + r_sw * gq[2:3] * sin
        for s in range(2):
            hd = 2 * p + s
            keep = lo if s == 0 else jnp.logical_not(lo)
            qn = qa[:, hd * 128:(hd + 1) * 128]
            nq = lax.rsqrt((jnp.sum(qn * qn, axis=-1, keepdims=True) + ss_r[s]) * inv_d + EPS) * (MLA_QK ** -0.5)
            q_ref[0, hd, :, 0:128] = (qn * nq * gq[0:1]).astype(BF16)
            q_ref[0, hd, :, 128:256] = jnp.where(keep, qr * nq, 0.0).astype(BF16)
            kn = kva[:, hd * 128:(hd + 1) * 128]
            nk = lax.rsqrt((jnp.sum(kn * kn, axis=-1, keepdims=True) + ss_kpe) * inv_d + EPS)
            k_ref[0, hd, :, 0:128] = (kn * nk * gk[0:1]).astype(BF16)
            k_ref[0, hd, :, 128:256] = jnp.where(keep, kr * nk, 0.0).astype(BF16)
            v_ref[0, hd] = kva[:, 1024 + hd * 128:1024 + (hd + 1) * 128].astype(BF16)


def _attn_kernel(q_ref, k_ref, v_ref, o_ref, *, tk):
    q = q_ref[0, 0]
    tq = q.shape[0]
    n_kv = k_ref.shape[2] // tk

    def body(j, carry):
        m, l, acc = carry
        start = pl.multiple_of(j * tk, tk)
        kj = k_ref[0, 0, pl.ds(start, tk), :]
        vj = v_ref[0, 0, pl.ds(start, tk), :]
        s = lax.dot_general(q, kj, _NT, preferred_element_type=F32)
        m_new = jnp.maximum(m, jnp.max(s, axis=-1, keepdims=True))
        alpha = jnp.exp(m - m_new)
        p = jnp.exp(s - m_new)
        l = alpha * l + jnp.sum(p, axis=-1, keepdims=True)
        acc = alpha * acc + _dot(p.astype(BF16), vj)
        return m_new, l, acc

    init = (jnp.full((tq, 1), -jnp.inf, F32), jnp.zeros((tq, 1), F32), jnp.zeros((tq, MLA_V), F32))
    m, l, acc = lax.fori_loop(0, n_kv, body, init)
    o_ref[0] = (acc / l).astype(BF16)


def _attn_ctx_kernel(q_ref, k_ref, v_ref, o_ref):
    s = lax.dot_general(q_ref[0, 0], k_ref[0, 0], _NT, preferred_element_type=F32)
    p = jnp.exp(s - jnp.max(s, axis=-1, keepdims=True))
    acc = _dot(p.astype(BF16), v_ref[0, 0])
    o_ref[0] = (acc / jnp.sum(p, axis=-1, keepdims=True)).astype(BF16)


def _odd_out_kernel(x_ref, m_ref, o_ref, oc_ref, wo_ref, n2_ref, rwt_ref, xo_ref, aff_ref, *, n_lat_tiles):
    o = jnp.where(pl.program_id(1) == n_lat_tiles, oc_ref[0], o_ref[0])
    out = _dot(o, wo_ref[...])
    _epilogue(x_ref[0], out, m_ref[0], n2_ref[...], rwt_ref[...], xo_ref, aff_ref)


def _gather_kernel(idx_ref, x_ref, m_ref, n2_ref, o_ref, rows, *, c_lat):
    d = D_MODEL
    c_tot = rows.shape[0]

    def body(r, carry):
        t = idx_ref[0, 0, r]
        rows[pl.ds(r, 1), :] = x_ref[0, pl.ds(t, 1), :]
        return carry

    lax.fori_loop(0, c_tot, body, 0, unroll=8)
    n2 = n2_ref[...]
    for lo_, hi_, mrow in ((0, c_lat, 0), (c_lat, c_tot, 1)):
        m = m_ref[mrow]
        o_ref[0, 0, lo_:hi_, :] = _modulate(rows[lo_:hi_, :], n2, m[:, 3 * d:4 * d], m[:, 4 * d:5 * d]).astype(BF16)


def _ffn_kernel(x_ref, wg_ref, wu_ref, wd_ref, o_ref):
    f = pl.program_id(2)
    x = x_ref[0, 0]
    a = _dot(x, wg_ref[0, 0].astype(BF16))
    u = _dot(x, wu_ref[0, 0].astype(BF16))
    y = _dot((_silu(a) * u).astype(BF16), wd_ref[0, 0].astype(BF16))

    @pl.when(f == 0)
    def _():
        o_ref[0, 0] = y

    @pl.when(f > 0)
    def _():
        o_ref[0, 0] += y


def _combine_kernel(idx_ref, gate_ref, x_hbm, y_ref, m_ref, o_hbm, acc, sem, *, c_lat):
    d = D_MODEL
    b = pl.program_id(0)
    e = pl.program_id(1)
    c_tot = y_ref.shape[2]

    @pl.when(e == 0)
    def _():
        cp = pltpu.make_async_copy(x_hbm.at[b], acc, sem)
        cp.start()
        cp.wait()

    for lo_, hi_, mrow in ((0, c_lat, 0), (c_lat, c_tot, 1)):
        g2 = m_ref[mrow][:, 5 * d:6 * d]

        def body(r, carry, g2=g2):
            t = idx_ref[0, 0, r]
            acc[pl.ds(t, 1), :] += y_ref[0, 0, pl.ds(r, 1), :] * (gate_ref[0, 0, r] * g2)
            return carry

        lax.fori_loop(lo_, hi_, body, 0, unroll=8)

    @pl.when(e == pl.num_programs(1) - 1)
    def _():
        cp = pltpu.make_async_copy(acc, o_hbm.at[b], sem)
        cp.start()
        cp.wait()


def _moe(xs, aff_t, modtab, n2g, l, w_gate, w_up, w_down, n_lat):
    bsz, t_all, d = xs.shape
    n_ctx = t_all - n_lat
    ne = N_EXPERTS
    c_lat = max(1, (EC_CAPACITY_FACTOR * n_lat) // ne)
    c_ctx = max(1, (EC_CAPACITY_FACTOR * n_ctx) // ne)
    c_tot = c_lat + c_ctx
    g_lat, i_lat = lax.top_k(aff_t[:, :, :n_lat], c_lat)
    g_ctx, i_ctx = lax.top_k(aff_t[:, :, n_lat:], c_ctx)
    idx = jnp.concatenate([i_lat, i_ctx + n_lat], axis=-1).astype(I32).reshape(bsz * ne, 1, c_tot)
    gates = jnp.concatenate([g_lat, g_ctx], axis=-1).reshape(bsz * ne, 1, c_tot)
    smem = functools.partial(pl.BlockSpec, memory_space=pltpu.SMEM)

    xg = pl.pallas_call(
        functools.partial(_gather_kernel, c_lat=c_lat),
        grid=(bsz, ne),
        in_specs=[smem((1, 1, c_tot), lambda b, e: (b * ne + e, 0, 0)),
                  pl.BlockSpec((1, t_all, d), lambda b, e: (b, 0, 0), pipeline_mode=pl.Buffered(1)),
                  pl.BlockSpec((2, 1, 6 * d), lambda b, e: (b, 0, 0)),
                  pl.BlockSpec((1, d), lambda b, e: (0, 0))],
        out_specs=pl.BlockSpec((1, 1, c_tot, d), lambda b, e: (b, e, 0, 0)),
        out_shape=jax.ShapeDtypeStruct((bsz, ne, c_tot, d), BF16),
        scratch_shapes=[pltpu.VMEM((c_tot, d), F32)],
        compiler_params=_cparams(("arbitrary", "arbitrary"), VMEM_LIMIT),
        name="moe_gather",
    )(idx, xs, modtab, n2g)

    nf = EXPERT_FF // FF_TILE
    y = pl.pallas_call(
        _ffn_kernel,
        grid=(bsz, ne, nf),
        in_specs=[pl.BlockSpec((1, 1, c_tot, d), lambda b, e, f: (b, e, 0, 0)),
                  pl.BlockSpec((1, 1, d, FF_TILE), lambda b, e, f: (l, e, 0, f)),
                  pl.BlockSpec((1, 1, d, FF_TILE), lambda b, e, f: (l, e, 0, f)),
                  pl.BlockSpec((1, 1, FF_TILE, d), lambda b, e, f: (l, e, f, 0))],
        out_specs=pl.BlockSpec((1, 1, c_tot, d), lambda b, e, f: (b, e, 0, 0)),
        out_shape=jax.ShapeDtypeStruct((bsz, ne, c_tot, d), F32),
        compiler_params=_cparams(("arbitrary", "arbitrary", "arbitrary"), VMEM_LIMIT),
        name="moe_ffn",
    )(xg, w_gate, w_up, w_down)

    return pl.pallas_call(
        functools.partial(_combine_kernel, c_lat=c_lat),
        grid=(bsz, ne),
        in_specs=[smem((1, 1, c_tot), lambda b, e: (b * ne + e, 0, 0)),
                  smem((1, 1, c_tot), lambda b, e: (b * ne + e, 0, 0)),
                  pl.BlockSpec(memory_space=pl.ANY),
                  pl.BlockSpec((1, 1, c_tot, d), lambda b, e: (b, e, 0, 0)),
                  pl.BlockSpec((2, 1, 6 * d), lambda b, e: (b, 0, 0))],
        out_specs=pl.BlockSpec(memory_space=pl.ANY),
        out_shape=jax.ShapeDtypeStruct(xs.shape, F32),
        scratch_shapes=[pltpu.VMEM((t_all, d), F32), pltpu.SemaphoreType.DMA(())],
        input_output_aliases={2: 0},
        compiler_params=_cparams(("arbitrary", "arbitrary"), VMEM_LIMIT),
        name="moe_combine",
    )(idx, gates, xs, y, modtab)


def _tile_specs(d, n_lat_tiles):
    x_spec = pl.BlockSpec((1, TM, d), lambda b, i: (b, i, 0))
    m_spec = pl.BlockSpec((1, 1, 6 * d), lambda b, i: (2 * b + i // n_lat_tiles, 0, 0))
    return x_spec, m_spec


def _const_spec(shape):
    return pl.BlockSpec(shape, lambda b, i: (0,) * len(shape))


def _even_layer(xs, modtab, n1g, n2g, rwt, w_in_ext, lg, gn_g, conv_w, conv_b, ln_g, ln_b, w_out, cos, sin, n_lat):
    bsz, t_all, d = xs.shape
    nt = t_all // TM
    nlt = n_lat // TM
    x_spec, m_spec = _tile_specs(d, nlt)
    n_cols = 7 * 512
    proj = pl.pallas_call(
        _even_in_kernel,
        grid=(bsz, nt),
        in_specs=[x_spec, m_spec, _const_spec((1, d)), _const_spec(w_in_ext.shape),
                  pl.BlockSpec((TM, 512), lambda b, i: (i, 0)), pl.BlockSpec((TM, 512), lambda b, i: (i, 0))],
        out_specs=pl.BlockSpec((1, TM, n_cols), lambda b, i: (b, i, 0)),
        out_shape=jax.ShapeDtypeStruct((bsz, t_all, n_cols), BF16),
        compiler_params=_cparams(("arbitrary", "arbitrary"), VMEM_LIMIT),
        name="even_in",
    )(xs, modtab, n1g, w_in_ext, cos, sin)

    fwd = lambda s: jnp.where(s == 0, nlt, s - 1)
    bwd = lambda s: jnp.where(s == 0, nlt, nlt - s)

    def col(order, j):
        return pl.BlockSpec((1, TM, 512), lambda b, s: (b, order(s), j))

    o_f, o_b = pl.pallas_call(
        _ret_kernel,
        grid=(bsz, nt),
        in_specs=[pl.BlockSpec(lg.shape, lambda b, s: (0, 0, 0, 0)),
                  col(fwd, 0), col(fwd, 1), col(fwd, 2), col(bwd, 0), col(bwd, 1), col(bwd, 2)],
        out_specs=[col(fwd, 0), col(bwd, 0)],
        out_shape=[jax.ShapeDtypeStruct((bsz, t_all, 512), BF16)] * 2,
        scratch_shapes=[pltpu.VMEM((2, RET_HEADS, RET_DK, 128), F32),
                        pltpu.VMEM((2, RET_HEADS, RET_CHUNK, RET_CHUNK), F32),
                        pltpu.VMEM((2, RET_HEADS, RET_CHUNK, 128), F32),
                        pltpu.VMEM((2, RET_HEADS, RET_CHUNK, 128), F32),
                        pltpu.VMEM((2, RET_HEADS, 1, 128), F32)],
        compiler_params=_cparams(("arbitrary", "arbitrary"), VMEM_LIMIT),
        name="retention",
    )(lg, proj, proj, proj, proj, proj, proj)

    hpt = TM // CONV_HALO
    n_halo = t_all // CONV_HALO

    def tcol(j):
        return pl.BlockSpec((1, TM, 512), lambda b, i: (b, i, j))

    def halo(j, nxt):
        if nxt:
            return pl.BlockSpec((1, CONV_HALO, 512), lambda b, i: (b, jnp.minimum((i + 1) * hpt, n_halo - 1), j))
        return pl.BlockSpec((1, CONV_HALO, 512), lambda b, i: (b, jnp.maximum(i * hpt - 1, 0), j))

    xs_new, aff_t = pl.pallas_call(
        functools.partial(_even_out_kernel, n_lat_tiles=nlt),
        grid=(bsz, nt),
        in_specs=[x_spec, m_spec, pl.BlockSpec((1, TM, 512), lambda b, i: (b, i, 0)),
                  pl.BlockSpec((1, TM, 512), lambda b, i: (b, i, 0)),
                  tcol(3), tcol(4), tcol(5), tcol(6), halo(5, False), halo(6, False), halo(5, True), halo(6, True),
                  _const_spec((1, 512)), _const_spec(conv_w.shape), _const_spec((1, 512)), _const_spec((1, 512)),
                  _const_spec((1, 512)), _const_spec(w_out.shape), _const_spec((1, d)), _const_spec(rwt.shape)],
        out_specs=[x_spec, pl.BlockSpec((1, N_EXPERTS, TM), lambda b, i: (b, 0, i))],
        out_shape=[jax.ShapeDtypeStruct(xs.shape, F32), jax.ShapeDtypeStruct((bsz, N_EXPERTS, t_all), F32)],
        scratch_shapes=[pltpu.VMEM((TM + 2 * CONV_HALO, CONV_CH), F32)],
        input_output_aliases={0: 0},
        compiler_params=_cparams(("arbitrary", "arbitrary"), VMEM_LIMIT),
        name="even_out",
    )(xs, modtab, o_f, o_b, proj, proj, proj, proj, proj, proj, proj, proj,
      gn_g, conv_w, conv_b, ln_g, ln_b, w_out, n2g, rwt)
    return xs_new, aff_t


def _odd_layer(xs, modtab, n1g, n2g, rwt, w_in_ext, qag, kvag, w_uq_ext, w_ukv_p, gq, gk, w_out, cos, sin, n_lat):
    bsz, t_all, d = xs.shape
    nt = t_all // TM
    nlt = n_lat // TM
    nh = MLA_HEADS
    x_spec, m_spec = _tile_specs(d, nlt)
    q, k, v = pl.pallas_call(
        _odd_in_kernel,
        grid=(bsz, nt),
        in_specs=[x_spec, m_spec, _const_spec((1, d)), _const_spec(w_in_ext.shape), _const_spec(qag.shape),
                  _const_spec(kvag.shape), _const_spec(w_uq_ext.shape), _const_spec(w_ukv_p.shape),
                  _const_spec(gq.shape), _const_spec(gk.shape),
                  pl.BlockSpec((TM, 128), lambda b, i: (i, 0)), pl.BlockSpec((TM, 128), lambda b, i: (i, 0))],
        out_specs=[pl.BlockSpec((1, nh, TM, MLA_QK_PAD), lambda b, i: (b, 0, i, 0)),
                   pl.BlockSpec((1, nh, TM, MLA_QK_PAD), lambda b, i: (b, 0, i, 0)),
                   pl.BlockSpec((1, nh, TM, MLA_V), lambda b, i: (b, 0, i, 0))],
        out_shape=[jax.ShapeDtypeStruct((bsz, nh, t_all, MLA_QK_PAD), BF16),
                   jax.ShapeDtypeStruct((bsz, nh, t_all, MLA_QK_PAD), BF16),
                   jax.ShapeDtypeStruct((bsz, nh, t_all, MLA_V), BF16)],
        compiler_params=_cparams(("arbitrary", "arbitrary"), VMEM_LIMIT),
        name="odd_in",
    )(xs, modtab, n1g, w_in_ext, qag, kvag, w_uq_ext, w_ukv_p, gq, gk, cos, sin)

    tk = next(c for c in (768, 512, 256) if t_all % c == 0)
    o = pl.pallas_call(
        functools.partial(_attn_kernel, tk=tk),
        grid=(bsz, nh, n_lat // ATT_TQ),
        in_specs=[pl.BlockSpec((1, 1, ATT_TQ, MLA_QK_PAD), lambda b, h, i: (b, h, i, 0)),
                  pl.BlockSpec((1, 1, t_all, MLA_QK_PAD), lambda b, h, i: (b, h, 0, 0)),
                  pl.BlockSpec((1, 1, t_all, MLA_V), lambda b, h, i: (b, h, 0, 0))],
        out_specs=pl.BlockSpec((1, ATT_TQ, MLA_V), lambda b, h, i: (b, i, h)),
        out_shape=jax.ShapeDtypeStruct((bsz, n_lat, nh * MLA_V), BF16),
        compiler_params=_cparams(("arbitrary", "arbitrary", "arbitrary"), VMEM_LIMIT),
        name="mla_attention",
    )(q, k, v)
    o_ctx = pl.pallas_call(
        _attn_ctx_kernel,
        grid=(bsz, nh),
        in_specs=[pl.BlockSpec((1, 1, TM, MLA_QK_PAD), lambda b, h: (b, h, nlt, 0)),
                  pl.BlockSpec((1, 1, TM, MLA_QK_PAD), lambda b, h: (b, h, nlt, 0)),
                  pl.BlockSpec((1, 1, TM, MLA_V), lambda b, h: (b, h, nlt, 0))],
        out_specs=pl.BlockSpec((1, TM, MLA_V), lambda b, h: (b, 0, h)),
        out_shape=jax.ShapeDtypeStruct((bsz, TM, nh * MLA_V), BF16),
        compiler_params=_cparams(("arbitrary", "arbitrary"), VMEM_LIMIT),
        name="mla_attention_ctx",
    )(q, k, v)

    xs_new, aff_t = pl.pallas_call(
        functools.partial(_odd_out_kernel, n_lat_tiles=nlt),
        grid=(bsz, nt),
        in_specs=[x_spec, m_spec, pl.BlockSpec((1, TM, nh * MLA_V), lambda b, i: (b, jnp.minimum(i, nlt - 1), 0)),
                  pl.BlockSpec((1, TM, nh * MLA_V), lambda b, i: (b, 0, 0)),
                  _const_spec(w_out.shape), _const_spec((1, d)), _const_spec(rwt.shape)],
        out_specs=[x_spec, pl.BlockSpec((1, N_EXPERTS, TM), lambda b, i: (b, 0, i))],
        out_shape=[jax.ShapeDtypeStruct(xs.shape, F32), jax.ShapeDtypeStruct((bsz, N_EXPERTS, t_all), F32)],
        input_output_aliases={0: 0},
        compiler_params=_cparams(("arbitrary", "arbitrary"), VMEM_LIMIT),
        name="odd_out",
    )(xs, modtab, o, o_ctx, w_out, n2g, rwt)
    return xs_new, aff_t


def _rope_tables(n_lat, n_ctx, nf):
    t = jnp.arange(n_lat)
    row = (t // GRID_W).astype(F32)
    col = (t % GRID_W).astype(F32)
    inv = ROPE_BASE ** (-(jnp.arange(nf, dtype=F32) / nf))
    ar = row[:, None] * inv[None, :]
    ac = col[:, None] * inv[None, :]
    cos = jnp.concatenate([jnp.cos(ar), jnp.cos(ar), jnp.cos(ac), jnp.cos(ac)], axis=-1)
    sin = jnp.concatenate([-jnp.sin(ar), jnp.sin(ar), -jnp.sin(ac), jnp.sin(ac)], axis=-1)
    cos = jnp.concatenate([cos, jnp.ones((n_ctx, 4 * nf), F32)], axis=0)
    sin = jnp.concatenate([sin, jnp.zeros((n_ctx, 4 * nf), F32)], axis=0)
    return cos, sin


def _swap_perm(nf):
    a = jnp.arange(nf)
    return jnp.concatenate([a + nf, a, a + 3 * nf, a + 2 * nf])


def kernel(x, c, ctx, c_ctx, mod_w, mod_b, norm1_g, norm2_g, ev_w_in, ret_decay_f, ret_decay_b, ret_gn_g, conv_w, conv_b, conv_ln_g, conv_ln_b, ev_w_out, od_w_in, mla_q_a_g, mla_kv_a_g, mla_w_uq, mla_w_ukv, mla_qk_g_q, mla_qk_g_k, od_w_out, router_w, moe_w_gate, moe_w_up, moe_w_down):
    bsz, n_lat, d = x.shape
    n_ctx = ctx.shape[1]
    depth = mod_w.shape[0]
    assert d == D_MODEL and n_ctx == TM and n_lat % ATT_TQ == 0 and bsz + 1 <= 8

    xs = jnp.concatenate([x, ctx], axis=1)
    cc = jnp.zeros((8, d), F32).at[:bsz].set(c).at[bsz].set(c_ctx)
    mod = _mod_all(cc, mod_w, mod_b)

    cos_e, sin_e = _rope_tables(n_lat, n_ctx, RET_DK // 4)
    cos_e, sin_e = jnp.tile(cos_e, (1, RET_HEADS)), jnp.tile(sin_e, (1, RET_HEADS))
    cos_o, sin_o = _rope_tables(n_lat, n_ctx, MLA_ROPE // 4)
    cos_o, sin_o = jnp.tile(cos_o, (1, 2)), jnp.tile(sin_o, (1, 2))
    perm_e = (jnp.arange(RET_HEADS)[:, None] * RET_DK + _swap_perm(RET_DK // 4)[None, :]).reshape(-1)
    perm_o = _swap_perm(MLA_ROPE // 4)

    for l in range(depth):
        i = l // 2
        modtab = jnp.stack([mod[l, :bsz], jnp.broadcast_to(mod[l, bsz], (bsz, 6 * d))], axis=1).reshape(bsz * 2, 1, 6 * d)
        n1g = norm1_g[l][None]
        n2g = norm2_g[l][None]
        rwt = router_w[l].T
        if l % 2 == 0:
            w = ev_w_in[i]
            w_ext = jnp.concatenate([w, w[:, 0:512][:, perm_e], w[:, 512:1024][:, perm_e]], axis=1).astype(BF16)
            lg = jnp.broadcast_to(jnp.stack([ret_decay_f[i], ret_decay_b[i]])[:, :, None, None], (2, RET_HEADS, 1, 128))
            cw = jnp.concatenate([conv_w[i], jnp.zeros((1, CONV_CH), F32)], axis=0)
            xs, aff_t = _even_layer(xs, modtab, n1g, n2g, rwt, w_ext, lg, ret_gn_g[i][None], cw, conv_b[i][None],
                                    conv_ln_g[i][None], conv_ln_b[i][None], ev_w_out[i].astype(BF16), cos_e, sin_e, n_lat)
        else:
            w = od_w_in[i]
            kpe = w[:, MLA_Q_LORA + MLA_KV_LORA:]
            w_ext = jnp.concatenate([w[:, :MLA_Q_LORA + MLA_KV_LORA], kpe, kpe, kpe[:, perm_o], kpe[:, perm_o]], axis=1).astype(BF16)
            wq = mla_w_uq[i].reshape(MLA_Q_LORA, MLA_HEADS, MLA_QK)
            wq_r = wq[:, :, MLA_NOPE:]
            w_uq_ext = jnp.concatenate([wq[:, :, :MLA_NOPE].reshape(MLA_Q_LORA, -1), wq_r.reshape(MLA_Q_LORA, -1),
                                        wq_r[:, :, perm_o].reshape(MLA_Q_LORA, -1)], axis=1).astype(BF16)
            wkv = mla_w_ukv[i].reshape(MLA_KV_LORA, MLA_HEADS, MLA_NOPE + MLA_V)
            w_ukv_p = jnp.concatenate([wkv[:, :, :MLA_NOPE].reshape(MLA_KV_LORA, -1),
                                       wkv[:, :, MLA_NOPE:].reshape(MLA_KV_LORA, -1)], axis=1).astype(BF16)

            def gains(g):
                gr = g[MLA_NOPE:]
                return jnp.stack([g[:MLA_NOPE], jnp.tile(gr, 2), jnp.tile(gr[perm_o], 2)])

            xs, aff_t = _odd_layer(xs, modtab, n1g, n2g, rwt, w_ext, mla_q_a_g[i][None], mla_kv_a_g[i][None], w_uq_ext,
                                   w_ukv_p, gains(mla_qk_g_q[i]), gains(mla_qk_g_k[i]), od_w_out[i].astype(BF16),
                                   cos_o, sin_o, n_lat)
        xs = _moe(xs, aff_t, modtab, n2g, l, moe_w_gate, moe_w_up, moe_w_down, n_lat)
    return xs[:, :n_lat]
```

```python
import functools

import jax
import jax.numpy as jnp
from jax import lax
from jax.experimental import pallas as pl
from jax.experimental.pallas import tpu as pltpu

F32 = jnp.float32
BF16 = jnp.bfloat16
I32 = jnp.int32

D_MODEL = 1024
GRID_W = 64
RET_HEADS = 4
RET_DK = 128
RET_CHUNK = 128
CONV_CH = 512
CONV_WIDTH = 31
CONV_HALO = 16
MLA_HEADS = 8
MLA_Q_LORA = 512
MLA_KV_LORA = 256
MLA_NOPE = 128
MLA_ROPE = 64
MLA_V = 128
MLA_QK = MLA_NOPE + MLA_ROPE
MLA_QK_PAD = 256
N_EXPERTS = 16
EC_CAPACITY_FACTOR = 2
EXPERT_FF = 1024
ROPE_BASE = 10000.0
LOG2E = 1.4426950408889634
EPS = 1e-6

TM = 256
ATT_TQ = 512
ATT_SUB = 256
ATT_KT_PER_STEP = (3, 2, 1)
FF_TILE = 512
ROW_GROUP = 8
VMEM_LIMIT = 56 * 2 ** 20

_NT = (((1,), (1,)), ((), ()))


def _cparams(sem, vmem=None):
    return pltpu.CompilerParams(dimension_semantics=sem, vmem_limit_bytes=vmem)


def _dot(a, b):
    return jnp.dot(a, b, preferred_element_type=F32)


def _split3(a):
    a1 = a.astype(BF16)
    r = a - a1.astype(F32)
    a2 = r.astype(BF16)
    a3 = (r - a2.astype(F32)).astype(BF16)
    return a1, a2, a3


def _dot_hi(a, b, dn):
    a1, a2, a3 = _split3(a)
    b1, b2, b3 = _split3(b)
    d = lambda x, y: lax.dot_general(x, y, dn, preferred_element_type=F32)
    return ((d(a3, b1) + d(a2, b2) + d(a1, b3)) + (d(a2, b1) + d(a1, b2))) + d(a1, b1)


def _silu(a):
    return a * jax.nn.sigmoid(a)


def _rms(x, g):
    return x * lax.rsqrt(jnp.mean(x * x, axis=-1, keepdims=True) + EPS) * g


def _modulate(x, g, shift, scale):
    return _rms(x, g) * (1.0 + scale) + shift


def _mod_kernel(c_ref, w_ref, b_ref, o_ref):
    s = _silu(c_ref[...])
    o_ref[0] = _dot_hi(s, w_ref[0], (((1,), (0,)), ((), ()))) + b_ref[0]


def _mod_all(cc, mod_w, mod_b):
    depth, d, n6 = mod_w.shape
    tn = 1536
    return pl.pallas_call(
        _mod_kernel,
        grid=(depth, n6 // tn),
        in_specs=[pl.BlockSpec((8, d), lambda l, j: (0, 0)),
                  pl.BlockSpec((1, d, tn), lambda l, j: (l, 0, j)),
                  pl.BlockSpec((1, 1, tn), lambda l, j: (l, 0, j))],
        out_specs=pl.BlockSpec((1, 8, tn), lambda l, j: (l, 0, j)),
        out_shape=jax.ShapeDtypeStruct((depth, 8, n6), F32),
        compiler_params=_cparams(("arbitrary", "arbitrary"), VMEM_LIMIT),
        name="mod_all",
    )(cc, mod_w, mod_b.reshape(depth, 1, n6))


def _epilogue(x, out, m, n2g, rwt, xo_ref, aff_ref):
    d = D_MODEL
    xn = x + m[:, 2 * d:3 * d] * out
    xo_ref[0] = xn
    h2 = _modulate(xn, n2g, m[:, 3 * d:4 * d], m[:, 4 * d:5 * d])
    h_hi = h2.astype(BF16)
    h_lo = (h2 - h_hi.astype(F32)).astype(BF16)
    z = _dot(h_hi, rwt)
    logits = (z[:, 0:128] + z[:, 128:256]) + _dot(h_lo, rwt[:, 0:128])
    lane = lax.broadcasted_iota(I32, logits.shape, 1)
    logits = jnp.where(lane < N_EXPERTS, logits, -jnp.inf)
    e = jnp.exp(logits - jnp.max(logits, axis=-1, keepdims=True))
    aff = e / jnp.sum(e, axis=-1, keepdims=True)
    aff_ref[0] = aff.T[0:N_EXPERTS, :]


def _even_in_kernel(x_ref, m_ref, g_ref, w_ref, cos_ref, sin_ref, o_ref):
    d = D_MODEL
    m = m_ref[0]
    h = _modulate(x_ref[0], g_ref[...], m[:, 0:d], m[:, d:2 * d]).astype(BF16)

    def grp(j):
        return _dot(h, w_ref[:, j * 512:(j + 1) * 512])

    c = cos_ref[...]
    s = sin_ref[...]
    o_ref[0, :, 0:512] = (grp(0) * c + grp(7) * s).astype(BF16)
    o_ref[0, :, 512:1024] = ((grp(1) * c + grp(8) * s) * (RET_DK ** -0.5)).astype(BF16)
    for j in range(2, 7):
        o_ref[0, :, j * 512:(j + 1) * 512] = grp(j).astype(BF16)


def _ret_kernel(lg_ref, qf_ref, kf_ref, vf_ref, qb_ref, kb_ref, vb_ref, of_ref, ob_ref,
                st_ref, intra_ref, xi_ref, zeta_ref, dc_ref):
    L = RET_CHUNK
    nc = TM // L

    @pl.when(pl.program_id(1) == 0)
    def _():
        st_ref[...] = jnp.zeros(st_ref.shape, F32)
        ii = lax.broadcasted_iota(I32, (L, L), 0).astype(F32)
        jj = lax.broadcasted_iota(I32, (L, L), 1).astype(F32)
        for dr in range(2):
            for h in range(RET_HEADS):
                x = lg_ref[dr, h]
                lg = jnp.minimum(x, 0.0) - jnp.log(1.0 + jnp.exp(-jnp.abs(x)))
                if dr == 0:
                    diff, xpos, zpos = ii - jj, ii + 1.0, (L - 1.0) - ii
                else:
                    diff, xpos, zpos = jj - ii, L - ii, ii
                intra_ref[dr, h] = jnp.where(diff >= 0, jnp.exp(lg * jnp.maximum(diff, 0.0)), 0.0)
                xi_ref[dr, h] = jnp.exp(lg * xpos)
                zeta_ref[dr, h] = jnp.exp(lg * zpos)
                dc_ref[dr, h] = jnp.exp(lg * float(L))

    for dr, (q_ref, k_ref, v_ref, o_ref) in enumerate(((qf_ref, kf_ref, vf_ref, of_ref),
                                                      (qb_ref, kb_ref, vb_ref, ob_ref))):
        order = range(nc) if dr == 0 else range(nc - 1, -1, -1)
        for c in order:
            for h in range(RET_HEADS):
                rs = slice(c * L, (c + 1) * L)
                cs = slice(h * 128, (h + 1) * 128)
                qc = q_ref[0, rs, cs]
                kc = k_ref[0, rs, cs]
                vc = v_ref[0, rs, cs]
                st = st_ref[dr, h]
                sc = lax.dot_general(qc, kc, _NT, preferred_element_type=F32) * intra_ref[dr, h]
                o = _dot(sc.astype(BF16), vc) + _dot(qc, st.astype(BF16)) * xi_ref[dr, h]
                kz = (kc.astype(F32) * zeta_ref[dr, h]).T.astype(BF16)
                st_ref[dr, h] = st * dc_ref[dr, h] + _dot(kz, vc)
                o_ref[0, rs, cs] = o.astype(BF16)


def _even_out_kernel(x_ref, m_ref, of_ref, ob_ref, gf_ref, gb_ref, bv_ref, bg_ref,
                     pv_ref, pg_ref, nv_ref, ng_ref, gn_ref, cw_ref, cb_ref, lng_ref, lnb_ref,
                     wo_ref, n2_ref, rwt_ref, xo_ref, aff_ref, u_scr, *, n_lat_tiles):
    i = pl.program_id(1)
    hl = CONV_HALO

    def glu(v_ref, g_ref):
        return v_ref[0].astype(F32) * jax.nn.sigmoid(g_ref[0].astype(F32))

    prev_ok = jnp.logical_and(i >= 1, i < n_lat_tiles)
    next_ok = i < n_lat_tiles - 1
    u_scr[0:hl, :] = jnp.where(prev_ok, glu(pv_ref, pg_ref), 0.0)
    u_scr[hl:hl + TM, :] = glu(bv_ref, bg_ref)
    u_scr[hl + TM:2 * hl + TM, :] = jnp.where(next_ok, glu(nv_ref, ng_ref), 0.0)
    y = jnp.zeros((TM, CONV_CH), F32) + cb_ref[...]
    off = hl - CONV_WIDTH // 2
    for k in range(CONV_WIDTH):
        y = y + cw_ref[k:k + 1, :] * u_scr[k + off:k + off + TM, :]
    mu = jnp.mean(y, axis=-1, keepdims=True)
    yc = y - mu
    var = jnp.mean(yc * yc, axis=-1, keepdims=True)
    conv = _silu(yc * lax.rsqrt(var + EPS) * lng_ref[...] + lnb_ref[...])

    def gnorm(o_ref, g_ref):
        parts = []
        for h in range(RET_HEADS):
            cs = slice(h * 128, (h + 1) * 128)
            o = o_ref[0, :, cs].astype(F32)
            mu_h = jnp.mean(o, axis=-1, keepdims=True)
            oc = o - mu_h
            var_h = jnp.mean(oc * oc, axis=-1, keepdims=True)
            parts.append(oc * lax.rsqrt(var_h + EPS) * gn_ref[:, cs] * _silu(g_ref[0, :, cs].astype(F32)))
        return parts

    pf = gnorm(of_ref, gf_ref)
    pb = gnorm(ob_ref, gb_ref)
    out = _dot(conv.astype(BF16), wo_ref[512:1024, :])
    for h in range(RET_HEADS):
        out = out + _dot((pf[h] + pb[h]).astype(BF16), wo_ref[h * 128:(h + 1) * 128, :])
    _epilogue(x_ref[0], out, m_ref[0], n2_ref[...], rwt_ref[...], xo_ref, aff_ref)


def _odd_in_kernel(x_ref, m_ref, g_ref, win_ref, qag_ref, kvag_ref, wuq_ref, wukv_ref,
                   gq_ref, gk_ref, cos_ref, sin_ref, q_ref, kt_ref, v_ref):
    d = D_MODEL
    nh = MLA_HEADS
    m = m_ref[0]
    h = _modulate(x_ref[0], g_ref[...], m[:, 0:d], m[:, d:2 * d]).astype(BF16)
    c = _dot(h, win_ref[...])
    cq = _rms(c[:, 0:512], qag_ref[...]).astype(BF16)
    ckv = _rms(c[:, 512:768], kvag_ref[...]).astype(BF16)
    kpe = c[:, 768:896]
    kpe_sw = c[:, 896:1024]
    qa = _dot(cq, wuq_ref[...])
    kva = _dot(ckv, wukv_ref[...])
    cos = cos_ref[...]
    sin = sin_ref[...]
    gq = gq_ref[...]
    gk = gk_ref[...]
    lane = lax.broadcasted_iota(I32, (TM, 128), 1)
    lo = lane < MLA_ROPE
    inv_d = 1.0 / MLA_QK

    ss_kpe = 0.5 * jnp.sum(kpe * kpe, axis=-1, keepdims=True)
    kr = kpe * gk[1:2] * cos + kpe_sw * gk[2:3] * sin
    for p in range(nh // 2):
        r = qa[:, 1024 + p * 128:1024 + (p + 1) * 128]
        r_sw = qa[:, 1536 + p * 128:1536 + (p + 1) * 128]
        r2 = r * r
        ss_r = (jnp.sum(jnp.where(lo, r2, 0.0), axis=-1, keepdims=True),
                jnp.sum(jnp.where(lo, 0.0, r2), axis=-1, keepdims=True))
        qr = r * gq[1:2] * cos + r_sw * gq[2:3] * sin
        for s in range(2):
            hd = 2 * p + s
            keep = lo if s == 0 else jnp.logical_not(lo)
            qn = qa[:, hd * 128:(hd + 1) * 128]
            nq = lax.rsqrt((jnp.sum(qn * qn, axis=-1, keepdims=True) + ss_r[s]) * inv_d + EPS) * (MLA_QK ** -0.5 * LOG2E)
            q_ref[0, hd, :, 0:128] = (qn * nq * gq[0:1]).astype(BF16)
            q_ref[0, hd, :, 128:256] = jnp.where(keep, qr * nq, 0.0).astype(BF16)
            kn = kva[:, hd * 128:(hd + 1) * 128]
            nk = lax.rsqrt((jnp.sum(kn * kn, axis=-1, keepdims=True) + ss_kpe) * inv_d + EPS)
            k_full = jnp.concatenate([kn * nk * gk[0:1], jnp.where(keep, kr * nk, 0.0)], axis=-1)
            kt_ref[0, hd, 0] = k_full.T.astype(BF16)
            v_ref[0, hd, :, 0:MLA_V] = kva[:, 1024 + hd * 128:1024 + (hd + 1) * 128].astype(BF16)
            v_ref[0, hd, :, MLA_V:2 * MLA_V] = jnp.ones((TM, MLA_V), BF16)


def _attn_kernel(q_ref, kt_ref, v_ref, o_ref, s_scr, m_scr, acc_scr, *, tps):
    tq = q_ref.shape[2]
    nsub = tq // ATT_SUB
    kw = kt_ref.shape[4]
    n_steps = kt_ref.shape[2] // tps

    def qk(step, buf):
        for s in range(nsub):
            q = q_ref[0, 0, s * ATT_SUB:(s + 1) * ATT_SUB, :]
            for c in range(tps):
                s_scr[buf, s, :, c * kw:(c + 1) * kw] = _dot(q, kt_ref[0, 0, step * tps + c])

    def smpv(step, buf):
        vj = v_ref[0, 0, pl.ds(pl.multiple_of(step * (tps * kw), tps * kw), tps * kw), :]
        for s in range(nsub):
            sv = s_scr[buf, s]
            m = m_scr[s]
            m_new = jnp.maximum(m, jnp.max(sv, axis=-1, keepdims=True))
            alpha = jnp.exp2(m - m_new)
            p = jnp.exp2(sv - m_new).astype(BF16)
            acc_scr[s] = alpha * acc_scr[s] + _dot(p, vj)
            m_scr[s] = m_new

    m_scr[...] = jnp.full(m_scr.shape, -jnp.inf, F32)
    acc_scr[...] = jnp.zeros(acc_scr.shape, F32)
    qk(0, 0)

    def body(i, carry):
        qk(2 * i + 1, 1)
        smpv(2 * i, 0)
        qk(2 * i + 2, 0)
        smpv(2 * i + 1, 1)
        return carry

    lax.fori_loop(0, (n_steps - 1) // 2, body, 0)
    if n_steps % 2 == 1:
        smpv(n_steps - 1, 0)
    else:
        qk(n_steps - 1, 1)
        smpv(n_steps - 2, 0)
        smpv(n_steps - 1, 1)
    for s in range(nsub):
        acc = acc_scr[s]
        o_ref[0, s * ATT_SUB:(s + 1) * ATT_SUB, :] = (acc[:, 0:MLA_V] / acc[:, MLA_V:2 * MLA_V]).astype(BF16)


def _attn_ctx_kernel(q_ref, kt_ref, v_ref, o_ref):
    s = _dot(q_ref[0, 0], kt_ref[0, 0, 0])
    p = jnp.exp2(s - jnp.max(s, axis=-1, keepdims=True))
    acc = _dot(p.astype(BF16), v_ref[0, 0])
    o_ref[0] = (acc[:, 0:MLA_V] / acc[:, MLA_V:2 * MLA_V]).astype(BF16)


def _odd_out_kernel(x_ref, m_ref, o_ref, oc_ref, wo_ref, n2_ref, rwt_ref, xo_ref, aff_ref, *, n_lat_tiles):
    o = jnp.where(pl.program_id(1) == n_lat_tiles, oc_ref[0], o_ref[0])
    out = _dot(o, wo_ref[...])
    _epilogue(x_ref[0], out, m_ref[0], n2_ref[...], rwt_ref[...], xo_ref, aff_ref)


def _gather_kernel(idx_ref, x_ref, m_ref, n2_ref, o_ref, rows, *, c_lat):
    d = D_MODEL
    c_tot = rows.shape[0]

    def body(r, carry):
        t = idx_ref[0, 0, r]
        rows[pl.ds(r, 1), :] = x_ref[0, pl.ds(t, 1), :]
        return carry

    lax.fori_loop(0, c_tot, body, 0, unroll=8)
    n2 = n2_ref[...]
    for lo_, hi_, mrow in ((0, c_lat, 0), (c_lat, c_tot, 1)):
        m = m_ref[mrow]
        o_ref[0, 0, lo_:hi_, :] = _modulate(rows[lo_:hi_, :], n2, m[:, 3 * d:4 * d], m[:, 4 * d:5 * d]).astype(BF16)


def _ffn_kernel(x_ref, wg_ref, wu_ref, wd_ref, o_ref):
    f = pl.program_id(2)
    x = x_ref[0, 0]
    a = _dot(x, wg_ref[0, 0].astype(BF16))
    u = _dot(x, wu_ref[0, 0].astype(BF16))
    y = _dot((_silu(a) * u).astype(BF16), wd_ref[0, 0].astype(BF16))

    @pl.when(f == 0)
    def _():
        o_ref[0, 0] = y

    @pl.when(f > 0)
    def _():
        o_ref[0, 0] += y


def _combine_kernel(idx_ref, gate_ref, x_hbm, y_ref, m_ref, o_hbm, acc, sem, *, c_lat):
    d = D_MODEL
    b = pl.program_id(0)
    e = pl.program_id(1)
    c_tot = y_ref.shape[2]

    @pl.when(e == 0)
    def _():
        cp = pltpu.make_async_copy(x_hbm.at[b], acc, sem)
        cp.start()
        cp.wait()

    for lo_, hi_, mrow in ((0, c_lat, 0), (c_lat, c_tot, 1)):
        g2 = m_ref[mrow][:, 5 * d:6 * d]

        def body(i, carry, g2=g2, lo_=lo_):
            base = pl.multiple_of(lo_ + i * ROW_GROUP, ROW_GROUP)
            ts = [idx_ref[0, 0, base + k] for k in range(ROW_GROUP)]
            new = [acc[pl.ds(ts[k], 1), :] + y_ref[0, 0, pl.ds(base + k, 1), :] * (gate_ref[0, 0, base + k] * g2)
                   for k in range(ROW_GROUP)]
            for k in range(ROW_GROUP):
                acc[pl.ds(ts[k], 1), :] = new[k]
            return carry

        lax.fori_loop(0, (hi_ - lo_) // ROW_GROUP, body, 0)

    @pl.when(e == pl.num_programs(1) - 1)
    def _():
        cp = pltpu.make_async_copy(acc, o_hbm.at[b], sem)
        cp.start()
        cp.wait()


def _moe(xs, aff_t, modtab, n2g, l, w_gate, w_up, w_down, n_lat):
    bsz, t_all, d = xs.shape
    n_ctx = t_all - n_lat
    ne = N_EXPERTS
    c_lat = max(1, (EC_CAPACITY_FACTOR * n_lat) // ne)
    c_ctx = max(1, (EC_CAPACITY_FACTOR * n_ctx) // ne)
    c_tot = c_lat + c_ctx
    assert c_lat % ROW_GROUP == 0 and c_ctx % ROW_GROUP == 0
    g_lat, i_lat = lax.top_k(aff_t[:, :, :n_lat], c_lat)
    g_ctx, i_ctx = lax.top_k(aff_t[:, :, n_lat:], c_ctx)
    idx = jnp.concatenate([i_lat, i_ctx + n_lat], axis=-1).astype(I32).reshape(bsz * ne, 1, c_tot)
    gates = jnp.concatenate([g_lat, g_ctx], axis=-1).reshape(bsz * ne, 1, c_tot)
    smem = functools.partial(pl.BlockSpec, memory_space=pltpu.SMEM)

    xg = pl.pallas_call(
        functools.partial(_gather_kernel, c_lat=c_lat),
        grid=(bsz, ne),
        in_specs=[smem((1, 1, c_tot), lambda b, e: (b * ne + e, 0, 0)),
                  pl.BlockSpec((1, t_all, d), lambda b, e: (b, 0, 0), pipeline_mode=pl.Buffered(1)),
                  pl.BlockSpec((2, 1, 6 * d), lambda b, e: (b, 0, 0)),
                  pl.BlockSpec((1, d), lambda b, e: (0, 0))],
        out_specs=pl.BlockSpec((1, 1, c_tot, d), lambda b, e: (b, e, 0, 0)),
        out_shape=jax.ShapeDtypeStruct((bsz, ne, c_tot, d), BF16),
        scratch_shapes=[pltpu.VMEM((c_tot, d), F32)],
        compiler_params=_cparams(("arbitrary", "arbitrary"), VMEM_LIMIT),
        name="moe_gather",
    )(idx, xs, modtab, n2g)

    nf = EXPERT_FF // FF_TILE
    y = pl.pallas_call(
        _ffn_kernel,
        grid=(bsz, ne, nf),
        in_specs=[pl.BlockSpec((1, 1, c_tot, d), lambda b, e, f: (b, e, 0, 0)),
                  pl.BlockSpec((1, 1, d, FF_TILE), lambda b, e, f: (l, e, 0, f)),
                  pl.BlockSpec((1, 1, d, FF_TILE), lambda b, e, f: (l, e, 0, f)),
                  pl.BlockSpec((1, 1, FF_TILE, d), lambda b, e, f: (l, e, f, 0))],
        out_specs=pl.BlockSpec((1, 1, c_tot, d), lambda b, e, f: (b, e, 0, 0)),
        out_shape=jax.ShapeDtypeStruct((bsz, ne, c_tot, d), F32),
        compiler_params=_cparams(("arbitrary", "arbitrary", "arbitrary"), VMEM_LIMIT),
        name="moe_ffn",
    )(xg, w_gate, w_up, w_down)

    return pl.pallas_call(
        functools.partial(_combine_kernel, c_lat=c_lat),
        grid=(bsz, ne),
        in_specs=[smem((1, 1, c_tot), lambda b, e: (b * ne + e, 0, 0)),
                  smem((1, 1, c_tot), lambda b, e: (b * ne + e, 0, 0)),
                  pl.BlockSpec(memory_space=pl.ANY),
                  pl.BlockSpec((1, 1, c_tot, d), lambda b, e: (b, e, 0, 0)),
                  pl.BlockSpec((2, 1, 6 * d), lambda b, e: (b, 0, 0))],
        out_specs=pl.BlockSpec(memory_space=pl.ANY),
        out_shape=jax.ShapeDtypeStruct(xs.shape, F32),
        scratch_shapes=[pltpu.VMEM((t_all, d), F32), pltpu.SemaphoreType.DMA(())],
        input_output_aliases={2: 0},
        compiler_params=_cparams(("arbitrary", "arbitrary"), VMEM_LIMIT),
        name="moe_combine",
    )(idx, gates, xs, y, modtab)


def _tile_specs(d, n_lat_tiles):
    x_spec = pl.BlockSpec((1, TM, d), lambda b, i: (b, i, 0))
    m_spec = pl.BlockSpec((1, 1, 6 * d), lambda b, i: (2 * b + i // n_lat_tiles, 0, 0))
    return x_spec, m_spec


def _const_spec(shape):
    return pl.BlockSpec(shape, lambda b, i: (0,) * len(shape))


def _even_layer(xs, modtab, n1g, n2g, rwt, w_in_ext, lg, gn_g, conv_w, conv_b, ln_g, ln_b, w_out, cos, sin, n_lat):
    bsz, t_all, d = xs.shape
    nt = t_all // TM
    nlt = n_lat // TM
    x_spec, m_spec = _tile_specs(d, nlt)
    n_cols = 7 * 512
    proj = pl.pallas_call(
        _even_in_kernel,
        grid=(bsz, nt),
        in_specs=[x_spec, m_spec, _const_spec((1, d)), _const_spec(w_in_ext.shape),
                  pl.BlockSpec((TM, 512), lambda b, i: (i, 0)), pl.BlockSpec((TM, 512), lambda b, i: (i, 0))],
        out_specs=pl.BlockSpec((1, TM, n_cols), lambda b, i: (b, i, 0)),
        out_shape=jax.ShapeDtypeStruct((bsz, t_all, n_cols), BF16),
        compiler_params=_cparams(("arbitrary", "arbitrary"), VMEM_LIMIT),
        name="even_in",
    )(xs, modtab, n1g, w_in_ext, cos, sin)

    fwd = lambda s: jnp.where(s == 0, nlt, s - 1)
    bwd = lambda s: jnp.where(s == 0, nlt, nlt - s)

    def col(order, j):
        return pl.BlockSpec((1, TM, 512), lambda b, s: (b, order(s), j))

    o_f, o_b = pl.pallas_call(
        _ret_kernel,
        grid=(bsz, nt),
        in_specs=[pl.BlockSpec(lg.shape, lambda b, s: (0, 0, 0, 0)),
                  col(fwd, 0), col(fwd, 1), col(fwd, 2), col(bwd, 0), col(bwd, 1), col(bwd, 2)],
        out_specs=[col(fwd, 0), col(bwd, 0)],
        out_shape=[jax.ShapeDtypeStruct((bsz, t_all, 512), BF16)] * 2,
        scratch_shapes=[pltpu.VMEM((2, RET_HEADS, RET_DK, 128), F32),
                        pltpu.VMEM((2, RET_HEADS, RET_CHUNK, RET_CHUNK), F32),
                        pltpu.VMEM((2, RET_HEADS, RET_CHUNK, 128), F32),
                        pltpu.VMEM((2, RET_HEADS, RET_CHUNK, 128), F32),
                        pltpu.VMEM((2, RET_HEADS, 1, 128), F32)],
        compiler_params=_cparams(("arbitrary", "arbitrary"), VMEM_LIMIT),
        name="retention",
    )(lg, proj, proj, proj, proj, proj, proj)

    hpt = TM // CONV_HALO
    n_halo = t_all // CONV_HALO

    def tcol(j):
        return pl.BlockSpec((1, TM, 512), lambda b, i: (b, i, j))

    def halo(j, nxt):
        if nxt:
            return pl.BlockSpec((1, CONV_HALO, 512), lambda b, i: (b, jnp.minimum((i + 1) * hpt, n_halo - 1), j))
        return pl.BlockSpec((1, CONV_HALO, 512), lambda b, i: (b, jnp.maximum(i * hpt - 1, 0), j))

    xs_new, aff_t = pl.pallas_call(
        functools.partial(_even_out_kernel, n_lat_tiles=nlt),
        grid=(bsz, nt),
        in_specs=[x_spec, m_spec, pl.BlockSpec((1, TM, 512), lambda b, i: (b, i, 0)),
                  pl.BlockSpec((1, TM, 512), lambda b, i: (b, i, 0)),
                  tcol(3), tcol(4), tcol(5), tcol(6), halo(5, False), halo(6, False), halo(5, True), halo(6, True),
                  _const_spec((1, 512)), _const_spec(conv_w.shape), _const_spec((1, 512)), _const_spec((1, 512)),
                  _const_spec((1, 512)), _const_spec(w_out.shape), _const_spec((1, d)), _const_spec(rwt.shape)],
        out_specs=[x_spec, pl.BlockSpec((1, N_EXPERTS, TM), lambda b, i: (b, 0, i))],
        out_shape=[jax.ShapeDtypeStruct(xs.shape, F32), jax.ShapeDtypeStruct((bsz, N_EXPERTS, t_all), F32)],
        scratch_shapes=[pltpu.VMEM((TM + 2 * CONV_HALO, CONV_CH), F32)],
        input_output_aliases={0: 0},
        compiler_params=_cparams(("arbitrary", "arbitrary"), VMEM_LIMIT),
        name="even_out",
    )(xs, modtab, o_f, o_b, proj, proj, proj, proj, proj, proj, proj, proj,
      gn_g, conv_w, conv_b, ln_g, ln_b, w_out, n2g, rwt)
    return xs_new, aff_t


def _odd_layer(xs, modtab, n1g, n2g, rwt, w_in_ext, qag, kvag, w_uq_ext, w_ukv_p, gq, gk, w_out, cos, sin, n_lat):
    bsz, t_all, d = xs.shape
    nt = t_all // TM
    nlt = n_lat // TM
    nh = MLA_HEADS
    x_spec, m_spec = _tile_specs(d, nlt)
    q, kt, v = pl.pallas_call(
        _odd_in_kernel,
        grid=(bsz, nt),
        in_specs=[x_spec, m_spec, _const_spec((1, d)), _const_spec(w_in_ext.shape), _const_spec(qag.shape),
                  _const_spec(kvag.shape), _const_spec(w_uq_ext.shape), _const_spec(w_ukv_p.shape),
                  _const_spec(gq.shape), _const_spec(gk.shape),
                  pl.BlockSpec((TM, 128), lambda b, i: (i, 0)), pl.BlockSpec((TM, 128), lambda b, i: (i, 0))],
        out_specs=[pl.BlockSpec((1, nh, TM, MLA_QK_PAD), lambda b, i: (b, 0, i, 0)),
                   pl.BlockSpec((1, nh, 1, MLA_QK_PAD, TM), lambda b, i: (b, 0, i, 0, 0)),
                   pl.BlockSpec((1, nh, TM, 2 * MLA_V), lambda b, i: (b, 0, i, 0))],
        out_shape=[jax.ShapeDtypeStruct((bsz, nh, t_all, MLA_QK_PAD), BF16),
                   jax.ShapeDtypeStruct((bsz, nh, nt, MLA_QK_PAD, TM), BF16),
                   jax.ShapeDtypeStruct((bsz, nh, t_all, 2 * MLA_V), BF16)],
        compiler_params=_cparams(("arbitrary", "arbitrary"), VMEM_LIMIT),
        name="odd_in",
    )(xs, modtab, n1g, w_in_ext, qag, kvag, w_uq_ext, w_ukv_p, gq, gk, cos, sin)

    tps = next(c for c in ATT_KT_PER_STEP if nt % c == 0)
    nsub = ATT_TQ // ATT_SUB
    o = pl.pallas_call(
        functools.partial(_attn_kernel, tps=tps),
        grid=(bsz, nh, n_lat // ATT_TQ),
        in_specs=[pl.BlockSpec((1, 1, ATT_TQ, MLA_QK_PAD), lambda b, h, i: (b, h, i, 0)),
                  pl.BlockSpec((1, 1, nt, MLA_QK_PAD, TM), lambda b, h, i: (b, h, 0, 0, 0)),
                  pl.BlockSpec((1, 1, t_all, 2 * MLA_V), lambda b, h, i: (b, h, 0, 0))],
        out_specs=pl.BlockSpec((1, ATT_TQ, MLA_V), lambda b, h, i: (b, i, h)),
        out_shape=jax.ShapeDtypeStruct((bsz, n_lat, nh * MLA_V), BF16),
        scratch_shapes=[pltpu.VMEM((2, nsub, ATT_SUB, tps * TM), F32), pltpu.VMEM((nsub, ATT_SUB, 1), F32),
                        pltpu.VMEM((nsub, ATT_SUB, 2 * MLA_V), F32)],
        compiler_params=_cparams(("arbitrary", "arbitrary", "arbitrary"), VMEM_LIMIT),
        name="mla_attention",
    )(q, kt, v)
    o_ctx = pl.pallas_call(
        _attn_ctx_kernel,
        grid=(bsz, nh),
        in_specs=[pl.BlockSpec((1, 1, TM, MLA_QK_PAD), lambda b, h: (b, h, nlt, 0)),
                  pl.BlockSpec((1, 1, 1, MLA_QK_PAD, TM), lambda b, h: (b, h, nlt, 0, 0)),
                  pl.BlockSpec((1, 1, TM, 2 * MLA_V), lambda b, h: (b, h, nlt, 0))],
        out_specs=pl.BlockSpec((1, TM, MLA_V), lambda b, h: (b, 0, h)),
        out_shape=jax.ShapeDtypeStruct((bsz, TM, nh * MLA_V), BF16),
        compiler_params=_cparams(("arbitrary", "arbitrary"), VMEM_LIMIT),
        name="mla_attention_ctx",
    )(q, kt, v)

    xs_new, aff_t = pl.pallas_call(
        functools.partial(_odd_out_kernel, n_lat_tiles=nlt),
        grid=(bsz, nt),
        in_specs=[x_spec, m_spec, pl.BlockSpec((1, TM, nh * MLA_V), lambda b, i: (b, jnp.minimum(i, nlt - 1), 0)),
                  pl.BlockSpec((1, TM, nh * MLA_V), lambda b, i: (b, 0, 0)),
                  _const_spec(w_out.shape), _const_spec((1, d)), _const_spec(rwt.shape)],
        out_specs=[x_spec, pl.BlockSpec((1, N_EXPERTS, TM), lambda b, i: (b, 0, i))],
        out_shape=[jax.ShapeDtypeStruct(xs.shape, F32), jax.ShapeDtypeStruct((bsz, N_EXPERTS, t_all), F32)],
        input_output_aliases={0: 0},
        compiler_params=_cparams(("arbitrary", "arbitrary"), VMEM_LIMIT),
        name="odd_out",
    )(xs, modtab, o, o_ctx, w_out, n2g, rwt)
    return xs_new, aff_t


def _rope_tables(n_lat, n_ctx, nf):
    t = jnp.arange(n_lat)
    row = (t // GRID_W).astype(F32)
    col = (t % GRID_W).astype(F32)
    inv = ROPE_BASE ** (-(jnp.arange(nf, dtype=F32) / nf))
    ar = row[:, None] * inv[None, :]
    ac = col[:, None] * inv[None, :]
    cos = jnp.concatenate([jnp.cos(ar), jnp.cos(ar), jnp.cos(ac), jnp.cos(ac)], axis=-1)
    sin = jnp.concatenate([-jnp.sin(ar), jnp.sin(ar), -jnp.sin(ac), jnp.sin(ac)], axis=-1)
    cos = jnp.concatenate([cos, jnp.ones((n_ctx, 4 * nf), F32)], axis=0)
    sin = jnp.concatenate([sin, jnp.zeros((n_ctx, 4 * nf), F32)], axis=0)
    return cos, sin


def _swap_perm(nf):
    a = jnp.arange(nf)
    return jnp.concatenate([a + nf, a, a + 3 * nf, a + 2 * nf])


def kernel(x, c, ctx, c_ctx, mod_w, mod_b, norm1_g, norm2_g, ev_w_in, ret_decay_f, ret_decay_b, ret_gn_g, conv_w, conv_b, conv_ln_g, conv_ln_b, ev_w_out, od_w_in, mla_q_a_g, mla_kv_a_g, mla_w_uq, mla_w_ukv, mla_qk_g_q, mla_qk_g_k, od_w_out, router_w, moe_w_gate, moe_w_up, moe_w_down):
    bsz, n_lat, d = x.shape
    n_ctx = ctx.shape[1]
    depth = mod_w.shape[0]
    assert d == D_MODEL and n_ctx == TM and n_lat % ATT_TQ == 0 and bsz + 1 <= 8

    xs = jnp.concatenate([x, ctx], axis=1)
    cc = jnp.zeros((8, d), F32).at[:bsz].set(c).at[bsz].set(c_ctx)
    mod = _mod_all(cc, mod_w, mod_b)

    cos_e, sin_e = _rope_tables(n_lat, n_ctx, RET_DK // 4)
    cos_e, sin_e = jnp.tile(cos_e, (1, RET_HEADS)), jnp.tile(sin_e, (1, RET_HEADS))
    cos_o, sin_o = _rope_tables(n_lat, n_ctx, MLA_ROPE // 4)
    cos_o, sin_o = jnp.tile(cos_o, (1, 2)), jnp.tile(sin_o, (1, 2))
    perm_e = (jnp.arange(RET_HEADS)[:, None] * RET_DK + _swap_perm(RET_DK // 4)[None, :]).reshape(-1)
    perm_o = _swap_perm(MLA_ROPE // 4)

    for l in range(depth):
        i = l // 2
        modtab = jnp.stack([mod[l, :bsz], jnp.broadcast_to(mod[l, bsz], (bsz, 6 * d))], axis=1).reshape(bsz * 2, 1, 6 * d)
        n1g = norm1_g[l][None]
        n2g = norm2_g[l][None]
        rw = router_w[l]
        rw_hi = rw.astype(BF16)
        rw_lo = (rw - rw_hi.astype(F32)).astype(BF16)
        pad = jnp.zeros((d, 128 - N_EXPERTS), BF16)
        rwt = jnp.concatenate([rw_hi, pad, rw_lo, pad], axis=1)
        if l % 2 == 0:
            w = ev_w_in[i]
            w_ext = jnp.concatenate([w, w[:, 0:512][:, perm_e], w[:, 512:1024][:, perm_e]], axis=1).astype(BF16)
            lg = jnp.broadcast_to(jnp.stack([ret_decay_f[i], ret_decay_b[i]])[:, :, None, None], (2, RET_HEADS, 1, 128))
            cw = jnp.concatenate([conv_w[i], jnp.zeros((1, CONV_CH), F32)], axis=0)
            xs, aff_t = _even_layer(xs, modtab, n1g, n2g, rwt, w_ext, lg, ret_gn_g[i][None], cw, conv_b[i][None],
                                    conv_ln_g[i][None], conv_ln_b[i][None], ev_w_out[i].astype(BF16), cos_e, sin_e, n_lat)
        else:
            w = od_w_in[i]
            kpe = w[:, MLA_Q_LORA + MLA_KV_LORA:]
            w_ext = jnp.concatenate([w[:, :MLA_Q_LORA + MLA_KV_LORA], kpe, kpe, kpe[:, perm_o], kpe[:, perm_o]], axis=1).astype(BF16)
            wq = mla_w_uq[i].reshape(MLA_Q_LORA, MLA_HEADS, MLA_QK)
            wq_r = wq[:, :, MLA_NOPE:]
            w_uq_ext = jnp.concatenate([wq[:, :, :MLA_NOPE].reshape(MLA_Q_LORA, -1), wq_r.reshape(MLA_Q_LORA, -1),
                                        wq_r[:, :, perm_o].reshape(MLA_Q_LORA, -1)], axis=1).astype(BF16)
            wkv = mla_w_ukv[i].reshape(MLA_KV_LORA, MLA_HEADS, MLA_NOPE + MLA_V)
            w_ukv_p = jnp.concatenate([wkv[:, :, :MLA_NOPE].reshape(MLA_KV_LORA, -1),
                                       wkv[:, :, MLA_NOPE:].reshape(MLA_KV_LORA, -1)], axis=1).astype(BF16)

            def gains(g):
                gr = g[MLA_NOPE:]
                return jnp.stack([g[:MLA_NOPE], jnp.tile(gr, 2), jnp.tile(gr[perm_o], 2)])

            xs, aff_t = _odd_layer(xs, modtab, n1g, n2g, rwt, w_ext, mla_q_a_g[i][None], mla_kv_a_g[i][None], w_uq_ext,
                                   w_ukv_p, gains(mla_qk_g_q[i]), gains(mla_qk_g_k[i]), od_w_out[i].astype(BF16),
                                   cos_o, sin_o, n_lat)
        xs = _moe(xs, aff_t, modtab, n2g, l, moe_w_gate, moe_w_up, moe_w_down, n_lat)
    return xs[:, :n_lat]
```

```python
import functools

import jax
import jax.numpy as jnp
from jax import lax
from jax.experimental import pallas as pl
from jax.experimental.pallas import tpu as pltpu

F32 = jnp.float32
BF16 = jnp.bfloat16
I32 = jnp.int32

D_MODEL = 1024
GRID_W = 64
RET_HEADS = 4
RET_DK = 128
RET_CHUNK = 128
CONV_CH = 512
CONV_WIDTH = 31
CONV_HALO = 16
MLA_HEADS = 8
MLA_Q_LORA = 512
MLA_KV_LORA = 256
MLA_NOPE = 128
MLA_ROPE = 64
MLA_V = 128
MLA_QK = MLA_NOPE + MLA_ROPE
MLA_QK_PAD = 256
N_EXPERTS = 16
EC_CAPACITY_FACTOR = 2
EXPERT_FF = 1024
ROPE_BASE = 10000.0
LOG2E = 1.4426950408889634
EPS = 1e-6

LANES = 128
SUBLANES = 8
TM = 256
ATT_TQ = 1024
ATT_SUB = 256
ATT_GROUP = 2
ATT_UNROLL = 4
ATT_KT_PER_STEP = (3, 2, 1)
FF_TILE = 512
ROW_GROUP = 8
VMEM_LIMIT = 56 * 2 ** 20

_NT = (((1,), (1,)), ((), ()))


def _cparams(sem, vmem=None):
    return pltpu.CompilerParams(dimension_semantics=sem, vmem_limit_bytes=vmem)


def _dot(a, b):
    return jnp.dot(a, b, preferred_element_type=F32)


def _split3(a):
    a1 = a.astype(BF16)
    r = a - a1.astype(F32)
    a2 = r.astype(BF16)
    a3 = (r - a2.astype(F32)).astype(BF16)
    return a1, a2, a3


def _dot_hi(a, b, dn):
    a1, a2, a3 = _split3(a)
    b1, b2, b3 = _split3(b)
    d = lambda x, y: lax.dot_general(x, y, dn, preferred_element_type=F32)
    return ((d(a3, b1) + d(a2, b2) + d(a1, b3)) + (d(a2, b1) + d(a1, b2))) + d(a1, b1)


def _silu(a):
    return a * jax.nn.sigmoid(a)


def _rms(x, g):
    return x * lax.rsqrt(jnp.mean(x * x, axis=-1, keepdims=True) + EPS) * g


def _modulate(x, g, shift, scale):
    return _rms(x, g) * (1.0 + scale) + shift


def _mod_kernel(c_ref, w_ref, b_ref, o_ref):
    s = _silu(c_ref[...])
    o_ref[0] = _dot_hi(s, w_ref[0], (((1,), (0,)), ((), ()))) + b_ref[0]


def _mod_all(cc, mod_w, mod_b):
    depth, d, n6 = mod_w.shape
    tn = 1536
    return pl.pallas_call(
        _mod_kernel,
        grid=(depth, n6 // tn),
        in_specs=[pl.BlockSpec((8, d), lambda l, j: (0, 0)),
                  pl.BlockSpec((1, d, tn), lambda l, j: (l, 0, j)),
                  pl.BlockSpec((1, 1, tn), lambda l, j: (l, 0, j))],
        out_specs=pl.BlockSpec((1, 8, tn), lambda l, j: (l, 0, j)),
        out_shape=jax.ShapeDtypeStruct((depth, 8, n6), F32),
        compiler_params=_cparams(("arbitrary", "arbitrary"), VMEM_LIMIT),
        name="mod_all",
    )(cc, mod_w, mod_b.reshape(depth, 1, n6))


def _epilogue(x, out, m, n2g, rwt, xo_ref, aff_ref):
    d = D_MODEL
    xn = x + m[:, 2 * d:3 * d] * out
    xo_ref[0] = xn
    h2 = _modulate(xn, n2g, m[:, 3 * d:4 * d], m[:, 4 * d:5 * d])
    h_hi = h2.astype(BF16)
    h_lo = (h2 - h_hi.astype(F32)).astype(BF16)
    z = _dot(h_hi, rwt)
    logits = (z[:, 0:128] + z[:, 128:256]) + _dot(h_lo, rwt[:, 0:128])
    lane = lax.broadcasted_iota(I32, logits.shape, 1)
    logits = jnp.where(lane < N_EXPERTS, logits, -jnp.inf)
    e = jnp.exp(logits - jnp.max(logits, axis=-1, keepdims=True))
    aff = e / jnp.sum(e, axis=-1, keepdims=True)
    aff_t = aff.T
    for k in range(TM // LANES):
        aff_ref[0, k] = aff_t[0:N_EXPERTS, k * LANES:(k + 1) * LANES]


def _even_in_kernel(x_ref, m_ref, g_ref, w_ref, cos_ref, sin_ref, o_ref):
    d = D_MODEL
    m = m_ref[0]
    h = _modulate(x_ref[0], g_ref[...], m[:, 0:d], m[:, d:2 * d]).astype(BF16)

    def grp(j):
        return _dot(h, w_ref[:, j * 512:(j + 1) * 512])

    c = cos_ref[...]
    s = sin_ref[...]
    o_ref[0, :, 0:512] = (grp(0) * c + grp(7) * s).astype(BF16)
    o_ref[0, :, 512:1024] = ((grp(1) * c + grp(8) * s) * (RET_DK ** -0.5)).astype(BF16)
    for j in range(2, 7):
        o_ref[0, :, j * 512:(j + 1) * 512] = grp(j).astype(BF16)


def _ret_kernel(lg_ref, qf_ref, kf_ref, vf_ref, qb_ref, kb_ref, vb_ref, of_ref, ob_ref,
                st_ref, intra_ref, xi_ref, zeta_ref, dc_ref):
    L = RET_CHUNK
    nc = TM // L

    @pl.when(pl.program_id(1) == 0)
    def _():
        st_ref[...] = jnp.zeros(st_ref.shape, F32)
        ii = lax.broadcasted_iota(I32, (L, L), 0).astype(F32)
        jj = lax.broadcasted_iota(I32, (L, L), 1).astype(F32)
        for dr in range(2):
            for h in range(RET_HEADS):
                x = lg_ref[dr, h]
                lg = jnp.minimum(x, 0.0) - jnp.log(1.0 + jnp.exp(-jnp.abs(x)))
                if dr == 0:
                    diff, xpos, zpos = ii - jj, ii + 1.0, (L - 1.0) - ii
                else:
                    diff, xpos, zpos = jj - ii, L - ii, ii
                intra_ref[dr, h] = jnp.where(diff >= 0, jnp.exp(lg * jnp.maximum(diff, 0.0)), 0.0)
                xi_ref[dr, h] = jnp.exp(lg * xpos)
                zeta_ref[dr, h] = jnp.exp(lg * zpos)
                dc_ref[dr, h] = jnp.exp(lg * float(L))

    for dr, (q_ref, k_ref, v_ref, o_ref) in enumerate(((qf_ref, kf_ref, vf_ref, of_ref),
                                                      (qb_ref, kb_ref, vb_ref, ob_ref))):
        order = range(nc) if dr == 0 else range(nc - 1, -1, -1)
        for c in order:
            for h in range(RET_HEADS):
                rs = slice(c * L, (c + 1) * L)
                cs = slice(h * 128, (h + 1) * 128)
                qc = q_ref[0, rs, cs]
                kc = k_ref[0, rs, cs]
                vc = v_ref[0, rs, cs]
                st = st_ref[dr, h]
                sc = lax.dot_general(qc, kc, _NT, preferred_element_type=F32) * intra_ref[dr, h]
                o = _dot(sc.astype(BF16), vc) + _dot(qc, st.astype(BF16)) * xi_ref[dr, h]
                kz = (kc.astype(F32) * zeta_ref[dr, h]).T.astype(BF16)
                st_ref[dr, h] = st * dc_ref[dr, h] + _dot(kz, vc)
                o_ref[0, rs, cs] = o.astype(BF16)


def _even_out_kernel(x_ref, m_ref, of_ref, ob_ref, gf_ref, gb_ref, bv_ref, bg_ref,
                     pv_ref, pg_ref, nv_ref, ng_ref, gn_ref, cw_ref, cb_ref, lng_ref, lnb_ref,
                     wo_ref, n2_ref, rwt_ref, xo_ref, aff_ref, u_scr, ush_scr, *, n_lat_tiles):
    i = pl.program_id(1)
    hl = CONV_HALO

    def glu(v_ref, g_ref):
        return v_ref[0].astype(F32) * jax.nn.sigmoid(g_ref[0].astype(F32))

    prev_ok = jnp.logical_and(i >= 1, i < n_lat_tiles)
    next_ok = i < n_lat_tiles - 1
    u_scr[0:hl, :] = jnp.where(prev_ok, glu(pv_ref, pg_ref), 0.0)
    u_scr[hl:hl + TM, :] = glu(bv_ref, bg_ref)
    u_scr[hl + TM:2 * hl + TM, :] = jnp.where(next_ok, glu(nv_ref, ng_ref), 0.0)
    n_sh = TM + 2 * hl - SUBLANES
    for r in range(1, SUBLANES):
        ush_scr[r - 1] = u_scr[r:r + n_sh, :]
    y = jnp.zeros((TM, CONV_CH), F32) + cb_ref[...]
    off = hl - CONV_WIDTH // 2
    for k in range(CONV_WIDTH):
        r = (k + off) % SUBLANES
        a = k + off - r
        win = u_scr[a:a + TM, :] if r == 0 else ush_scr[r - 1, a:a + TM, :]
        y = y + cw_ref[k:k + 1, :] * win
    mu = jnp.mean(y, axis=-1, keepdims=True)
    yc = y - mu
    var = jnp.mean(yc * yc, axis=-1, keepdims=True)
    conv = _silu(yc * lax.rsqrt(var + EPS) * lng_ref[...] + lnb_ref[...])

    def gnorm(o_ref, g_ref):
        parts = []
        for h in range(RET_HEADS):
            cs = slice(h * 128, (h + 1) * 128)
            o = o_ref[0, :, cs].astype(F32)
            mu_h = jnp.mean(o, axis=-1, keepdims=True)
            oc = o - mu_h
            var_h = jnp.mean(oc * oc, axis=-1, keepdims=True)
            parts.append(oc * lax.rsqrt(var_h + EPS) * gn_ref[:, cs] * _silu(g_ref[0, :, cs].astype(F32)))
        return parts

    pf = gnorm(of_ref, gf_ref)
    pb = gnorm(ob_ref, gb_ref)
    out = _dot(conv.astype(BF16), wo_ref[512:1024, :])
    for h in range(RET_HEADS):
        out = out + _dot((pf[h] + pb[h]).astype(BF16), wo_ref[h * 128:(h + 1) * 128, :])
    _epilogue(x_ref[0], out, m_ref[0], n2_ref[...], rwt_ref[...], xo_ref, aff_ref)


def _odd_in_kernel(x_ref, m_ref, g_ref, win_ref, qag_ref, kvag_ref, wuq_ref, wukv_ref,
                   gq_ref, gk_ref, cos_ref, sin_ref, q_ref, kt_ref, v_ref):
    d = D_MODEL
    nh = MLA_HEADS
    m = m_ref[0]
    h = _modulate(x_ref[0], g_ref[...], m[:, 0:d], m[:, d:2 * d]).astype(BF16)
    c = _dot(h, win_ref[...])
    cq = _rms(c[:, 0:512], qag_ref[...]).astype(BF16)
    ckv = _rms(c[:, 512:768], kvag_ref[...]).astype(BF16)
    kpe = c[:, 768:896]
    kpe_sw = c[:, 896:1024]
    qa = _dot(cq, wuq_ref[...])
    kva = _dot(ckv, wukv_ref[...])
    cos = cos_ref[...]
    sin = sin_ref[...]
    gq = gq_ref[...]
    gk = gk_ref[...]
    lane = lax.broadcasted_iota(I32, (TM, 128), 1)
    lo = lane < MLA_ROPE
    inv_d = 1.0 / MLA_QK

    ss_kpe = 0.5 * jnp.sum(kpe * kpe, axis=-1, keepdims=True)
    kr = kpe * gk[1:2] * cos + kpe_sw * gk[2:3] * sin
    for p in range(nh // 2):
        r = qa[:, 1024 + p * 128:1024 + (p + 1) * 128]
        r_sw = qa[:, 1536 + p * 128:1536 + (p + 1) * 128]
        r2 = r * r
        ss_r = (jnp.sum(jnp.where(lo, r2, 0.0), axis=-1, keepdims=True),
                jnp.sum(jnp.where(lo, 0.0, r2), axis=-1, keepdims=True))
        qr = r * gq[1:2] * cos + r_sw * gq[2:3] * sin
        for s in range(2):
            hd = 2 * p + s
            keep = lo if s == 0 else jnp.logical_not(lo)
            qn = qa[:, hd * 128:(hd + 1) * 128]
            nq = lax.rsqrt((jnp.sum(qn * qn, axis=-1, keepdims=True) + ss_r[s]) * inv_d + EPS) * (MLA_QK ** -0.5 * LOG2E)
            q_ref[0, hd, :, 0:128] = (qn * nq * gq[0:1]).astype(BF16)
            q_ref[0, hd, :, 128:256] = jnp.where(keep, qr * nq, 0.0).astype(BF16)
            kn = kva[:, hd * 128:(hd + 1) * 128]
            nk = lax.rsqrt((jnp.sum(kn * kn, axis=-1, keepdims=True) + ss_kpe) * inv_d + EPS)
            k_full = jnp.concatenate([kn * nk * gk[0:1], jnp.where(keep, kr * nk, 0.0)], axis=-1)
            kt_ref[0, hd, 0] = k_full.T.astype(BF16)
            v_ref[0, hd, :, 0:MLA_V] = kva[:, 1024 + hd * 128:1024 + (hd + 1) * 128].astype(BF16)
            v_ref[0, hd, :, MLA_V:2 * MLA_V] = jnp.ones((TM, MLA_V), BF16)


def _attn_kernel(q_ref, kt_ref, v_ref, o_ref, s_scr, m_scr, acc_scr, *, tps):
    tq = q_ref.shape[2]
    nsub = tq // ATT_SUB
    kw = kt_ref.shape[4]
    n_steps = kt_ref.shape[2] // tps
    n_units = (nsub // ATT_GROUP) * n_steps

    def split(u):
        g = u // n_steps
        return g, u - g * n_steps

    def qk(u, buf):
        g, j = split(u)
        for k in range(ATT_GROUP):
            q = q_ref[0, 0, pl.ds(pl.multiple_of((g * ATT_GROUP + k) * ATT_SUB, ATT_SUB), ATT_SUB), :]
            for c in range(tps):
                s_scr[buf, k, :, c * kw:(c + 1) * kw] = _dot(q, kt_ref[0, 0, j * tps + c])

    def smpv(u, buf):
        g, j = split(u)
        vj = v_ref[0, 0, pl.ds(pl.multiple_of(j * (tps * kw), tps * kw), tps * kw), :]
        for k in range(ATT_GROUP):
            s = g * ATT_GROUP + k
            sv = s_scr[buf, k]
            m = m_scr[s]
            m_new = jnp.maximum(m, jnp.max(sv, axis=-1, keepdims=True))
            alpha = jnp.exp2(m - m_new)
            p = jnp.exp2(sv - m_new).astype(BF16)
            acc_scr[s] = alpha * acc_scr[s] + _dot(p, vj)
            m_scr[s] = m_new

    m_scr[...] = jnp.full(m_scr.shape, -jnp.inf, F32)
    acc_scr[...] = jnp.zeros(acc_scr.shape, F32)
    qk(0, 0)

    def body(i, carry):
        for r in range(ATT_UNROLL):
            u = ATT_UNROLL * i + r
            qk(u + 1, (r + 1) % 2)
            smpv(u, r % 2)
        return carry

    n_loop = (n_units - 1) // ATT_UNROLL
    lax.fori_loop(0, n_loop, body, 0)
    for u in range(ATT_UNROLL * n_loop, n_units):
        if u + 1 < n_units:
            qk(u + 1, (u + 1) % 2)
        smpv(u, u % 2)
    for s in range(nsub):
        acc = acc_scr[s]
        o_ref[0, s * ATT_SUB:(s + 1) * ATT_SUB, :] = (acc[:, 0:MLA_V] / acc[:, MLA_V:2 * MLA_V]).astype(BF16)


def _attn_ctx_kernel(q_ref, kt_ref, v_ref, o_ref):
    s = _dot(q_ref[0, 0], kt_ref[0, 0, 0])
    p = jnp.exp2(s - jnp.max(s, axis=-1, keepdims=True))
    acc = _dot(p.astype(BF16), v_ref[0, 0])
    o_ref[0] = (acc[:, 0:MLA_V] / acc[:, MLA_V:2 * MLA_V]).astype(BF16)


def _odd_out_kernel(x_ref, m_ref, o_ref, oc_ref, wo_ref, n2_ref, rwt_ref, xo_ref, aff_ref, *, n_lat_tiles):
    o = jnp.where(pl.program_id(1) == n_lat_tiles, oc_ref[0], o_ref[0])
    out = _dot(o, wo_ref[...])
    _epilogue(x_ref[0], out, m_ref[0], n2_ref[...], rwt_ref[...], xo_ref, aff_ref)


def _ind(c):
    return jnp.where(c, 1.0, 0.0)


def _select_kernel(a_ref, idx_ref, gate_ref, sel_scr, l_scr, a_scr, *, segs):
    ne = N_EXPERTS
    ri = lax.broadcasted_iota(I32, (LANES, LANES), 0)
    ci = lax.broadcasted_iota(I32, (LANES, LANES), 1)
    upper = _ind(ri <= ci).astype(BF16)
    ones8 = jnp.ones((8, LANES), BF16)
    for k0, nt, cap, c0 in segs:
        a = a_ref[0, k0:k0 + nt]
        bits = lax.bitcast_convert_type(a, I32)

        def sbody(i, prefix, bits=bits, cap=cap):
            cand = prefix | jnp.left_shift(jnp.int32(1), 30 - i)
            cnt = jnp.sum(jnp.sum(_ind(bits >= cand[None]), axis=0), axis=-1, keepdims=True)
            return jnp.where(cnt >= cap, cand, prefix)

        thr = lax.fori_loop(0, 31, sbody, jnp.zeros((ne, 1), I32))[None]
        gt = bits > thr
        eqf = _ind(bits == thr)
        need = cap - jnp.sum(jnp.sum(_ind(gt), axis=0), axis=-1, keepdims=True)
        leq = _dot(eqf.reshape(nt * ne, LANES).astype(BF16), upper).reshape(nt, ne, LANES)
        carry = jnp.zeros((ne, 1), F32)
        sel = []
        for k in range(nt):
            rank = carry + leq[k] - eqf[k]
            sel.append(jnp.where(gt[k], 1.0, jnp.where(rank < need, eqf[k], 0.0)))
            carry = carry + leq[k][:, LANES - 1:LANES]
        sel = jnp.stack(sel)
        pad = jnp.zeros((LANES - nt, ne, LANES), F32)
        sel_scr[0:nt] = sel
        sel_scr[nt:LANES] = pad
        l_scr[0:nt] = _dot(sel.reshape(nt * ne, LANES).astype(BF16), upper).reshape(nt, ne, LANES)
        l_scr[nt:LANES] = pad
        a_hi = a.astype(BF16).astype(F32)
        a_mid = (a - a_hi).astype(BF16).astype(F32)
        a_scr[0, 0:nt] = a_hi
        a_scr[1, 0:nt] = a_mid
        a_scr[2, 0:nt] = a - a_hi - a_mid
        for part in range(3):
            a_scr[part, nt:LANES] = pad

        rows = -(-cap // LANES) * LANES
        slot = lax.broadcasted_iota(I32, (rows, LANES), 0).astype(F32)
        lane = lax.broadcasted_iota(I32, (rows, LANES), 1)
        lane_f = lane.astype(F32)
        idx_all = jnp.zeros((rows, LANES), F32)
        gate_all = jnp.zeros((rows, LANES), F32)
        for e in range(ne):
            sel_e = sel_scr[:, e, :].astype(BF16)
            tot = lax.dot_general(ones8, sel_e, _NT, preferred_element_type=F32)
            pb = _dot(tot.astype(BF16), upper)[0:1]
            le = pb <= slot
            kc = jnp.sum(_ind(le), axis=-1, keepdims=True)
            base = jnp.sum(jnp.where(le, tot[0:1], 0.0), axis=-1, keepdims=True)
            onehot = _ind(lane_f == kc).astype(BF16)
            rhs = jnp.concatenate([l_scr[:, e, :], a_scr[0, :, e, :], a_scr[1, :, e, :], a_scr[2, :, e, :]],
                                  axis=-1).astype(BF16)
            g = _dot(onehot, rhs)
            off = jnp.sum(_ind(g[:, 0:LANES] <= slot - base), axis=-1, keepdims=True)
            aff = (g[:, LANES:2 * LANES] + g[:, 2 * LANES:3 * LANES]) + g[:, 3 * LANES:4 * LANES]
            gate = jnp.sum(jnp.where(lane_f == off, aff, 0.0), axis=-1, keepdims=True)
            idx_all = jnp.where(lane == e, (kc + k0) * LANES + off, idx_all)
            gate_all = jnp.where(lane == e, gate, gate_all)
        idx_ref[0, :, c0:c0 + cap] = idx_all.T[0:ne, 0:cap].astype(I32)
        gate_ref[0, :, c0:c0 + cap] = gate_all.T[0:ne, 0:cap]


def _gather_kernel(idx_ref, x_ref, m_ref, n2_ref, o_ref, rows, *, c_lat):
    d = D_MODEL
    c_tot = rows.shape[0]

    def body(r, carry):
        t = idx_ref[0, 0, r]
        rows[pl.ds(r, 1), :] = x_ref[0, pl.ds(t, 1), :]
        return carry

    lax.fori_loop(0, c_tot, body, 0, unroll=8)
    n2 = n2_ref[...]
    for lo_, hi_, mrow in ((0, c_lat, 0), (c_lat, c_tot, 1)):
        m = m_ref[mrow]
        o_ref[0, 0, lo_:hi_, :] = _modulate(rows[lo_:hi_, :], n2, m[:, 3 * d:4 * d], m[:, 4 * d:5 * d]).astype(BF16)


def _ffn_kernel(x_ref, wg_ref, wu_ref, wd_ref, o_ref):
    f = pl.program_id(2)
    x = x_ref[0, 0]
    a = _dot(x, wg_ref[0, 0].astype(BF16))
    u = _dot(x, wu_ref[0, 0].astype(BF16))
    y = _dot((_silu(a) * u).astype(BF16), wd_ref[0, 0].astype(BF16))

    @pl.when(f == 0)
    def _():
        o_ref[0, 0] = y

    @pl.when(f > 0)
    def _():
        o_ref[0, 0] += y


def _combine_kernel(idx_ref, gate_ref, x_hbm, y_ref, m_ref, o_hbm, acc, sem, *, c_lat):
    d = D_MODEL
    b = pl.program_id(0)
    e = pl.program_id(1)
    c_tot = y_ref.shape[2]

    @pl.when(e == 0)
    def _():
        cp = pltpu.make_async_copy(x_hbm.at[b], acc, sem)
        cp.start()
        cp.wait()

    for lo_, hi_, mrow in ((0, c_lat, 0), (c_lat, c_tot, 1)):
        g2 = m_ref[mrow][:, 5 * d:6 * d]

        def body(i, carry, g2=g2, lo_=lo_):
            base = pl.multiple_of(lo_ + i * ROW_GROUP, ROW_GROUP)
            ts = [idx_ref[0, 0, base + k] for k in range(ROW_GROUP)]
            new = [acc[pl.ds(ts[k], 1), :] + y_ref[0, 0, pl.ds(base + k, 1), :] * (gate_ref[0, 0, base + k] * g2)
                   for k in range(ROW_GROUP)]
            for k in range(ROW_GROUP):
                acc[pl.ds(ts[k], 1), :] = new[k]
            return carry

        lax.fori_loop(0, (hi_ - lo_) // ROW_GROUP, body, 0)

    @pl.when(e == pl.num_programs(1) - 1)
    def _():
        cp = pltpu.make_async_copy(acc, o_hbm.at[b], sem)
        cp.start()
        cp.wait()


def _moe(xs, aff_t, modtab, n2g, l, w_gate, w_up, w_down, n_lat):
    bsz, t_all, d = xs.shape
    n_ctx = t_all - n_lat
    ne = N_EXPERTS
    c_lat = max(1, (EC_CAPACITY_FACTOR * n_lat) // ne)
    c_ctx = max(1, (EC_CAPACITY_FACTOR * n_ctx) // ne)
    c_tot = c_lat + c_ctx
    assert c_lat % ROW_GROUP == 0 and c_ctx % ROW_GROUP == 0
    assert n_lat % LANES == 0 and n_ctx % LANES == 0 and t_all // LANES <= LANES
    segs = ((0, n_lat // LANES, c_lat, 0), (n_lat // LANES, n_ctx // LANES, c_ctx, c_lat))
    idx, gates = pl.pallas_call(
        functools.partial(_select_kernel, segs=segs),
        grid=(bsz,),
        in_specs=[pl.BlockSpec((1, t_all // LANES, ne, LANES), lambda b: (b, 0, 0, 0))],
        out_specs=[pl.BlockSpec((1, ne, c_tot), lambda b: (b, 0, 0)), pl.BlockSpec((1, ne, c_tot), lambda b: (b, 0, 0))],
        out_shape=[jax.ShapeDtypeStruct((bsz, ne, c_tot), I32), jax.ShapeDtypeStruct((bsz, ne, c_tot), F32)],
        scratch_shapes=[pltpu.VMEM((LANES, ne, LANES), F32), pltpu.VMEM((LANES, ne, LANES), F32),
                        pltpu.VMEM((3, LANES, ne, LANES), F32)],
        compiler_params=_cparams(("arbitrary",), VMEM_LIMIT),
        name="moe_select",
    )(aff_t)
    idx = idx.reshape(bsz * ne, 1, c_tot)
    gates = gates.reshape(bsz * ne, 1, c_tot)
    smem = functools.partial(pl.BlockSpec, memory_space=pltpu.SMEM)

    xg = pl.pallas_call(
        functools.partial(_gather_kernel, c_lat=c_lat),
        grid=(bsz, ne),
        in_specs=[smem((1, 1, c_tot), lambda b, e: (b * ne + e, 0, 0)),
                  pl.BlockSpec((1, t_all, d), lambda b, e: (b, 0, 0), pipeline_mode=pl.Buffered(1)),
                  pl.BlockSpec((2, 1, 6 * d), lambda b, e: (b, 0, 0)),
                  pl.BlockSpec((1, d), lambda b, e: (0, 0))],
        out_specs=pl.BlockSpec((1, 1, c_tot, d), lambda b, e: (b, e, 0, 0)),
        out_shape=jax.ShapeDtypeStruct((bsz, ne, c_tot, d), BF16),
        scratch_shapes=[pltpu.VMEM((c_tot, d), F32)],
        compiler_params=_cparams(("arbitrary", "arbitrary"), VMEM_LIMIT),
        name="moe_gather",
    )(idx, xs, modtab, n2g)

    nf = EXPERT_FF // FF_TILE
    y = pl.pallas_call(
        _ffn_kernel,
        grid=(bsz, ne, nf),
        in_specs=[pl.BlockSpec((1, 1, c_tot, d), lambda b, e, f: (b, e, 0, 0)),
                  pl.BlockSpec((1, 1, d, FF_TILE), lambda b, e, f: (l, e, 0, f)),
                  pl.BlockSpec((1, 1, d, FF_TILE), lambda b, e, f: (l, e, 0, f)),
                  pl.BlockSpec((1, 1, FF_TILE, d), lambda b, e, f: (l, e, f, 0))],
        out_specs=pl.BlockSpec((1, 1, c_tot, d), lambda b, e, f: (b, e, 0, 0)),
        out_shape=jax.ShapeDtypeStruct((bsz, ne, c_tot, d), F32),
        compiler_params=_cparams(("arbitrary", "arbitrary", "arbitrary"), VMEM_LIMIT),
        name="moe_ffn",
    )(xg, w_gate, w_up, w_down)

    return pl.pallas_call(
        functools.partial(_combine_kernel, c_lat=c_lat),
        grid=(bsz, ne),
        in_specs=[smem((1, 1, c_tot), lambda b, e: (b * ne + e, 0, 0)),
                  smem((1, 1, c_tot), lambda b, e: (b * ne + e, 0, 0)),
                  pl.BlockSpec(memory_space=pl.ANY),
                  pl.BlockSpec((1, 1, c_tot, d), lambda b, e: (b, e, 0, 0)),
                  pl.BlockSpec((2, 1, 6 * d), lambda b, e: (b, 0, 0))],
        out_specs=pl.BlockSpec(memory_space=pl.ANY),
        out_shape=jax.ShapeDtypeStruct(xs.shape, F32),
        scratch_shapes=[pltpu.VMEM((t_all, d), F32), pltpu.SemaphoreType.DMA(())],
        input_output_aliases={2: 0},
        compiler_params=_cparams(("arbitrary", "arbitrary"), VMEM_LIMIT),
        name="moe_combine",
    )(idx, gates, xs, y, modtab)


def _tile_specs(d, n_lat_tiles):
    x_spec = pl.BlockSpec((1, TM, d), lambda b, i: (b, i, 0))
    m_spec = pl.BlockSpec((1, 1, 6 * d), lambda b, i: (2 * b + i // n_lat_tiles, 0, 0))
    return x_spec, m_spec


def _const_spec(shape):
    return pl.BlockSpec(shape, lambda b, i: (0,) * len(shape))


def _even_layer(xs, modtab, n1g, n2g, rwt, w_in_ext, lg, gn_g, conv_w, conv_b, ln_g, ln_b, w_out, cos, sin, n_lat):
    bsz, t_all, d = xs.shape
    nt = t_all // TM
    nlt = n_lat // TM
    x_spec, m_spec = _tile_specs(d, nlt)
    n_cols = 7 * 512
    proj = pl.pallas_call(
        _even_in_kernel,
        grid=(bsz, nt),
        in_specs=[x_spec, m_spec, _const_spec((1, d)), _const_spec(w_in_ext.shape),
                  pl.BlockSpec((TM, 512), lambda b, i: (i, 0)), pl.BlockSpec((TM, 512), lambda b, i: (i, 0))],
        out_specs=pl.BlockSpec((1, TM, n_cols), lambda b, i: (b, i, 0)),
        out_shape=jax.ShapeDtypeStruct((bsz, t_all, n_cols), BF16),
        compiler_params=_cparams(("arbitrary", "arbitrary"), VMEM_LIMIT),
        name="even_in",
    )(xs, modtab, n1g, w_in_ext, cos, sin)

    fwd = lambda s: jnp.where(s == 0, nlt, s - 1)
    bwd = lambda s: jnp.where(s == 0, nlt, nlt - s)

    def col(order, j):
        return pl.BlockSpec((1, TM, 512), lambda b, s: (b, order(s), j))

    o_f, o_b = pl.pallas_call(
        _ret_kernel,
        grid=(bsz, nt),
        in_specs=[pl.BlockSpec(lg.shape, lambda b, s: (0, 0, 0, 0)),
                  col(fwd, 0), col(fwd, 1), col(fwd, 2), col(bwd, 0), col(bwd, 1), col(bwd, 2)],
        out_specs=[col(fwd, 0), col(bwd, 0)],
        out_shape=[jax.ShapeDtypeStruct((bsz, t_all, 512), BF16)] * 2,
        scratch_shapes=[pltpu.VMEM((2, RET_HEADS, RET_DK, 128), F32),
                        pltpu.VMEM((2, RET_HEADS, RET_CHUNK, RET_CHUNK), F32),
                        pltpu.VMEM((2, RET_HEADS, RET_CHUNK, 128), F32),
                        pltpu.VMEM((2, RET_HEADS, RET_CHUNK, 128), F32),
                        pltpu.VMEM((2, RET_HEADS, 1, 128), F32)],
        compiler_params=_cparams(("arbitrary", "arbitrary"), VMEM_LIMIT),
        name="retention",
    )(lg, proj, proj, proj, proj, proj, proj)

    hpt = TM // CONV_HALO
    n_halo = t_all // CONV_HALO

    def tcol(j):
        return pl.BlockSpec((1, TM, 512), lambda b, i: (b, i, j))

    def halo(j, nxt):
        if nxt:
            return pl.BlockSpec((1, CONV_HALO, 512), lambda b, i: (b, jnp.minimum((i + 1) * hpt, n_halo - 1), j))
        return pl.BlockSpec((1, CONV_HALO, 512), lambda b, i: (b, jnp.maximum(i * hpt - 1, 0), j))

    xs_new, aff_t = pl.pallas_call(
        functools.partial(_even_out_kernel, n_lat_tiles=nlt),
        grid=(bsz, nt),
        in_specs=[x_spec, m_spec, pl.BlockSpec((1, TM, 512), lambda b, i: (b, i, 0)),
                  pl.BlockSpec((1, TM, 512), lambda b, i: (b, i, 0)),
                  tcol(3), tcol(4), tcol(5), tcol(6), halo(5, False), halo(6, False), halo(5, True), halo(6, True),
                  _const_spec((1, 512)), _const_spec(conv_w.shape), _const_spec((1, 512)), _const_spec((1, 512)),
                  _const_spec((1, 512)), _const_spec(w_out.shape), _const_spec((1, d)), _const_spec(rwt.shape)],
        out_specs=[x_spec, pl.BlockSpec((1, TM // LANES, N_EXPERTS, LANES), lambda b, i: (b, i, 0, 0))],
        out_shape=[jax.ShapeDtypeStruct(xs.shape, F32), jax.ShapeDtypeStruct((bsz, t_all // LANES, N_EXPERTS, LANES), F32)],
        scratch_shapes=[pltpu.VMEM((TM + 2 * CONV_HALO, CONV_CH), F32),
                        pltpu.VMEM((SUBLANES - 1, TM + 2 * CONV_HALO - SUBLANES, CONV_CH), F32)],
        input_output_aliases={0: 0},
        compiler_params=_cparams(("arbitrary", "arbitrary"), VMEM_LIMIT),
        name="even_out",
    )(xs, modtab, o_f, o_b, proj, proj, proj, proj, proj, proj, proj, proj,
      gn_g, conv_w, conv_b, ln_g, ln_b, w_out, n2g, rwt)
    return xs_new, aff_t


def _odd_layer(xs, modtab, n1g, n2g, rwt, w_in_ext, qag, kvag, w_uq_ext, w_ukv_p, gq, gk, w_out, cos, sin, n_lat):
    bsz, t_all, d = xs.shape
    nt = t_all // TM
    nlt = n_lat // TM
    nh = MLA_HEADS
    x_spec, m_spec = _tile_specs(d, nlt)
    q, kt, v = pl.pallas_call(
        _odd_in_kernel,
        grid=(bsz, nt),
        in_specs=[x_spec, m_spec, _const_spec((1, d)), _const_spec(w_in_ext.shape), _const_spec(qag.shape),
                  _const_spec(kvag.shape), _const_spec(w_uq_ext.shape), _const_spec(w_ukv_p.shape),
                  _const_spec(gq.shape), _const_spec(gk.shape),
                  pl.BlockSpec((TM, 128), lambda b, i: (i, 0)), pl.BlockSpec((TM, 128), lambda b, i: (i, 0))],
        out_specs=[pl.BlockSpec((1, nh, TM, MLA_QK_PAD), lambda b, i: (b, 0, i, 0)),
                   pl.BlockSpec((1, nh, 1, MLA_QK_PAD, TM), lambda b, i: (b, 0, i, 0, 0)),
                   pl.BlockSpec((1, nh, TM, 2 * MLA_V), lambda b, i: (b, 0, i, 0))],
        out_shape=[jax.ShapeDtypeStruct((bsz, nh, t_all, MLA_QK_PAD), BF16),
                   jax.ShapeDtypeStruct((bsz, nh, nt, MLA_QK_PAD, TM), BF16),
                   jax.ShapeDtypeStruct((bsz, nh, t_all, 2 * MLA_V), BF16)],
        compiler_params=_cparams(("arbitrary", "arbitrary"), VMEM_LIMIT),
        name="odd_in",
    )(xs, modtab, n1g, w_in_ext, qag, kvag, w_uq_ext, w_ukv_p, gq, gk, cos, sin)

    tps = next(c for c in ATT_KT_PER_STEP if nt % c == 0)
    nsub = ATT_TQ // ATT_SUB
    o = pl.pallas_call(
        functools.partial(_attn_kernel, tps=tps),
        grid=(bsz, nh, n_lat // ATT_TQ),
        in_specs=[pl.BlockSpec((1, 1, ATT_TQ, MLA_QK_PAD), lambda b, h, i: (b, h, i, 0)),
                  pl.BlockSpec((1, 1, nt, MLA_QK_PAD, TM), lambda b, h, i: (b, h, 0, 0, 0)),
                  pl.BlockSpec((1, 1, t_all, 2 * MLA_V), lambda b, h, i: (b, h, 0, 0))],
        out_specs=pl.BlockSpec((1, ATT_TQ, MLA_V), lambda b, h, i: (b, i, h)),
        out_shape=jax.ShapeDtypeStruct((bsz, n_lat, nh * MLA_V), BF16),
        scratch_shapes=[pltpu.VMEM((2, ATT_GROUP, ATT_SUB, tps * TM), F32), pltpu.VMEM((nsub, ATT_SUB, 1), F32),
                        pltpu.VMEM((nsub, ATT_SUB, 2 * MLA_V), F32)],
        compiler_params=_cparams(("arbitrary", "arbitrary", "arbitrary"), VMEM_LIMIT),
        name="mla_attention",
    )(q, kt, v)
    o_ctx = pl.pallas_call(
        _attn_ctx_kernel,
        grid=(bsz, nh),
        in_specs=[pl.BlockSpec((1, 1, TM, MLA_QK_PAD), lambda b, h: (b, h, nlt, 0)),
                  pl.BlockSpec((1, 1, 1, MLA_QK_PAD, TM), lambda b, h: (b, h, nlt, 0, 0)),
                  pl.BlockSpec((1, 1, TM, 2 * MLA_V), lambda b, h: (b, h, nlt, 0))],
        out_specs=pl.BlockSpec((1, TM, MLA_V), lambda b, h: (b, 0, h)),
        out_shape=jax.ShapeDtypeStruct((bsz, TM, nh * MLA_V), BF16),
        compiler_params=_cparams(("arbitrary", "arbitrary"), VMEM_LIMIT),
        name="mla_attention_ctx",
    )(q, kt, v)

    xs_new, aff_t = pl.pallas_call(
        functools.partial(_odd_out_kernel, n_lat_tiles=nlt),
        grid=(bsz, nt),
        in_specs=[x_spec, m_spec, pl.BlockSpec((1, TM, nh * MLA_V), lambda b, i: (b, jnp.minimum(i, nlt - 1), 0)),
                  pl.BlockSpec((1, TM, nh * MLA_V), lambda b, i: (b, 0, 0)),
                  _const_spec(w_out.shape), _const_spec((1, d)), _const_spec(rwt.shape)],
        out_specs=[x_spec, pl.BlockSpec((1, TM // LANES, N_EXPERTS, LANES), lambda b, i: (b, i, 0, 0))],
        out_shape=[jax.ShapeDtypeStruct(xs.shape, F32), jax.ShapeDtypeStruct((bsz, t_all // LANES, N_EXPERTS, LANES), F32)],
        input_output_aliases={0: 0},
        compiler_params=_cparams(("arbitrary", "arbitrary"), VMEM_LIMIT),
        name="odd_out",
    )(xs, modtab, o, o_ctx, w_out, n2g, rwt)
    return xs_new, aff_t


def _rope_tables(n_lat, n_ctx, nf):
    t = jnp.arange(n_lat)
    row = (t // GRID_W).astype(F32)
    col = (t % GRID_W).astype(F32)
    inv = ROPE_BASE ** (-(jnp.arange(nf, dtype=F32) / nf))
    ar = row[:, None] * inv[None, :]
    ac = col[:, None] * inv[None, :]
    cos = jnp.concatenate([jnp.cos(ar), jnp.cos(ar), jnp.cos(ac), jnp.cos(ac)], axis=-1)
    sin = jnp.concatenate([-jnp.sin(ar), jnp.sin(ar), -jnp.sin(ac), jnp.sin(ac)], axis=-1)
    cos = jnp.concatenate([cos, jnp.ones((n_ctx, 4 * nf), F32)], axis=0)
    sin = jnp.concatenate([sin, jnp.zeros((n_ctx, 4 * nf), F32)], axis=0)
    return cos, sin


def _swap_perm(nf):
    a = jnp.arange(nf)
    return jnp.concatenate([a + nf, a, a + 3 * nf, a + 2 * nf])


def kernel(x, c, ctx, c_ctx, mod_w, mod_b, norm1_g, norm2_g, ev_w_in, ret_decay_f, ret_decay_b, ret_gn_g, conv_w, conv_b, conv_ln_g, conv_ln_b, ev_w_out, od_w_in, mla_q_a_g, mla_kv_a_g, mla_w_uq, mla_w_ukv, mla_qk_g_q, mla_qk_g_k, od_w_out, router_w, moe_w_gate, moe_w_up, moe_w_down):
    bsz, n_lat, d = x.shape
    n_ctx = ctx.shape[1]
    depth = mod_w.shape[0]
    assert d == D_MODEL and n_ctx == TM and n_lat % ATT_TQ == 0 and bsz + 1 <= 8

    xs = jnp.concatenate([x, ctx], axis=1)
    cc = jnp.zeros((8, d), F32).at[:bsz].set(c).at[bsz].set(c_ctx)
    mod = _mod_all(cc, mod_w, mod_b)

    cos_e, sin_e = _rope_tables(n_lat, n_ctx, RET_DK // 4)
    cos_e, sin_e = jnp.tile(cos_e, (1, RET_HEADS)), jnp.tile(sin_e, (1, RET_HEADS))
    cos_o, sin_o = _rope_tables(n_lat, n_ctx, MLA_ROPE // 4)
    cos_o, sin_o = jnp.tile(cos_o, (1, 2)), jnp.tile(sin_o, (1, 2))
    perm_e = (jnp.arange(RET_HEADS)[:, None] * RET_DK + _swap_perm(RET_DK // 4)[None, :]).reshape(-1)
    perm_o = _swap_perm(MLA_ROPE // 4)

    for l in range(depth):
        i = l // 2
        modtab = jnp.stack([mod[l, :bsz], jnp.broadcast_to(mod[l, bsz], (bsz, 6 * d))], axis=1).reshape(bsz * 2, 1, 6 * d)
        n1g = norm1_g[l][None]
        n2g = norm2_g[l][None]
        rw = router_w[l]
        rw_hi = rw.astype(BF16)
        rw_lo = (rw - rw_hi.astype(F32)).astype(BF16)
        pad = jnp.zeros((d, 128 - N_EXPERTS), BF16)
        rwt = jnp.concatenate([rw_hi, pad, rw_lo, pad], axis=1)
        if l % 2 == 0:
            w = ev_w_in[i]
            w_ext = jnp.concatenate([w, w[:, 0:512][:, perm_e], w[:, 512:1024][:, perm_e]], axis=1).astype(BF16)
            lg = jnp.broadcast_to(jnp.stack([ret_decay_f[i], ret_decay_b[i]])[:, :, None, None], (2, RET_HEADS, 1, 128))
            cw = jnp.concatenate([conv_w[i], jnp.zeros((1, CONV_CH), F32)], axis=0)
            xs, aff_t = _even_layer(xs, modtab, n1g, n2g, rwt, w_ext, lg, ret_gn_g[i][None], cw, conv_b[i][None],
                                    conv_ln_g[i][None], conv_ln_b[i][None], ev_w_out[i].astype(BF16), cos_e, sin_e, n_lat)
        else:
            w = od_w_in[i]
            kpe = w[:, MLA_Q_LORA + MLA_KV_LORA:]
            w_ext = jnp.concatenate([w[:, :MLA_Q_LORA + MLA_KV_LORA], kpe, kpe, kpe[:, perm_o], kpe[:, perm_o]], axis=1).astype(BF16)
            wq = mla_w_uq[i].reshape(MLA_Q_LORA, MLA_HEADS, MLA_QK)
            wq_r = wq[:, :, MLA_NOPE:]
            w_uq_ext = jnp.concatenate([wq[:, :, :MLA_NOPE].reshape(MLA_Q_LORA, -1), wq_r.reshape(MLA_Q_LORA, -1),
                                        wq_r[:, :, perm_o].reshape(MLA_Q_LORA, -1)], axis=1).astype(BF16)
            wkv = mla_w_ukv[i].reshape(MLA_KV_LORA, MLA_HEADS, MLA_NOPE + MLA_V)
            w_ukv_p = jnp.concatenate([wkv[:, :, :MLA_NOPE].reshape(MLA_KV_LORA, -1),
                                       wkv[:, :, MLA_NOPE:].reshape(MLA_KV_LORA, -1)], axis=1).astype(BF16)

            def gains(g):
                gr = g[MLA_NOPE:]
                return jnp.stack([g[:MLA_NOPE], jnp.tile(gr, 2), jnp.tile(gr[perm_o], 2)])

            xs, aff_t = _odd_layer(xs, modtab, n1g, n2g, rwt, w_ext, mla_q_a_g[i][None], mla_kv_a_g[i][None], w_uq_ext,
                                   w_ukv_p, gains(mla_qk_g_q[i]), gains(mla_qk_g_k[i]), od_w_out[i].astype(BF16),
                                   cos_o, sin_o, n_lat)
        xs = _moe(xs, aff_t, modtab, n2g, l, moe_w_gate, moe_w_up, moe_w_down, n_lat)
    return xs[:, :n_lat]
```

```python
import functools

import jax
import jax.numpy as jnp
from jax import lax
from jax.experimental import pallas as pl
from jax.experimental.pallas import tpu as pltpu

F32 = jnp.float32
BF16 = jnp.bfloat16
I32 = jnp.int32

D_MODEL = 1024
GRID_W = 64
RET_HEADS = 4
RET_DK = 128
RET_CHUNK = 256
CONV_CH = 512
CONV_WIDTH = 31
CONV_HALO = 16
MLA_HEADS = 8
MLA_Q_LORA = 512
MLA_KV_LORA = 256
MLA_NOPE = 128
MLA_ROPE = 64
MLA_V = 128
MLA_QK = MLA_NOPE + MLA_ROPE
MLA_QK_PAD = 256
N_EXPERTS = 16
EC_CAPACITY_FACTOR = 2
EXPERT_FF = 1024
ROPE_BASE = 10000.0
LOG2E = 1.4426950408889634
EPS = 1e-6

LANES = 128
SUBLANES = 8
TM = 256
ATT_TQ = 2048
ATT_SUB = 256
ATT_GROUP = 2
ATT_UNROLL = 4
ATT_KT_PER_STEP = (3, 2, 1)
FF_TILE = 1024
ROW_GROUP = 8
VMEM_LIMIT = 56 * 2 ** 20

_NT = (((1,), (1,)), ((), ()))


def _cparams(sem, vmem=None):
    return pltpu.CompilerParams(dimension_semantics=sem, vmem_limit_bytes=vmem)


def _dot(a, b):
    return jnp.dot(a, b, preferred_element_type=F32)


def _split3(a):
    a1 = a.astype(BF16)
    r = a - a1.astype(F32)
    a2 = r.astype(BF16)
    a3 = (r - a2.astype(F32)).astype(BF16)
    return a1, a2, a3


def _dot_hi(a, b, dn):
    a1, a2, a3 = _split3(a)
    b1, b2, b3 = _split3(b)
    d = lambda x, y: lax.dot_general(x, y, dn, preferred_element_type=F32)
    return ((d(a3, b1) + d(a2, b2) + d(a1, b3)) + (d(a2, b1) + d(a1, b2))) + d(a1, b1)


def _silu(a):
    return a * jax.nn.sigmoid(a)


def _rms(x, g):
    return x * lax.rsqrt(jnp.mean(x * x, axis=-1, keepdims=True) + EPS) * g


def _modulate(x, g, shift, scale):
    return _rms(x, g) * (1.0 + scale) + shift


def _mod_kernel(c_ref, w_ref, b_ref, o_ref):
    s = _silu(c_ref[...])
    o_ref[0] = _dot_hi(s, w_ref[0], (((1,), (0,)), ((), ()))) + b_ref[0]


def _mod_all(cc, mod_w, mod_b):
    depth, d, n6 = mod_w.shape
    tn = 1536
    return pl.pallas_call(
        _mod_kernel,
        grid=(depth, n6 // tn),
        in_specs=[pl.BlockSpec((8, d), lambda l, j: (0, 0)),
                  pl.BlockSpec((1, d, tn), lambda l, j: (l, 0, j)),
                  pl.BlockSpec((1, 1, tn), lambda l, j: (l, 0, j))],
        out_specs=pl.BlockSpec((1, 8, tn), lambda l, j: (l, 0, j)),
        out_shape=jax.ShapeDtypeStruct((depth, 8, n6), F32),
        compiler_params=_cparams(("arbitrary", "arbitrary"), VMEM_LIMIT),
        name="mod_all",
    )(cc, mod_w, mod_b.reshape(depth, 1, n6))


def _epilogue(x, out, m, n2g, rwt, xo_ref, aff_ref):
    d = D_MODEL
    xn = x + m[:, 2 * d:3 * d] * out
    xo_ref[0] = xn
    h2 = _modulate(xn, n2g, m[:, 3 * d:4 * d], m[:, 4 * d:5 * d])
    h_hi = h2.astype(BF16)
    h_lo = (h2 - h_hi.astype(F32)).astype(BF16)
    z = _dot(h_hi, rwt)
    logits = (z[:, 0:128] + z[:, 128:256]) + _dot(h_lo, rwt[:, 0:128])
    lane = lax.broadcasted_iota(I32, logits.shape, 1)
    logits = jnp.where(lane < N_EXPERTS, logits, -jnp.inf)
    e = jnp.exp(logits - jnp.max(logits, axis=-1, keepdims=True))
    aff = e / jnp.sum(e, axis=-1, keepdims=True)
    aff_t = aff.T
    for k in range(TM // LANES):
        aff_ref[0, k] = aff_t[0:N_EXPERTS, k * LANES:(k + 1) * LANES]


def _even_in_kernel(x_ref, m_ref, g_ref, w_ref, cos_ref, sin_ref, o_ref):
    d = D_MODEL
    m = m_ref[0]
    h = _modulate(x_ref[0], g_ref[...], m[:, 0:d], m[:, d:2 * d]).astype(BF16)

    def grp(j):
        return _dot(h, w_ref[:, j * 512:(j + 1) * 512])

    c = cos_ref[...]
    s = sin_ref[...]
    o_ref[0, :, 0:512] = (grp(0) * c + grp(7) * s).astype(BF16)
    o_ref[0, :, 512:1024] = ((grp(1) * c + grp(8) * s) * (RET_DK ** -0.5)).astype(BF16)
    for j in range(2, 7):
        o_ref[0, :, j * 512:(j + 1) * 512] = grp(j).astype(BF16)


def _ret_kernel(lg_ref, qf_ref, kf_ref, vf_ref, qb_ref, kb_ref, vb_ref, of_ref, ob_ref,
                st_ref, intra_ref, xi_ref, zeta_ref, dc_ref):
    L = RET_CHUNK
    nc = TM // L

    @pl.when(pl.program_id(1) == 0)
    def _():
        st_ref[...] = jnp.zeros(st_ref.shape, F32)
        ii = lax.broadcasted_iota(I32, (L, L), 0).astype(F32)
        jj = lax.broadcasted_iota(I32, (L, L), 1).astype(F32)
        pos = lax.broadcasted_iota(I32, (L, LANES), 0).astype(F32)
        for dr in range(2):
            for h in range(RET_HEADS):
                x = lg_ref[dr, h]
                lg = jnp.minimum(x, 0.0) - jnp.log(1.0 + jnp.exp(-jnp.abs(x)))
                if dr == 0:
                    diff, xpos, zpos = ii - jj, pos + 1.0, (L - 1.0) - pos
                else:
                    diff, xpos, zpos = jj - ii, L - pos, pos
                intra_ref[dr, h] = jnp.where(diff >= 0, jnp.exp(lg[:, 0:1] * jnp.maximum(diff, 0.0)), 0.0)
                xi_ref[dr, h] = jnp.exp(lg * xpos)
                zeta_ref[dr, h] = jnp.exp(lg * zpos)
                dc_ref[dr, h] = jnp.exp(lg * float(L))

    for dr, (q_ref, k_ref, v_ref, o_ref) in enumerate(((qf_ref, kf_ref, vf_ref, of_ref),
                                                      (qb_ref, kb_ref, vb_ref, ob_ref))):
        order = range(nc) if dr == 0 else range(nc - 1, -1, -1)
        for c in order:
            for h in range(RET_HEADS):
                rs = slice(c * L, (c + 1) * L)
                cs = slice(h * 128, (h + 1) * 128)
                qc = q_ref[0, rs, cs]
                kc = k_ref[0, rs, cs]
                vc = v_ref[0, rs, cs]
                st = st_ref[dr, h]
                sc = lax.dot_general(qc, kc, _NT, preferred_element_type=F32) * intra_ref[dr, h]
                o = _dot(sc.astype(BF16), vc) + _dot(qc, st.astype(BF16)) * xi_ref[dr, h]
                kz = (kc.astype(F32) * zeta_ref[dr, h]).T.astype(BF16)
                st_ref[dr, h] = st * dc_ref[dr, h] + _dot(kz, vc)
                o_ref[0, rs, cs] = o.astype(BF16)


def _even_out_kernel(x_ref, m_ref, of_ref, ob_ref, gf_ref, gb_ref, bv_ref, bg_ref,
                     pv_ref, pg_ref, nv_ref, ng_ref, gn_ref, cw_ref, cb_ref, lng_ref, lnb_ref,
                     wo_ref, n2_ref, rwt_ref, xo_ref, aff_ref, u_scr, ush_scr, *, n_lat_tiles):
    i = pl.program_id(1)
    hl = CONV_HALO

    def glu(v_ref, g_ref):
        return v_ref[0].astype(F32) * jax.nn.sigmoid(g_ref[0].astype(F32))

    prev_ok = jnp.logical_and(i >= 1, i < n_lat_tiles)
    next_ok = i < n_lat_tiles - 1
    u_scr[0:hl, :] = jnp.where(prev_ok, glu(pv_ref, pg_ref), 0.0)
    u_scr[hl:hl + TM, :] = glu(bv_ref, bg_ref)
    u_scr[hl + TM:2 * hl + TM, :] = jnp.where(next_ok, glu(nv_ref, ng_ref), 0.0)
    n_sh = TM + 2 * hl - SUBLANES
    for r in range(1, SUBLANES):
        ush_scr[r - 1] = u_scr[r:r + n_sh, :]
    y = jnp.zeros((TM, CONV_CH), F32) + cb_ref[...]
    off = hl - CONV_WIDTH // 2
    for k in range(CONV_WIDTH):
        r = (k + off) % SUBLANES
        a = k + off - r
        win = u_scr[a:a + TM, :] if r == 0 else ush_scr[r - 1, a:a + TM, :]
        y = y + cw_ref[k:k + 1, :] * win
    mu = jnp.mean(y, axis=-1, keepdims=True)
    yc = y - mu
    var = jnp.mean(yc * yc, axis=-1, keepdims=True)
    conv = _silu(yc * lax.rsqrt(var + EPS) * lng_ref[...] + lnb_ref[...])

    def gnorm(o_ref, g_ref):
        parts = []
        for h in range(RET_HEADS):
            cs = slice(h * 128, (h + 1) * 128)
            o = o_ref[0, :, cs].astype(F32)
            mu_h = jnp.mean(o, axis=-1, keepdims=True)
            oc = o - mu_h
            var_h = jnp.mean(oc * oc, axis=-1, keepdims=True)
            parts.append(oc * lax.rsqrt(var_h + EPS) * gn_ref[:, cs] * _silu(g_ref[0, :, cs].astype(F32)))
        return parts

    pf = gnorm(of_ref, gf_ref)
    pb = gnorm(ob_ref, gb_ref)
    out = _dot(conv.astype(BF16), wo_ref[512:1024, :])
    for h in range(RET_HEADS):
        out = out + _dot((pf[h] + pb[h]).astype(BF16), wo_ref[h * 128:(h + 1) * 128, :])
    _epilogue(x_ref[0], out, m_ref[0], n2_ref[...], rwt_ref[...], xo_ref, aff_ref)


def _odd_in_kernel(x_ref, m_ref, g_ref, win_ref, qag_ref, kvag_ref, wuq_ref, wukv_ref,
                   gq_ref, gk_ref, cos_ref, sin_ref, q_ref, kt_ref, v_ref):
    d = D_MODEL
    nh = MLA_HEADS
    m = m_ref[0]
    h = _modulate(x_ref[0], g_ref[...], m[:, 0:d], m[:, d:2 * d]).astype(BF16)
    c = _dot(h, win_ref[...])
    cq = _rms(c[:, 0:512], qag_ref[...]).astype(BF16)
    ckv = _rms(c[:, 512:768], kvag_ref[...]).astype(BF16)
    kpe = c[:, 768:896]
    kpe_sw = c[:, 896:1024]
    qa = _dot(cq, wuq_ref[...])
    kva = _dot(ckv, wukv_ref[...])
    cos = cos_ref[...]
    sin = sin_ref[...]
    gq = gq_ref[...]
    gk = gk_ref[...]
    lane = lax.broadcasted_iota(I32, (TM, 128), 1)
    lo = lane < MLA_ROPE
    inv_d = 1.0 / MLA_QK

    ss_kpe = 0.5 * jnp.sum(kpe * kpe, axis=-1, keepdims=True)
    kr = kpe * gk[1:2] * cos + kpe_sw * gk[2:3] * sin
    for p in range(nh // 2):
        r = qa[:, 1024 + p * 128:1024 + (p + 1) * 128]
        r_sw = qa[:, 1536 + p * 128:1536 + (p + 1) * 128]
        r2 = r * r
        ss_r = (jnp.sum(jnp.where(lo, r2, 0.0), axis=-1, keepdims=True),
                jnp.sum(jnp.where(lo, 0.0, r2), axis=-1, keepdims=True))
        qr = r * gq[1:2] * cos + r_sw * gq[2:3] * sin
        for s in range(2):
            hd = 2 * p + s
            keep = lo if s == 0 else jnp.logical_not(lo)
            qn = qa[:, hd * 128:(hd + 1) * 128]
            nq = lax.rsqrt((jnp.sum(qn * qn, axis=-1, keepdims=True) + ss_r[s]) * inv_d + EPS) * (MLA_QK ** -0.5 * LOG2E)
            q_ref[0, hd, :, 0:128] = (qn * nq * gq[0:1]).astype(BF16)
            q_ref[0, hd, :, 128:256] = jnp.where(keep, qr * nq, 0.0).astype(BF16)
            kn = kva[:, hd * 128:(hd + 1) * 128]
            nk = lax.rsqrt((jnp.sum(kn * kn, axis=-1, keepdims=True) + ss_kpe) * inv_d + EPS)
            k_full = jnp.concatenate([kn * nk * gk[0:1], jnp.where(keep, kr * nk, 0.0)], axis=-1)
            kt_ref[0, hd, 0] = k_full.T.astype(BF16)
            v_ref[0, hd, :, 0:MLA_V] = kva[:, 1024 + hd * 128:1024 + (hd + 1) * 128].astype(BF16)
            v_ref[0, hd, :, MLA_V:2 * MLA_V] = jnp.ones((TM, MLA_V), BF16)


def _attn_kernel(q_ref, kt_ref, v_ref, o_ref, s_scr, m_scr, acc_scr, *, tps):
    tq = q_ref.shape[2]
    nsub = tq // ATT_SUB
    kw = kt_ref.shape[4]
    n_steps = kt_ref.shape[2] // tps
    n_units = (nsub // ATT_GROUP) * n_steps

    def split(u):
        g = u // n_steps
        return g, u - g * n_steps

    def qk(u, buf):
        g, j = split(u)
        for k in range(ATT_GROUP):
            q = q_ref[0, 0, pl.ds(pl.multiple_of((g * ATT_GROUP + k) * ATT_SUB, ATT_SUB), ATT_SUB), :]
            for c in range(tps):
                s_scr[buf, k, :, c * kw:(c + 1) * kw] = _dot(q, kt_ref[0, 0, j * tps + c])

    def smpv(u, buf):
        g, j = split(u)
        vj = v_ref[0, 0, pl.ds(pl.multiple_of(j * (tps * kw), tps * kw), tps * kw), :]
        for k in range(ATT_GROUP):
            s = g * ATT_GROUP + k
            sv = s_scr[buf, k]
            m = m_scr[s]
            m_new = jnp.maximum(m, jnp.max(sv, axis=-1, keepdims=True))
            alpha = jnp.exp2(m - m_new)
            p = jnp.exp2(sv - m_new).astype(BF16)
            acc_scr[s] = alpha * acc_scr[s] + _dot(p, vj)
            m_scr[s] = m_new

    m_scr[...] = jnp.full(m_scr.shape, -jnp.inf, F32)
    acc_scr[...] = jnp.zeros(acc_scr.shape, F32)
    qk(0, 0)

    def body(i, carry):
        for r in range(ATT_UNROLL):
            u = ATT_UNROLL * i + r
            qk(u + 1, (r + 1) % 2)
            smpv(u, r % 2)
        return carry

    n_loop = (n_units - 1) // ATT_UNROLL
    lax.fori_loop(0, n_loop, body, 0)
    for u in range(ATT_UNROLL * n_loop, n_units):
        if u + 1 < n_units:
            qk(u + 1, (u + 1) % 2)
        smpv(u, u % 2)
    for s in range(nsub):
        acc = acc_scr[s]
        o_ref[0, s * ATT_SUB:(s + 1) * ATT_SUB, :] = (acc[:, 0:MLA_V] / acc[:, MLA_V:2 * MLA_V]).astype(BF16)


def _attn_ctx_kernel(q_ref, kt_ref, v_ref, o_ref):
    s = _dot(q_ref[0, 0], kt_ref[0, 0, 0])
    p = jnp.exp2(s - jnp.max(s, axis=-1, keepdims=True))
    acc = _dot(p.astype(BF16), v_ref[0, 0])
    o_ref[0] = (acc[:, 0:MLA_V] / acc[:, MLA_V:2 * MLA_V]).astype(BF16)


def _odd_out_kernel(x_ref, m_ref, o_ref, oc_ref, wo_ref, n2_ref, rwt_ref, xo_ref, aff_ref, *, n_lat_tiles):
    o = jnp.where(pl.program_id(1) == n_lat_tiles, oc_ref[0], o_ref[0])
    out = _dot(o, wo_ref[...])
    _epilogue(x_ref[0], out, m_ref[0], n2_ref[...], rwt_ref[...], xo_ref, aff_ref)


def _ind(c):
    return jnp.where(c, 1.0, 0.0)


def _select_kernel(a_ref, idx_ref, gate_ref, sel_scr, l_scr, a_scr, *, segs):
    ne = N_EXPERTS
    ri = lax.broadcasted_iota(I32, (LANES, LANES), 0)
    ci = lax.broadcasted_iota(I32, (LANES, LANES), 1)
    upper = _ind(ri <= ci).astype(BF16)
    ones8 = jnp.ones((8, LANES), BF16)
    for k0, nt, cap, c0 in segs:
        a = a_ref[0, k0:k0 + nt]
        bits = lax.bitcast_convert_type(a, I32)

        def sbody(i, prefix, bits=bits, cap=cap):
            cand = prefix | jnp.left_shift(jnp.int32(1), 30 - i)
            cnt = jnp.sum(jnp.sum(_ind(bits >= cand[None]), axis=0), axis=-1, keepdims=True)
            return jnp.where(cnt >= cap, cand, prefix)

        thr = lax.fori_loop(0, 31, sbody, jnp.zeros((ne, 1), I32))[None]
        gt = bits > thr
        eqf = _ind(bits == thr)
        need = cap - jnp.sum(jnp.sum(_ind(gt), axis=0), axis=-1, keepdims=True)
        leq = _dot(eqf.reshape(nt * ne, LANES).astype(BF16), upper).reshape(nt, ne, LANES)
        carry = jnp.zeros((ne, 1), F32)
        sel = []
        for k in range(nt):
            rank = carry + leq[k] - eqf[k]
            sel.append(jnp.where(gt[k], 1.0, jnp.where(rank < need, eqf[k], 0.0)))
            carry = carry + leq[k][:, LANES - 1:LANES]
        sel = jnp.stack(sel)
        pad = jnp.zeros((LANES - nt, ne, LANES), F32)
        sel_scr[0:nt] = sel
        sel_scr[nt:LANES] = pad
        l_scr[0:nt] = _dot(sel.reshape(nt * ne, LANES).astype(BF16), upper).reshape(nt, ne, LANES)
        l_scr[nt:LANES] = pad
        a_hi = a.astype(BF16).astype(F32)
        a_mid = (a - a_hi).astype(BF16).astype(F32)
        a_scr[0, 0:nt] = a_hi
        a_scr[1, 0:nt] = a_mid
        a_scr[2, 0:nt] = a - a_hi - a_mid
        for part in range(3):
            a_scr[part, nt:LANES] = pad

        rows = -(-cap // LANES) * LANES
        slot = lax.broadcasted_iota(I32, (rows, LANES), 0).astype(F32)
        lane = lax.broadcasted_iota(I32, (rows, LANES), 1)
        lane_f = lane.astype(F32)
        idx_all = jnp.zeros((rows, LANES), F32)
        gate_all = jnp.zeros((rows, LANES), F32)
        for e in range(ne):
            sel_e = sel_scr[:, e, :].astype(BF16)
            tot = lax.dot_general(ones8, sel_e, _NT, preferred_element_type=F32)
            pb = _dot(tot.astype(BF16), upper)[0:1]
            le = pb <= slot
            kc = jnp.sum(_ind(le), axis=-1, keepdims=True)
            base = jnp.sum(jnp.where(le, tot[0:1], 0.0), axis=-1, keepdims=True)
            onehot = _ind(lane_f == kc).astype(BF16)
            rhs = jnp.concatenate([l_scr[:, e, :], a_scr[0, :, e, :], a_scr[1, :, e, :], a_scr[2, :, e, :]],
                                  axis=-1).astype(BF16)
            g = _dot(onehot, rhs)
            off = jnp.sum(_ind(g[:, 0:LANES] <= slot - base), axis=-1, keepdims=True)
            aff = (g[:, LANES:2 * LANES] + g[:, 2 * LANES:3 * LANES]) + g[:, 3 * LANES:4 * LANES]
            gate = jnp.sum(jnp.where(lane_f == off, aff, 0.0), axis=-1, keepdims=True)
            idx_all = jnp.where(lane == e, (kc + k0) * LANES + off, idx_all)
            gate_all = jnp.where(lane == e, gate, gate_all)
        idx_ref[0, :, c0:c0 + cap] = idx_all.T[0:ne, 0:cap].astype(I32)
        gate_ref[0, :, c0:c0 + cap] = gate_all.T[0:ne, 0:cap]


def _gather_kernel(idx_ref, x_ref, m_ref, n2_ref, o_ref, rows, *, c_lat):
    d = D_MODEL
    c_tot = rows.shape[0]

    def body(r, carry):
        t = idx_ref[0, 0, r]
        rows[pl.ds(r, 1), :] = x_ref[0, pl.ds(t, 1), :]
        return carry

    lax.fori_loop(0, c_tot, body, 0, unroll=8)
    n2 = n2_ref[...]
    for lo_, hi_, mrow in ((0, c_lat, 0), (c_lat, c_tot, 1)):
        m = m_ref[mrow]
        o_ref[0, 0, lo_:hi_, :] = _modulate(rows[lo_:hi_, :], n2, m[:, 3 * d:4 * d], m[:, 4 * d:5 * d]).astype(BF16)


def _ffn_kernel(x_ref, wg_ref, wu_ref, wd_ref, o_ref):
    f = pl.program_id(2)
    x = x_ref[0, 0]
    a = _dot(x, wg_ref[0, 0].astype(BF16))
    u = _dot(x, wu_ref[0, 0].astype(BF16))
    y = _dot((_silu(a) * u).astype(BF16), wd_ref[0, 0].astype(BF16))

    @pl.when(f == 0)
    def _():
        o_ref[0, 0] = y

    @pl.when(f > 0)
    def _():
        o_ref[0, 0] += y


def _combine_kernel(idx_ref, gate_ref, x_hbm, y_ref, m_ref, o_hbm, acc, sem, *, c_lat):
    d = D_MODEL
    b = pl.program_id(0)
    e = pl.program_id(1)
    c_tot = y_ref.shape[2]

    @pl.when(e == 0)
    def _():
        cp = pltpu.make_async_copy(x_hbm.at[b], acc, sem)
        cp.start()
        cp.wait()

    for lo_, hi_, mrow in ((0, c_lat, 0), (c_lat, c_tot, 1)):
        g2 = m_ref[mrow][:, 5 * d:6 * d]

        def body(i, carry, g2=g2, lo_=lo_):
            base = pl.multiple_of(lo_ + i * ROW_GROUP, ROW_GROUP)
            ts = [idx_ref[0, 0, base + k] for k in range(ROW_GROUP)]
            new = [acc[pl.ds(ts[k], 1), :] + y_ref[0, 0, pl.ds(base + k, 1), :] * (gate_ref[0, 0, base + k] * g2)
                   for k in range(ROW_GROUP)]
            for k in range(ROW_GROUP):
                acc[pl.ds(ts[k], 1), :] = new[k]
            return carry

        lax.fori_loop(0, (hi_ - lo_) // ROW_GROUP, body, 0)

    @pl.when(e == pl.num_programs(1) - 1)
    def _():
        cp = pltpu.make_async_copy(acc.at[pl.ds(0, o_hbm.shape[1])], o_hbm.at[b], sem)
        cp.start()
        cp.wait()


def _moe(xs, aff_t, modtab, n2g, l, w_gate, w_up, w_down, n_lat, latent_only):
    bsz, t_all, d = xs.shape
    n_ctx = t_all - n_lat
    ne = N_EXPERTS
    c_lat = max(1, (EC_CAPACITY_FACTOR * n_lat) // ne)
    c_ctx = max(1, (EC_CAPACITY_FACTOR * n_ctx) // ne)
    c_tot = c_lat + c_ctx
    assert c_lat % ROW_GROUP == 0 and c_ctx % ROW_GROUP == 0
    assert n_lat % LANES == 0 and n_ctx % LANES == 0 and t_all // LANES <= LANES
    segs = ((0, n_lat // LANES, c_lat, 0), (n_lat // LANES, n_ctx // LANES, c_ctx, c_lat))
    idx, gates = pl.pallas_call(
        functools.partial(_select_kernel, segs=segs),
        grid=(bsz,),
        in_specs=[pl.BlockSpec((1, t_all // LANES, ne, LANES), lambda b: (b, 0, 0, 0))],
        out_specs=[pl.BlockSpec((1, ne, c_tot), lambda b: (b, 0, 0)), pl.BlockSpec((1, ne, c_tot), lambda b: (b, 0, 0))],
        out_shape=[jax.ShapeDtypeStruct((bsz, ne, c_tot), I32), jax.ShapeDtypeStruct((bsz, ne, c_tot), F32)],
        scratch_shapes=[pltpu.VMEM((LANES, ne, LANES), F32), pltpu.VMEM((LANES, ne, LANES), F32),
                        pltpu.VMEM((3, LANES, ne, LANES), F32)],
        compiler_params=_cparams(("arbitrary",), VMEM_LIMIT),
        name="moe_select",
    )(aff_t)
    idx = idx.reshape(bsz * ne, 1, c_tot)
    gates = gates.reshape(bsz * ne, 1, c_tot)
    smem = functools.partial(pl.BlockSpec, memory_space=pltpu.SMEM)

    xg = pl.pallas_call(
        functools.partial(_gather_kernel, c_lat=c_lat),
        grid=(bsz, ne),
        in_specs=[smem((1, 1, c_tot), lambda b, e: (b * ne + e, 0, 0)),
                  pl.BlockSpec((1, t_all, d), lambda b, e: (b, 0, 0), pipeline_mode=pl.Buffered(1)),
                  pl.BlockSpec((2, 1, 6 * d), lambda b, e: (b, 0, 0)),
                  pl.BlockSpec((1, d), lambda b, e: (0, 0))],
        out_specs=pl.BlockSpec((1, 1, c_tot, d), lambda b, e: (b, e, 0, 0)),
        out_shape=jax.ShapeDtypeStruct((bsz, ne, c_tot, d), BF16),
        scratch_shapes=[pltpu.VMEM((c_tot, d), F32)],
        compiler_params=_cparams(("arbitrary", "arbitrary"), VMEM_LIMIT),
        name="moe_gather",
    )(idx, xs, modtab, n2g)

    nf = EXPERT_FF // FF_TILE
    y = pl.pallas_call(
        _ffn_kernel,
        grid=(bsz, ne, nf),
        in_specs=[pl.BlockSpec((1, 1, c_tot, d), lambda b, e, f: (b, e, 0, 0)),
                  pl.BlockSpec((1, 1, d, FF_TILE), lambda b, e, f: (l, e, 0, f)),
                  pl.BlockSpec((1, 1, d, FF_TILE), lambda b, e, f: (l, e, 0, f)),
                  pl.BlockSpec((1, 1, FF_TILE, d), lambda b, e, f: (l, e, f, 0))],
        out_specs=pl.BlockSpec((1, 1, c_tot, d), lambda b, e, f: (b, e, 0, 0)),
        out_shape=jax.ShapeDtypeStruct((bsz, ne, c_tot, d), F32),
        compiler_params=_cparams(("arbitrary", "arbitrary", "arbitrary"), VMEM_LIMIT),
        name="moe_ffn",
    )(xg, w_gate, w_up, w_down)

    return pl.pallas_call(
        functools.partial(_combine_kernel, c_lat=c_lat),
        grid=(bsz, ne),
        in_specs=[smem((1, 1, c_tot), lambda b, e: (b * ne + e, 0, 0)),
                  smem((1, 1, c_tot), lambda b, e: (b * ne + e, 0, 0)),
                  pl.BlockSpec(memory_space=pl.ANY),
                  pl.BlockSpec((1, 1, c_tot, d), lambda b, e: (b, e, 0, 0)),
                  pl.BlockSpec((2, 1, 6 * d), lambda b, e: (b, 0, 0))],
        out_specs=pl.BlockSpec(memory_space=pl.ANY),
        out_shape=jax.ShapeDtypeStruct((bsz, n_lat if latent_only else t_all, d), F32),
        scratch_shapes=[pltpu.VMEM((t_all, d), F32), pltpu.SemaphoreType.DMA(())],
        input_output_aliases={} if latent_only else {2: 0},
        compiler_params=_cparams(("arbitrary", "arbitrary"), VMEM_LIMIT),
        name="moe_combine",
    )(idx, gates, xs, y, modtab)


def _tile_specs(d, n_lat_tiles):
    x_spec = pl.BlockSpec((1, TM, d), lambda b, i: (b, i, 0))
    m_spec = pl.BlockSpec((1, 1, 6 * d), lambda b, i: (2 * b + i // n_lat_tiles, 0, 0))
    return x_spec, m_spec


def _const_spec(shape):
    return pl.BlockSpec(shape, lambda b, i: (0,) * len(shape))


def _even_layer(xs, modtab, n1g, n2g, rwt, w_in_ext, lg, gn_g, conv_w, conv_b, ln_g, ln_b, w_out, cos, sin, n_lat):
    bsz, t_all, d = xs.shape
    nt = t_all // TM
    nlt = n_lat // TM
    x_spec, m_spec = _tile_specs(d, nlt)
    n_cols = 7 * 512
    proj = pl.pallas_call(
        _even_in_kernel,
        grid=(bsz, nt),
        in_specs=[x_spec, m_spec, _const_spec((1, d)), _const_spec(w_in_ext.shape),
                  pl.BlockSpec((TM, 512), lambda b, i: (i, 0)), pl.BlockSpec((TM, 512), lambda b, i: (i, 0))],
        out_specs=pl.BlockSpec((1, TM, n_cols), lambda b, i: (b, i, 0)),
        out_shape=jax.ShapeDtypeStruct((bsz, t_all, n_cols), BF16),
        compiler_params=_cparams(("arbitrary", "arbitrary"), VMEM_LIMIT),
        name="even_in",
    )(xs, modtab, n1g, w_in_ext, cos, sin)

    fwd = lambda s: jnp.where(s == 0, nlt, s - 1)
    bwd = lambda s: jnp.where(s == 0, nlt, nlt - s)

    def col(order, j):
        return pl.BlockSpec((1, TM, 512), lambda b, s: (b, order(s), j))

    o_f, o_b = pl.pallas_call(
        _ret_kernel,
        grid=(bsz, nt),
        in_specs=[pl.BlockSpec(lg.shape, lambda b, s: (0, 0, 0, 0)),
                  col(fwd, 0), col(fwd, 1), col(fwd, 2), col(bwd, 0), col(bwd, 1), col(bwd, 2)],
        out_specs=[col(fwd, 0), col(bwd, 0)],
        out_shape=[jax.ShapeDtypeStruct((bsz, t_all, 512), BF16)] * 2,
        scratch_shapes=[pltpu.VMEM((2, RET_HEADS, RET_DK, 128), F32),
                        pltpu.VMEM((2, RET_HEADS, RET_CHUNK, RET_CHUNK), F32),
                        pltpu.VMEM((2, RET_HEADS, RET_CHUNK, 128), F32),
                        pltpu.VMEM((2, RET_HEADS, RET_CHUNK, 128), F32),
                        pltpu.VMEM((2, RET_HEADS, 1, 128), F32)],
        compiler_params=_cparams(("arbitrary", "arbitrary"), VMEM_LIMIT),
        name="retention",
    )(lg, proj, proj, proj, proj, proj, proj)

    hpt = TM // CONV_HALO
    n_halo = t_all // CONV_HALO

    def tcol(j):
        return pl.BlockSpec((1, TM, 512), lambda b, i: (b, i, j))

    def halo(j, nxt):
        if nxt:
            return pl.BlockSpec((1, CONV_HALO, 512), lambda b, i: (b, jnp.minimum((i + 1) * hpt, n_halo - 1), j))
        return pl.BlockSpec((1, CONV_HALO, 512), lambda b, i: (b, jnp.maximum(i * hpt - 1, 0), j))

    xs_new, aff_t = pl.pallas_call(
        functools.partial(_even_out_kernel, n_lat_tiles=nlt),
        grid=(bsz, nt),
        in_specs=[x_spec, m_spec, pl.BlockSpec((1, TM, 512), lambda b, i: (b, i, 0)),
                  pl.BlockSpec((1, TM, 512), lambda b, i: (b, i, 0)),
                  tcol(3), tcol(4), tcol(5), tcol(6), halo(5, False), halo(6, False), halo(5, True), halo(6, True),
                  _const_spec((1, 512)), _const_spec(conv_w.shape), _const_spec((1, 512)), _const_spec((1, 512)),
                  _const_spec((1, 512)), _const_spec(w_out.shape), _const_spec((1, d)), _const_spec(rwt.shape)],
        out_specs=[x_spec, pl.BlockSpec((1, TM // LANES, N_EXPERTS, LANES), lambda b, i: (b, i, 0, 0))],
        out_shape=[jax.ShapeDtypeStruct(xs.shape, F32), jax.ShapeDtypeStruct((bsz, t_all // LANES, N_EXPERTS, LANES), F32)],
        scratch_shapes=[pltpu.VMEM((TM + 2 * CONV_HALO, CONV_CH), F32),
                        pltpu.VMEM((SUBLANES - 1, TM + 2 * CONV_HALO - SUBLANES, CONV_CH), F32)],
        input_output_aliases={0: 0},
        compiler_params=_cparams(("arbitrary", "arbitrary"), VMEM_LIMIT),
        name="even_out",
    )(xs, modtab, o_f, o_b, proj, proj, proj, proj, proj, proj, proj, proj,
      gn_g, conv_w, conv_b, ln_g, ln_b, w_out, n2g, rwt)
    return xs_new, aff_t


def _odd_layer(xs, modtab, n1g, n2g, rwt, w_in_ext, qag, kvag, w_uq_ext, w_ukv_p, gq, gk, w_out, cos, sin, n_lat):
    bsz, t_all, d = xs.shape
    nt = t_all // TM
    nlt = n_lat // TM
    nh = MLA_HEADS
    x_spec, m_spec = _tile_specs(d, nlt)
    q, kt, v = pl.pallas_call(
        _odd_in_kernel,
        grid=(bsz, nt),
        in_specs=[x_spec, m_spec, _const_spec((1, d)), _const_spec(w_in_ext.shape), _const_spec(qag.shape),
                  _const_spec(kvag.shape), _const_spec(w_uq_ext.shape), _const_spec(w_ukv_p.shape),
                  _const_spec(gq.shape), _const_spec(gk.shape),
                  pl.BlockSpec((TM, 128), lambda b, i: (i, 0)), pl.BlockSpec((TM, 128), lambda b, i: (i, 0))],
        out_specs=[pl.BlockSpec((1, nh, TM, MLA_QK_PAD), lambda b, i: (b, 0, i, 0)),
                   pl.BlockSpec((1, nh, 1, MLA_QK_PAD, TM), lambda b, i: (b, 0, i, 0, 0)),
                   pl.BlockSpec((1, nh, TM, 2 * MLA_V), lambda b, i: (b, 0, i, 0))],
        out_shape=[jax.ShapeDtypeStruct((bsz, nh, t_all, MLA_QK_PAD), BF16),
                   jax.ShapeDtypeStruct((bsz, nh, nt, MLA_QK_PAD, TM), BF16),
                   jax.ShapeDtypeStruct((bsz, nh, t_all, 2 * MLA_V), BF16)],
        compiler_params=_cparams(("arbitrary", "arbitrary"), VMEM_LIMIT),
        name="odd_in",
    )(xs, modtab, n1g, w_in_ext, qag, kvag, w_uq_ext, w_ukv_p, gq, gk, cos, sin)

    tps = next(c for c in ATT_KT_PER_STEP if nt % c == 0)
    nsub = ATT_TQ // ATT_SUB
    o = pl.pallas_call(
        functools.partial(_attn_kernel, tps=tps),
        grid=(bsz, nh, n_lat // ATT_TQ),
        in_specs=[pl.BlockSpec((1, 1, ATT_TQ, MLA_QK_PAD), lambda b, h, i: (b, h, i, 0)),
                  pl.BlockSpec((1, 1, nt, MLA_QK_PAD, TM), lambda b, h, i: (b, h, 0, 0, 0)),
                  pl.BlockSpec((1, 1, t_all, 2 * MLA_V), lambda b, h, i: (b, h, 0, 0))],
        out_specs=pl.BlockSpec((1, ATT_TQ, MLA_V), lambda b, h, i: (b, i, h)),
        out_shape=jax.ShapeDtypeStruct((bsz, n_lat, nh * MLA_V), BF16),
        scratch_shapes=[pltpu.VMEM((2, ATT_GROUP, ATT_SUB, tps * TM), F32), pltpu.VMEM((nsub, ATT_SUB, 1), F32),
                        pltpu.VMEM((nsub, ATT_SUB, 2 * MLA_V), F32)],
        compiler_params=_cparams(("arbitrary", "arbitrary", "arbitrary"), VMEM_LIMIT),
        name="mla_attention",
    )(q, kt, v)
    o_ctx = pl.pallas_call(
        _attn_ctx_kernel,
        grid=(bsz, nh),
        in_specs=[pl.BlockSpec((1, 1, TM, MLA_QK_PAD), lambda b, h: (b, h, nlt, 0)),
                  pl.BlockSpec((1, 1, 1, MLA_QK_PAD, TM), lambda b, h: (b, h, nlt, 0, 0)),
                  pl.BlockSpec((1, 1, TM, 2 * MLA_V), lambda b, h: (b, h, nlt, 0))],
        out_specs=pl.BlockSpec((1, TM, MLA_V), lambda b, h: (b, 0, h)),
        out_shape=jax.ShapeDtypeStruct((bsz, TM, nh * MLA_V), BF16),
        compiler_params=_cparams(("arbitrary", "arbitrary"), VMEM_LIMIT),
        name="mla_attention_ctx",
    )(q, kt, v)

    xs_new, aff_t = pl.pallas_call(
        functools.partial(_odd_out_kernel, n_lat_tiles=nlt),
        grid=(bsz, nt),
        in_specs=[x_spec, m_spec, pl.BlockSpec((1, TM, nh * MLA_V), lambda b, i: (b, jnp.minimum(i, nlt - 1), 0)),
                  pl.BlockSpec((1, TM, nh * MLA_V), lambda b, i: (b, 0, 0)),
                  _const_spec(w_out.shape), _const_spec((1, d)), _const_spec(rwt.shape)],
        out_specs=[x_spec, pl.BlockSpec((1, TM // LANES, N_EXPERTS, LANES), lambda b, i: (b, i, 0, 0))],
        out_shape=[jax.ShapeDtypeStruct(xs.shape, F32), jax.ShapeDtypeStruct((bsz, t_all // LANES, N_EXPERTS, LANES), F32)],
        input_output_aliases={0: 0},
        compiler_params=_cparams(("arbitrary", "arbitrary"), VMEM_LIMIT),
        name="odd_out",
    )(xs, modtab, o, o_ctx, w_out, n2g, rwt)
    return xs_new, aff_t


def _rope_tables(n_lat, n_ctx, nf):
    t = jnp.arange(n_lat)
    row = (t // GRID_W).astype(F32)
    col = (t % GRID_W).astype(F32)
    inv = ROPE_BASE ** (-(jnp.arange(nf, dtype=F32) / nf))
    ar = row[:, None] * inv[None, :]
    ac = col[:, None] * inv[None, :]
    cos = jnp.concatenate([jnp.cos(ar), jnp.cos(ar), jnp.cos(ac), jnp.cos(ac)], axis=-1)
    sin = jnp.concatenate([-jnp.sin(ar), jnp.sin(ar), -jnp.sin(ac), jnp.sin(ac)], axis=-1)
    cos = jnp.concatenate([cos, jnp.ones((n_ctx, 4 * nf), F32)], axis=0)
    sin = jnp.concatenate([sin, jnp.zeros((n_ctx, 4 * nf), F32)], axis=0)
    return cos, sin


def _swap_perm(nf):
    a = jnp.arange(nf)
    return jnp.concatenate([a + nf, a, a + 3 * nf, a + 2 * nf])


def kernel(x, c, ctx, c_ctx, mod_w, mod_b, norm1_g, norm2_g, ev_w_in, ret_decay_f, ret_decay_b, ret_gn_g, conv_w, conv_b, conv_ln_g, conv_ln_b, ev_w_out, od_w_in, mla_q_a_g, mla_kv_a_g, mla_w_uq, mla_w_ukv, mla_qk_g_q, mla_qk_g_k, od_w_out, router_w, moe_w_gate, moe_w_up, moe_w_down):
    bsz, n_lat, d = x.shape
    n_ctx = ctx.shape[1]
    depth = mod_w.shape[0]
    assert d == D_MODEL and n_ctx == TM and n_lat % ATT_TQ == 0 and bsz + 1 <= 8

    xs = jnp.concatenate([x, ctx], axis=1)
    cc = jnp.zeros((8, d), F32).at[:bsz].set(c).at[bsz].set(c_ctx)
    mod = _mod_all(cc, mod_w, mod_b)

    cos_e, sin_e = _rope_tables(n_lat, n_ctx, RET_DK // 4)
    cos_e, sin_e = jnp.tile(cos_e, (1, RET_HEADS)), jnp.tile(sin_e, (1, RET_HEADS))
    cos_o, sin_o = _rope_tables(n_lat, n_ctx, MLA_ROPE // 4)
    cos_o, sin_o = jnp.tile(cos_o, (1, 2)), jnp.tile(sin_o, (1, 2))
    perm_e = (jnp.arange(RET_HEADS)[:, None] * RET_DK + _swap_perm(RET_DK // 4)[None, :]).reshape(-1)
    perm_o = _swap_perm(MLA_ROPE // 4)

    for l in range(depth):
        i = l // 2
        modtab = jnp.stack([mod[l, :bsz], jnp.broadcast_to(mod[l, bsz], (bsz, 6 * d))], axis=1).reshape(bsz * 2, 1, 6 * d)
        n1g = norm1_g[l][None]
        n2g = norm2_g[l][None]
        rw = router_w[l]
        rw_hi = rw.astype(BF16)
        rw_lo = (rw - rw_hi.astype(F32)).astype(BF16)
        pad = jnp.zeros((d, 128 - N_EXPERTS), BF16)
        rwt = jnp.concatenate([rw_hi, pad, rw_lo, pad], axis=1)
        if l % 2 == 0:
            w = ev_w_in[i]
            w_ext = jnp.concatenate([w, w[:, 0:512][:, perm_e], w[:, 512:1024][:, perm_e]], axis=1).astype(BF16)
            lg = jnp.broadcast_to(jnp.stack([ret_decay_f[i], ret_decay_b[i]])[:, :, None, None], (2, RET_HEADS, 1, 128))
            cw = jnp.concatenate([conv_w[i], jnp.zeros((1, CONV_CH), F32)], axis=0)
            xs, aff_t = _even_layer(xs, modtab, n1g, n2g, rwt, w_ext, lg, ret_gn_g[i][None], cw, conv_b[i][None],
                                    conv_ln_g[i][None], conv_ln_b[i][None], ev_w_out[i].astype(BF16), cos_e, sin_e, n_lat)
        else:
            w = od_w_in[i]
            kpe = w[:, MLA_Q_LORA + MLA_KV_LORA:]
            w_ext = jnp.concatenate([w[:, :MLA_Q_LORA + MLA_KV_LORA], kpe, kpe, kpe[:, perm_o], kpe[:, perm_o]], axis=1).astype(BF16)
            wq = mla_w_uq[i].reshape(MLA_Q_LORA, MLA_HEADS, MLA_QK)
            wq_r = wq[:, :, MLA_NOPE:]
            w_uq_ext = jnp.concatenate([wq[:, :, :MLA_NOPE].reshape(MLA_Q_LORA, -1), wq_r.reshape(MLA_Q_LORA, -1),
                                        wq_r[:, :, perm_o].reshape(MLA_Q_LORA, -1)], axis=1).astype(BF16)
            wkv = mla_w_ukv[i].reshape(MLA_KV_LORA, MLA_HEADS, MLA_NOPE + MLA_V)
            w_ukv_p = jnp.concatenate([wkv[:, :, :MLA_NOPE].reshape(MLA_KV_LORA, -1),
                                       wkv[:, :, MLA_NOPE:].reshape(MLA_KV_LORA, -1)], axis=1).astype(BF16)

            def gains(g):
                gr = g[MLA_NOPE:]
                return jnp.stack([g[:MLA_NOPE], jnp.tile(gr, 2), jnp.tile(gr[perm_o], 2)])

            xs, aff_t = _odd_layer(xs, modtab, n1g, n2g, rwt, w_ext, mla_q_a_g[i][None], mla_kv_a_g[i][None], w_uq_ext,
                                   w_ukv_p, gains(mla_qk_g_q[i]), gains(mla_qk_g_k[i]), od_w_out[i].astype(BF16),
                                   cos_o, sin_o, n_lat)
        xs = _moe(xs, aff_t, modtab, n2g, l, moe_w_gate, moe_w_up, moe_w_down, n_lat, latent_only=l == depth - 1)
    return xs
```

```python
import functools

import jax
import jax.numpy as jnp
from jax import lax
from jax.experimental import pallas as pl
from jax.experimental.pallas import tpu as pltpu

F32 = jnp.float32
BF16 = jnp.bfloat16
I32 = jnp.int32

D_MODEL = 1024
GRID_W = 64
RET_HEADS = 4
RET_DK = 128
RET_CHUNK = 256
CONV_CH = 512
CONV_WIDTH = 31
CONV_HALO = 16
MLA_HEADS = 8
MLA_Q_LORA = 512
MLA_KV_LORA = 256
MLA_NOPE = 128
MLA_ROPE = 64
MLA_V = 128
MLA_QK = MLA_NOPE + MLA_ROPE
MLA_QK_PAD = 256
N_EXPERTS = 16
EC_CAPACITY_FACTOR = 2
EXPERT_FF = 1024
ROPE_BASE = 10000.0
LOG2E = 1.4426950408889634
EPS = 1e-6

LANES = 128
SUBLANES = 8
TM = 256
ATT_TQ = 2048
ATT_SUB = 256
ATT_GROUP = 2
ATT_UNROLL = 4
ATT_KT_PER_STEP = (3, 2, 1)
FF_TILE = 1024
SELECT_REFINE_STEPS = 29
F32_MIN_NORMAL_BITS = 0x00800000
ROW_GROUP = 8
VMEM_LIMIT = 56 * 2 ** 20

_NT = (((1,), (1,)), ((), ()))


def _cparams(sem, vmem=None):
    return pltpu.CompilerParams(dimension_semantics=sem, vmem_limit_bytes=vmem)


def _dot(a, b):
    return jnp.dot(a, b, preferred_element_type=F32)


def _split3(a):
    a1 = a.astype(BF16)
    r = a - a1.astype(F32)
    a2 = r.astype(BF16)
    a3 = (r - a2.astype(F32)).astype(BF16)
    return a1, a2, a3


def _dot_hi(a, b, dn):
    a1, a2, a3 = _split3(a)
    b1, b2, b3 = _split3(b)
    d = lambda x, y: lax.dot_general(x, y, dn, preferred_element_type=F32)
    return ((d(a3, b1) + d(a2, b2) + d(a1, b3)) + (d(a2, b1) + d(a1, b2))) + d(a1, b1)


def _silu(a):
    return a * jax.nn.sigmoid(a)


def _rms(x, g):
    return x * lax.rsqrt(jnp.mean(x * x, axis=-1, keepdims=True) + EPS) * g


def _modulate(x, g, shift, scale):
    return _rms(x, g) * (1.0 + scale) + shift


def _mod_kernel(c_ref, w_ref, b_ref, o_ref):
    s = _silu(c_ref[...])
    o_ref[0] = _dot_hi(s, w_ref[0], (((1,), (0,)), ((), ()))) + b_ref[0]


def _mod_all(cc, mod_w, mod_b):
    depth, d, n6 = mod_w.shape
    tn = 1536
    return pl.pallas_call(
        _mod_kernel,
        grid=(depth, n6 // tn),
        in_specs=[pl.BlockSpec((8, d), lambda l, j: (0, 0)),
                  pl.BlockSpec((1, d, tn), lambda l, j: (l, 0, j)),
                  pl.BlockSpec((1, 1, tn), lambda l, j: (l, 0, j))],
        out_specs=pl.BlockSpec((1, 8, tn), lambda l, j: (l, 0, j)),
        out_shape=jax.ShapeDtypeStruct((depth, 8, n6), F32),
        compiler_params=_cparams(("arbitrary", "arbitrary"), VMEM_LIMIT),
        name="mod_all",
    )(cc, mod_w, mod_b.reshape(depth, 1, n6))


def _epilogue(x, out, m, n2g, rwt, xo_ref, hrow_ref, aff_ref):
    d = D_MODEL
    xn = x + m[:, 2 * d:3 * d] * out
    xo_ref[0] = xn
    h2 = _modulate(xn, n2g, m[:, 3 * d:4 * d], m[:, 4 * d:5 * d])
    hrow_ref[0] = h2.reshape(TM, SUBLANES, LANES)
    h_hi = h2.astype(BF16)
    h_lo = (h2 - h_hi.astype(F32)).astype(BF16)
    z = _dot(h_hi, rwt)
    logits = (z[:, 0:128] + z[:, 128:256]) + _dot(h_lo, rwt[:, 0:128])
    lane = lax.broadcasted_iota(I32, logits.shape, 1)
    logits = jnp.where(lane < N_EXPERTS, logits, -jnp.inf)
    e = jnp.exp(logits - jnp.max(logits, axis=-1, keepdims=True))
    aff = e / jnp.sum(e, axis=-1, keepdims=True)
    aff_t = aff.T
    for k in range(TM // LANES):
        aff_ref[0, k] = aff_t[0:N_EXPERTS, k * LANES:(k + 1) * LANES]


def _even_in_kernel(x_ref, m_ref, g_ref, w_ref, cos_ref, sin_ref, o_ref):
    d = D_MODEL
    m = m_ref[0]
    h = _modulate(x_ref[0], g_ref[...], m[:, 0:d], m[:, d:2 * d]).astype(BF16)

    def grp(j):
        return _dot(h, w_ref[:, j * 512:(j + 1) * 512])

    c = cos_ref[...]
    s = sin_ref[...]
    o_ref[0, :, 0:512] = (grp(0) * c + grp(7) * s).astype(BF16)
    o_ref[0, :, 512:1024] = ((grp(1) * c + grp(8) * s) * (RET_DK ** -0.5)).astype(BF16)
    for j in range(2, 7):
        o_ref[0, :, j * 512:(j + 1) * 512] = grp(j).astype(BF16)


def _ret_kernel(lg_ref, qf_ref, kf_ref, vf_ref, qb_ref, kb_ref, vb_ref, of_ref, ob_ref,
                st_ref, intra_ref, xi_ref, zeta_ref, dc_ref):
    L = RET_CHUNK
    nc = TM // L

    @pl.when(pl.program_id(1) == 0)
    def _():
        st_ref[...] = jnp.zeros(st_ref.shape, F32)
        ii = lax.broadcasted_iota(I32, (L, L), 0).astype(F32)
        jj = lax.broadcasted_iota(I32, (L, L), 1).astype(F32)
        pos = lax.broadcasted_iota(I32, (L, LANES), 0).astype(F32)
        for dr in range(2):
            for h in range(RET_HEADS):
                x = lg_ref[dr, h]
                lg = jnp.minimum(x, 0.0) - jnp.log(1.0 + jnp.exp(-jnp.abs(x)))
                if dr == 0:
                    diff, xpos, zpos = ii - jj, pos + 1.0, (L - 1.0) - pos
                else:
                    diff, xpos, zpos = jj - ii, L - pos, pos
                intra_ref[dr, h] = jnp.where(diff >= 0, jnp.exp(lg[:, 0:1] * jnp.maximum(diff, 0.0)), 0.0)
                xi_ref[dr, h] = jnp.exp(lg * xpos)
                zeta_ref[dr, h] = jnp.exp(lg * zpos)
                dc_ref[dr, h] = jnp.exp(lg * float(L))

    for dr, (q_ref, k_ref, v_ref, o_ref) in enumerate(((qf_ref, kf_ref, vf_ref, of_ref),
                                                      (qb_ref, kb_ref, vb_ref, ob_ref))):
        order = range(nc) if dr == 0 else range(nc - 1, -1, -1)
        for c in order:
            for h in range(RET_HEADS):
                rs = slice(c * L, (c + 1) * L)
                cs = slice(h * 128, (h + 1) * 128)
                qc = q_ref[0, rs, cs]
                kc = k_ref[0, rs, cs]
                vc = v_ref[0, rs, cs]
                st = st_ref[dr, h]
                sc = lax.dot_general(qc, kc, _NT, preferred_element_type=F32) * intra_ref[dr, h]
                o = _dot(sc.astype(BF16), vc) + _dot(qc, st.astype(BF16)) * xi_ref[dr, h]
                kz = (kc.astype(F32) * zeta_ref[dr, h]).T.astype(BF16)
                st_ref[dr, h] = st * dc_ref[dr, h] + _dot(kz, vc)
                o_ref[0, rs, cs] = o.astype(BF16)


def _even_out_kernel(x_ref, m_ref, of_ref, ob_ref, gf_ref, gb_ref, bv_ref, bg_ref,
                     pv_ref, pg_ref, nv_ref, ng_ref, gn_ref, cw_ref, cb_ref, lng_ref, lnb_ref,
                     wo_ref, n2_ref, rwt_ref, xo_ref, hrow_ref, aff_ref, u_scr, ush_scr, *, n_lat_tiles):
    i = pl.program_id(1)
    hl = CONV_HALO

    def glu(v_ref, g_ref):
        return v_ref[0].astype(F32) * jax.nn.sigmoid(g_ref[0].astype(F32))

    prev_ok = jnp.logical_and(i >= 1, i < n_lat_tiles)
    next_ok = i < n_lat_tiles - 1
    u_scr[0:hl, :] = jnp.where(prev_ok, glu(pv_ref, pg_ref), 0.0)
    u_scr[hl:hl + TM, :] = glu(bv_ref, bg_ref)
    u_scr[hl + TM:2 * hl + TM, :] = jnp.where(next_ok, glu(nv_ref, ng_ref), 0.0)
    n_sh = TM + 2 * hl - SUBLANES
    for r in range(1, SUBLANES):
        ush_scr[r - 1] = u_scr[r:r + n_sh, :]
    y = jnp.zeros((TM, CONV_CH), F32) + cb_ref[...]
    off = hl - CONV_WIDTH // 2
    for k in range(CONV_WIDTH):
        r = (k + off) % SUBLANES
        a = k + off - r
        win = u_scr[a:a + TM, :] if r == 0 else ush_scr[r - 1, a:a + TM, :]
        y = y + cw_ref[k:k + 1, :] * win
    mu = jnp.mean(y, axis=-1, keepdims=True)
    yc = y - mu
    var = jnp.mean(yc * yc, axis=-1, keepdims=True)
    conv = _silu(yc * lax.rsqrt(var + EPS) * lng_ref[...] + lnb_ref[...])

    def gnorm(o_ref, g_ref):
        parts = []
        for h in range(RET_HEADS):
            cs = slice(h * 128, (h + 1) * 128)
            o = o_ref[0, :, cs].astype(F32)
            mu_h = jnp.mean(o, axis=-1, keepdims=True)
            oc = o - mu_h
            var_h = jnp.mean(oc * oc, axis=-1, keepdims=True)
            parts.append(oc * lax.rsqrt(var_h + EPS) * gn_ref[:, cs] * _silu(g_ref[0, :, cs].astype(F32)))
        return parts

    pf = gnorm(of_ref, gf_ref)
    pb = gnorm(ob_ref, gb_ref)
    out = _dot(conv.astype(BF16), wo_ref[512:1024, :])
    for h in range(RET_HEADS):
        out = out + _dot((pf[h] + pb[h]).astype(BF16), wo_ref[h * 128:(h + 1) * 128, :])
    _epilogue(x_ref[0], out, m_ref[0], n2_ref[...], rwt_ref[...], xo_ref, hrow_ref, aff_ref)


def _odd_in_kernel(x_ref, m_ref, g_ref, win_ref, qag_ref, kvag_ref, wuq_ref, wukv_ref,
                   gq_ref, gk_ref, cos_ref, sin_ref, q_ref, kt_ref, v_ref):
    d = D_MODEL
    nh = MLA_HEADS
    m = m_ref[0]
    h = _modulate(x_ref[0], g_ref[...], m[:, 0:d], m[:, d:2 * d]).astype(BF16)
    c = _dot(h, win_ref[...])
    cq = _rms(c[:, 0:512], qag_ref[...]).astype(BF16)
    ckv = _rms(c[:, 512:768], kvag_ref[...]).astype(BF16)
    kpe = c[:, 768:896]
    kpe_sw = c[:, 896:1024]
    qa = _dot(cq, wuq_ref[...])
    kva = _dot(ckv, wukv_ref[...])
    cos = cos_ref[...]
    sin = sin_ref[...]
    gq = gq_ref[...]
    gk = gk_ref[...]
    lane = lax.broadcasted_iota(I32, (TM, 128), 1)
    lo = lane < MLA_ROPE
    inv_d = 1.0 / MLA_QK

    ss_kpe = 0.5 * jnp.sum(kpe * kpe, axis=-1, keepdims=True)
    kr = kpe * gk[1:2] * cos + kpe_sw * gk[2:3] * sin
    for p in range(nh // 2):
        r = qa[:, 1024 + p * 128:1024 + (p + 1) * 128]
        r_sw = qa[:, 1536 + p * 128:1536 + (p + 1) * 128]
        r2 = r * r
        ss_r = (jnp.sum(jnp.where(lo, r2, 0.0), axis=-1, keepdims=True),
                jnp.sum(jnp.where(lo, 0.0, r2), axis=-1, keepdims=True))
        qr = r * gq[1:2] * cos + r_sw * gq[2:3] * sin
        for s in range(2):
            hd = 2 * p + s
            keep = lo if s == 0 else jnp.logical_not(lo)
            qn = qa[:, hd * 128:(hd + 1) * 128]
            nq = lax.rsqrt((jnp.sum(qn * qn, axis=-1, keepdims=True) + ss_r[s]) * inv_d + EPS) * (MLA_QK ** -0.5 * LOG2E)
            q_ref[0, hd, :, 0:128] = (qn * nq * gq[0:1]).astype(BF16)
            q_ref[0, hd, :, 128:256] = jnp.where(keep, qr * nq, 0.0).astype(BF16)
            kn = kva[:, hd * 128:(hd + 1) * 128]
            nk = lax.rsqrt((jnp.sum(kn * kn, axis=-1, keepdims=True) + ss_kpe) * inv_d + EPS)
            k_full = jnp.concatenate([kn * nk * gk[0:1], jnp.where(keep, kr * nk, 0.0)], axis=-1)
            kt_ref[0, hd, 0] = k_full.T.astype(BF16)
            v_ref[0, hd, :, 0:MLA_V] = kva[:, 1024 + hd * 128:1024 + (hd + 1) * 128].astype(BF16)
            v_ref[0, hd, :, MLA_V:2 * MLA_V] = jnp.ones((TM, MLA_V), BF16)


def _attn_kernel(q_ref, kt_ref, v_ref, o_ref, s_scr, m_scr, acc_scr, *, tps):
    tq = q_ref.shape[2]
    nsub = tq // ATT_SUB
    kw = kt_ref.shape[4]
    n_steps = kt_ref.shape[2] // tps
    n_units = (nsub // ATT_GROUP) * n_steps

    def split(u):
        g = u // n_steps
        return g, u - g * n_steps

    def qk(u, buf):
        g, j = split(u)
        for k in range(ATT_GROUP):
            q = q_ref[0, 0, pl.ds(pl.multiple_of((g * ATT_GROUP + k) * ATT_SUB, ATT_SUB), ATT_SUB), :]
            for c in range(tps):
                s_scr[buf, k, :, c * kw:(c + 1) * kw] = _dot(q, kt_ref[0, 0, j * tps + c])

    def smpv(u, buf):
        g, j = split(u)
        vj = v_ref[0, 0, pl.ds(pl.multiple_of(j * (tps * kw), tps * kw), tps * kw), :]
        for k in range(ATT_GROUP):
            s = g * ATT_GROUP + k
            sv = s_scr[buf, k]
            m = m_scr[s]
            m_new = jnp.maximum(m, jnp.max(sv, axis=-1, keepdims=True))
            alpha = jnp.exp2(m - m_new)
            p = jnp.exp2(sv - m_new).astype(BF16)
            acc_scr[s] = alpha * acc_scr[s] + _dot(p, vj)
            m_scr[s] = m_new

    m_scr[...] = jnp.full(m_scr.shape, -jnp.inf, F32)
    acc_scr[...] = jnp.zeros(acc_scr.shape, F32)
    qk(0, 0)

    def body(i, carry):
        for r in range(ATT_UNROLL):
            u = ATT_UNROLL * i + r
            qk(u + 1, (r + 1) % 2)
            smpv(u, r % 2)
        return carry

    n_loop = (n_units - 1) // ATT_UNROLL
    lax.fori_loop(0, n_loop, body, 0)
    for u in range(ATT_UNROLL * n_loop, n_units):
        if u + 1 < n_units:
            qk(u + 1, (u + 1) % 2)
        smpv(u, u % 2)
    for s in range(nsub):
        acc = acc_scr[s]
        o_ref[0, s * ATT_SUB:(s + 1) * ATT_SUB, :] = (acc[:, 0:MLA_V] / acc[:, MLA_V:2 * MLA_V]).astype(BF16)


def _attn_ctx_kernel(q_ref, kt_ref, v_ref, o_ref):
    s = _dot(q_ref[0, 0], kt_ref[0, 0, 0])
    p = jnp.exp2(s - jnp.max(s, axis=-1, keepdims=True))
    acc = _dot(p.astype(BF16), v_ref[0, 0])
    o_ref[0] = (acc[:, 0:MLA_V] / acc[:, MLA_V:2 * MLA_V]).astype(BF16)


def _odd_out_kernel(x_ref, m_ref, o_ref, oc_ref, wo_ref, n2_ref, rwt_ref, xo_ref, hrow_ref, aff_ref, *, n_lat_tiles):
    o = jnp.where(pl.program_id(1) == n_lat_tiles, oc_ref[0], o_ref[0])
    out = _dot(o, wo_ref[...])
    _epilogue(x_ref[0], out, m_ref[0], n2_ref[...], rwt_ref[...], xo_ref, hrow_ref, aff_ref)


def _ind(c):
    return jnp.where(c, 1.0, 0.0)


def _select_kernel(a_ref, idx_ref, gate_ref, sel_scr, l_scr, a_scr, *, segs):
    ne = N_EXPERTS
    ri = lax.broadcasted_iota(I32, (LANES, LANES), 0)
    ci = lax.broadcasted_iota(I32, (LANES, LANES), 1)
    upper = _ind(ri <= ci).astype(BF16)
    ones8 = jnp.ones((8, LANES), BF16)
    for k0, nt, cap, c0 in segs:
        a = a_ref[0, k0:k0 + nt]
        def count_ge(v, a=a):
            return jnp.sum(jnp.sum(_ind(a >= v[None]), axis=0), axis=-1, keepdims=True)

        def sbody(i, prefix, cap=cap):
            cand = prefix | jnp.left_shift(jnp.int32(1), 30 - i)
            return jnp.where(count_ge(lax.bitcast_convert_type(cand, F32)) >= cap, cand, prefix)

        tbits = lax.fori_loop(0, 31, sbody, jnp.zeros((ne, 1), I32))

        def rbody(i, lohi, cap=cap):
            lo, hi = lohi
            mid = lo + (hi - lo) * 0.5
            ok = count_ge(mid) >= cap
            return jnp.where(ok, mid, lo), jnp.where(ok, hi, mid)

        hbits = jnp.maximum(tbits + 1, F32_MIN_NORMAL_BITS)
        lo, hi = lax.fori_loop(0, SELECT_REFINE_STEPS, rbody,
                               (lax.bitcast_convert_type(tbits, F32), lax.bitcast_convert_type(hbits, F32)))
        gt = a >= hi[None]
        eqf = _ind(a >= lo[None]) - _ind(gt)
        need = cap - jnp.sum(jnp.sum(_ind(gt), axis=0), axis=-1, keepdims=True)
        leq = _dot(eqf.reshape(nt * ne, LANES).astype(BF16), upper).reshape(nt, ne, LANES)
        carry = jnp.zeros((ne, 1), F32)
        sel = []
        for k in range(nt):
            rank = carry + leq[k] - eqf[k]
            sel.append(jnp.where(gt[k], 1.0, jnp.where(rank < need, eqf[k], 0.0)))
            carry = carry + leq[k][:, LANES - 1:LANES]
        sel = jnp.stack(sel)
        pad = jnp.zeros((LANES - nt, ne, LANES), F32)
        sel_scr[0:nt] = sel
        sel_scr[nt:LANES] = pad
        l_scr[0:nt] = _dot(sel.reshape(nt * ne, LANES).astype(BF16), upper).reshape(nt, ne, LANES)
        l_scr[nt:LANES] = pad
        a_hi = a.astype(BF16).astype(F32)
        a_mid = (a - a_hi).astype(BF16).astype(F32)
        a_scr[0, 0:nt] = a_hi
        a_scr[1, 0:nt] = a_mid
        a_scr[2, 0:nt] = a - a_hi - a_mid
        for part in range(3):
            a_scr[part, nt:LANES] = pad

        rows = -(-cap // LANES) * LANES
        slot = lax.broadcasted_iota(I32, (rows, LANES), 0).astype(F32)
        lane = lax.broadcasted_iota(I32, (rows, LANES), 1)
        lane_f = lane.astype(F32)
        idx_all = jnp.zeros((rows, LANES), F32)
        gate_all = jnp.zeros((rows, LANES), F32)
        for e in range(ne):
            sel_e = sel_scr[:, e, :].astype(BF16)
            tot = lax.dot_general(ones8, sel_e, _NT, preferred_element_type=F32)
            pb = _dot(tot.astype(BF16), upper)[0:1]
            le = pb <= slot
            kc = jnp.sum(_ind(le), axis=-1, keepdims=True)
            base = jnp.sum(jnp.where(le, tot[0:1], 0.0), axis=-1, keepdims=True)
            onehot = _ind(lane_f == kc).astype(BF16)
            rhs = jnp.concatenate([l_scr[:, e, :], a_scr[0, :, e, :], a_scr[1, :, e, :], a_scr[2, :, e, :]],
                                  axis=-1).astype(BF16)
            g = _dot(onehot, rhs)
            off = jnp.sum(_ind(g[:, 0:LANES] <= slot - base), axis=-1, keepdims=True)
            aff = (g[:, LANES:2 * LANES] + g[:, 2 * LANES:3 * LANES]) + g[:, 3 * LANES:4 * LANES]
            gate = jnp.sum(jnp.where(lane_f == off, aff, 0.0), axis=-1, keepdims=True)
            idx_all = jnp.where(lane == e, (kc + k0) * LANES + off, idx_all)
            gate_all = jnp.where(lane == e, gate, gate_all)
        idx_ref[0, :, c0:c0 + cap] = idx_all.T[0:ne, 0:cap].astype(I32)
        gate_ref[0, :, c0:c0 + cap] = gate_all.T[0:ne, 0:cap]


def _gather_kernel(idx_ref, h_ref, o_ref, rows):
    c_tot = rows.shape[0]

    def body(i, carry):
        base = pl.multiple_of(i * ROW_GROUP, ROW_GROUP)
        for k in range(ROW_GROUP):
            rows[base + k] = h_ref[0, idx_ref[0, 0, base + k]]
        return carry

    lax.fori_loop(0, c_tot // ROW_GROUP, body, 0)
    o_ref[0, 0] = rows[...].reshape(c_tot, D_MODEL).astype(BF16)


def _ffn_kernel(x_ref, wg_ref, wu_ref, wd_ref, o_ref):
    f = pl.program_id(2)
    x = x_ref[0, 0]
    a = _dot(x, wg_ref[0, 0].astype(BF16))
    u = _dot(x, wu_ref[0, 0].astype(BF16))
    y = _dot((_silu(a) * u).astype(BF16), wd_ref[0, 0].astype(BF16))

    @pl.when(f == 0)
    def _():
        o_ref[0, 0] = y

    @pl.when(f > 0)
    def _():
        o_ref[0, 0] += y


def _combine_kernel(idx_ref, gate_ref, x_hbm, y_ref, m_ref, o_hbm, acc, sem, *, c_lat):
    d = D_MODEL
    b = pl.program_id(0)
    e = pl.program_id(1)
    c_tot = y_ref.shape[2]

    @pl.when(e == 0)
    def _():
        cp = pltpu.make_async_copy(x_hbm.at[b], acc, sem)
        cp.start()
        cp.wait()

    for lo_, hi_, mrow in ((0, c_lat, 0), (c_lat, c_tot, 1)):
        g2 = m_ref[mrow][:, 5 * d:6 * d]

        def body(i, carry, g2=g2, lo_=lo_):
            base = pl.multiple_of(lo_ + i * ROW_GROUP, ROW_GROUP)
            ts = [idx_ref[0, 0, base + k] for k in range(ROW_GROUP)]
            new = [acc[pl.ds(ts[k], 1), :] + y_ref[0, 0, pl.ds(base + k, 1), :] * (gate_ref[0, 0, base + k] * g2)
                   for k in range(ROW_GROUP)]
            for k in range(ROW_GROUP):
                acc[pl.ds(ts[k], 1), :] = new[k]
            return carry

        lax.fori_loop(0, (hi_ - lo_) // ROW_GROUP, body, 0)

    @pl.when(e == pl.num_programs(1) - 1)
    def _():
        cp = pltpu.make_async_copy(acc.at[pl.ds(0, o_hbm.shape[1])], o_hbm.at[b], sem)
        cp.start()
        cp.wait()


def _moe(xs, hrow, aff_t, modtab, l, w_gate, w_up, w_down, n_lat, latent_only):
    bsz, t_all, d = xs.shape
    n_ctx = t_all - n_lat
    ne = N_EXPERTS
    c_lat = max(1, (EC_CAPACITY_FACTOR * n_lat) // ne)
    c_ctx = max(1, (EC_CAPACITY_FACTOR * n_ctx) // ne)
    c_tot = c_lat + c_ctx
    assert c_lat % ROW_GROUP == 0 and c_ctx % ROW_GROUP == 0
    assert n_lat % LANES == 0 and n_ctx % LANES == 0 and t_all // LANES <= LANES
    segs = ((0, n_lat // LANES, c_lat, 0), (n_lat // LANES, n_ctx // LANES, c_ctx, c_lat))
    idx, gates = pl.pallas_call(
        functools.partial(_select_kernel, segs=segs),
        grid=(bsz,),
        in_specs=[pl.BlockSpec((1, t_all // LANES, ne, LANES), lambda b: (b, 0, 0, 0))],
        out_specs=[pl.BlockSpec((1, ne, c_tot), lambda b: (b, 0, 0)), pl.BlockSpec((1, ne, c_tot), lambda b: (b, 0, 0))],
        out_shape=[jax.ShapeDtypeStruct((bsz, ne, c_tot), I32), jax.ShapeDtypeStruct((bsz, ne, c_tot), F32)],
        scratch_shapes=[pltpu.VMEM((LANES, ne, LANES), F32), pltpu.VMEM((LANES, ne, LANES), F32),
                        pltpu.VMEM((3, LANES, ne, LANES), F32)],
        compiler_params=_cparams(("arbitrary",), VMEM_LIMIT),
        name="moe_select",
    )(aff_t)
    idx = idx.reshape(bsz * ne, 1, c_tot)
    gates = gates.reshape(bsz * ne, 1, c_tot)
    smem = functools.partial(pl.BlockSpec, memory_space=pltpu.SMEM)

    xg = pl.pallas_call(
        _gather_kernel,
        grid=(bsz, ne),
        in_specs=[smem((1, 1, c_tot), lambda b, e: (b * ne + e, 0, 0)),
                  pl.BlockSpec((1, t_all, SUBLANES, LANES), lambda b, e: (b, 0, 0, 0), pipeline_mode=pl.Buffered(1))],
        out_specs=pl.BlockSpec((1, 1, c_tot, d), lambda b, e: (b, e, 0, 0)),
        out_shape=jax.ShapeDtypeStruct((bsz, ne, c_tot, d), BF16),
        scratch_shapes=[pltpu.VMEM((c_tot, SUBLANES, LANES), F32)],
        compiler_params=_cparams(("arbitrary", "arbitrary"), VMEM_LIMIT),
        name="moe_gather",
    )(idx, hrow)

    nf = EXPERT_FF // FF_TILE
    y = pl.pallas_call(
        _ffn_kernel,
        grid=(bsz, ne, nf),
        in_specs=[pl.BlockSpec((1, 1, c_tot, d), lambda b, e, f: (b, e, 0, 0)),
                  pl.BlockSpec((1, 1, d, FF_TILE), lambda b, e, f: (l, e, 0, f)),
                  pl.BlockSpec((1, 1, d, FF_TILE), lambda b, e, f: (l, e, 0, f)),
                  pl.BlockSpec((1, 1, FF_TILE, d), lambda b, e, f: (l, e, f, 0))],
        out_specs=pl.BlockSpec((1, 1, c_tot, d), lambda b, e, f: (b, e, 0, 0)),
        out_shape=jax.ShapeDtypeStruct((bsz, ne, c_tot, d), F32),
        compiler_params=_cparams(("arbitrary", "arbitrary", "arbitrary"), VMEM_LIMIT),
        name="moe_ffn",
    )(xg, w_gate, w_up, w_down)

    return pl.pallas_call(
        functools.partial(_combine_kernel, c_lat=c_lat),
        grid=(bsz, ne),
        in_specs=[smem((1, 1, c_tot), lambda b, e: (b * ne + e, 0, 0)),
                  smem((1, 1, c_tot), lambda b, e: (b * ne + e, 0, 0)),
                  pl.BlockSpec(memory_space=pl.ANY),
                  pl.BlockSpec((1, 1, c_tot, d), lambda b, e: (b, e, 0, 0)),
                  pl.BlockSpec((2, 1, 6 * d), lambda b, e: (b, 0, 0))],
        out_specs=pl.BlockSpec(memory_space=pl.ANY),
        out_shape=jax.ShapeDtypeStruct((bsz, n_lat if latent_only else t_all, d), F32),
        scratch_shapes=[pltpu.VMEM((t_all, d), F32), pltpu.SemaphoreType.DMA(())],
        input_output_aliases={} if latent_only else {2: 0},
        compiler_params=_cparams(("arbitrary", "arbitrary"), VMEM_LIMIT),
        name="moe_combine",
    )(idx, gates, xs, y, modtab)


def _tile_specs(d, n_lat_tiles):
    x_spec = pl.BlockSpec((1, TM, d), lambda b, i: (b, i, 0))
    m_spec = pl.BlockSpec((1, 1, 6 * d), lambda b, i: (2 * b + i // n_lat_tiles, 0, 0))
    return x_spec, m_spec


def _const_spec(shape):
    return pl.BlockSpec(shape, lambda b, i: (0,) * len(shape))


def _even_layer(xs, modtab, n1g, n2g, rwt, w_in_ext, lg, gn_g, conv_w, conv_b, ln_g, ln_b, w_out, cos, sin, n_lat):
    bsz, t_all, d = xs.shape
    nt = t_all // TM
    nlt = n_lat // TM
    x_spec, m_spec = _tile_specs(d, nlt)
    n_cols = 7 * 512
    proj = pl.pallas_call(
        _even_in_kernel,
        grid=(bsz, nt),
        in_specs=[x_spec, m_spec, _const_spec((1, d)), _const_spec(w_in_ext.shape),
                  pl.BlockSpec((TM, 512), lambda b, i: (i, 0)), pl.BlockSpec((TM, 512), lambda b, i: (i, 0))],
        out_specs=pl.BlockSpec((1, TM, n_cols), lambda b, i: (b, i, 0)),
        out_shape=jax.ShapeDtypeStruct((bsz, t_all, n_cols), BF16),
        compiler_params=_cparams(("arbitrary", "arbitrary"), VMEM_LIMIT),
        name="even_in",
    )(xs, modtab, n1g, w_in_ext, cos, sin)

    fwd = lambda s: jnp.where(s == 0, nlt, s - 1)
    bwd = lambda s: jnp.where(s == 0, nlt, nlt - s)

    def col(order, j):
        return pl.BlockSpec((1, TM, 512), lambda b, s: (b, order(s), j))

    o_f, o_b = pl.pallas_call(
        _ret_kernel,
        grid=(bsz, nt),
        in_specs=[pl.BlockSpec(lg.shape, lambda b, s: (0, 0, 0, 0)),
                  col(fwd, 0), col(fwd, 1), col(fwd, 2), col(bwd, 0), col(bwd, 1), col(bwd, 2)],
        out_specs=[col(fwd, 0), col(bwd, 0)],
        out_shape=[jax.ShapeDtypeStruct((bsz, t_all, 512), BF16)] * 2,
        scratch_shapes=[pltpu.VMEM((2, RET_HEADS, RET_DK, 128), F32),
                        pltpu.VMEM((2, RET_HEADS, RET_CHUNK, RET_CHUNK), F32),
                        pltpu.VMEM((2, RET_HEADS, RET_CHUNK, 128), F32),
                        pltpu.VMEM((2, RET_HEADS, RET_CHUNK, 128), F32),
                        pltpu.VMEM((2, RET_HEADS, 1, 128), F32)],
        compiler_params=_cparams(("arbitrary", "arbitrary"), VMEM_LIMIT),
        name="retention",
    )(lg, proj, proj, proj, proj, proj, proj)

    hpt = TM // CONV_HALO
    n_halo = t_all // CONV_HALO

    def tcol(j):
        return pl.BlockSpec((1, TM, 512), lambda b, i: (b, i, j))

    def halo(j, nxt):
        if nxt:
            return pl.BlockSpec((1, CONV_HALO, 512), lambda b, i: (b, jnp.minimum((i + 1) * hpt, n_halo - 1), j))
        return pl.BlockSpec((1, CONV_HALO, 512), lambda b, i: (b, jnp.maximum(i * hpt - 1, 0), j))

    xs_new, hrow, aff_t = pl.pallas_call(
        functools.partial(_even_out_kernel, n_lat_tiles=nlt),
        grid=(bsz, nt),
        in_specs=[x_spec, m_spec, pl.BlockSpec((1, TM, 512), lambda b, i: (b, i, 0)),
                  pl.BlockSpec((1, TM, 512), lambda b, i: (b, i, 0)),
                  tcol(3), tcol(4), tcol(5), tcol(6), halo(5, False), halo(6, False), halo(5, True), halo(6, True),
                  _const_spec((1, 512)), _const_spec(conv_w.shape), _const_spec((1, 512)), _const_spec((1, 512)),
                  _const_spec((1, 512)), _const_spec(w_out.shape), _const_spec((1, d)), _const_spec(rwt.shape)],
        out_specs=[x_spec, pl.BlockSpec((1, TM, SUBLANES, LANES), lambda b, i: (b, i, 0, 0)),
                   pl.BlockSpec((1, TM // LANES, N_EXPERTS, LANES), lambda b, i: (b, i, 0, 0))],
        out_shape=[jax.ShapeDtypeStruct(xs.shape, F32), jax.ShapeDtypeStruct((bsz, t_all, SUBLANES, LANES), F32),
                   jax.ShapeDtypeStruct((bsz, t_all // LANES, N_EXPERTS, LANES), F32)],
        scratch_shapes=[pltpu.VMEM((TM + 2 * CONV_HALO, CONV_CH), F32),
                        pltpu.VMEM((SUBLANES - 1, TM + 2 * CONV_HALO - SUBLANES, CONV_CH), F32)],
        input_output_aliases={0: 0},
        compiler_params=_cparams(("arbitrary", "arbitrary"), VMEM_LIMIT),
        name="even_out",
    )(xs, modtab, o_f, o_b, proj, proj, proj, proj, proj, proj, proj, proj,
      gn_g, conv_w, conv_b, ln_g, ln_b, w_out, n2g, rwt)
    return xs_new, hrow, aff_t


def _odd_layer(xs, modtab, n1g, n2g, rwt, w_in_ext, qag, kvag, w_uq_ext, w_ukv_p, gq, gk, w_out, cos, sin, n_lat):
    bsz, t_all, d = xs.shape
    nt = t_all // TM
    nlt = n_lat // TM
    nh = MLA_HEADS
    x_spec, m_spec = _tile_specs(d, nlt)
    q, kt, v = pl.pallas_call(
        _odd_in_kernel,
        grid=(bsz, nt),
        in_specs=[x_spec, m_spec, _const_spec((1, d)), _const_spec(w_in_ext.shape), _const_spec(qag.shape),
                  _const_spec(kvag.shape), _const_spec(w_uq_ext.shape), _const_spec(w_ukv_p.shape),
                  _const_spec(gq.shape), _const_spec(gk.shape),
                  pl.BlockSpec((TM, 128), lambda b, i: (i, 0)), pl.BlockSpec((TM, 128), lambda b, i: (i, 0))],
        out_specs=[pl.BlockSpec((1, nh, TM, MLA_QK_PAD), lambda b, i: (b, 0, i, 0)),
                   pl.BlockSpec((1, nh, 1, MLA_QK_PAD, TM), lambda b, i: (b, 0, i, 0, 0)),
                   pl.BlockSpec((1, nh, TM, 2 * MLA_V), lambda b, i: (b, 0, i, 0))],
        out_shape=[jax.ShapeDtypeStruct((bsz, nh, t_all, MLA_QK_PAD), BF16),
                   jax.ShapeDtypeStruct((bsz, nh, nt, MLA_QK_PAD, TM), BF16),
                   jax.ShapeDtypeStruct((bsz, nh, t_all, 2 * MLA_V), BF16)],
        compiler_params=_cparams(("arbitrary", "arbitrary"), VMEM_LIMIT),
        name="odd_in",
    )(xs, modtab, n1g, w_in_ext, qag, kvag, w_uq_ext, w_ukv_p, gq, gk, cos, sin)

    tps = next(c for c in ATT_KT_PER_STEP if nt % c == 0)
    nsub = ATT_TQ // ATT_SUB
    o = pl.pallas_call(
        functools.partial(_attn_kernel, tps=tps),
        grid=(bsz, nh, n_lat // ATT_TQ),
        in_specs=[pl.BlockSpec((1, 1, ATT_TQ, MLA_QK_PAD), lambda b, h, i: (b, h, i, 0)),
                  pl.BlockSpec((1, 1, nt, MLA_QK_PAD, TM), lambda b, h, i: (b, h, 0, 0, 0)),
                  pl.BlockSpec((1, 1, t_all, 2 * MLA_V), lambda b, h, i: (b, h, 0, 0))],
        out_specs=pl.BlockSpec((1, ATT_TQ, MLA_V), lambda b, h, i: (b, i, h)),
        out_shape=jax.ShapeDtypeStruct((bsz, n_lat, nh * MLA_V), BF16),
        scratch_shapes=[pltpu.VMEM((2, ATT_GROUP, ATT_SUB, tps * TM), F32), pltpu.VMEM((nsub, ATT_SUB, 1), F32),
                        pltpu.VMEM((nsub, ATT_SUB, 2 * MLA_V), F32)],
        compiler_params=_cparams(("arbitrary", "arbitrary", "arbitrary"), VMEM_LIMIT),
        name="mla_attention",
    )(q, kt, v)
    o_ctx = pl.pallas_call(
        _attn_ctx_kernel,
        grid=(bsz, nh),
        in_specs=[pl.BlockSpec((1, 1, TM, MLA_QK_PAD), lambda b, h: (b, h, nlt, 0)),
                  pl.BlockSpec((1, 1, 1, MLA_QK_PAD, TM), lambda b, h: (b, h, nlt, 0, 0)),
                  pl.BlockSpec((1, 1, TM, 2 * MLA_V), lambda b, h: (b, h, nlt, 0))],
        out_specs=pl.BlockSpec((1, TM, MLA_V), lambda b, h: (b, 0, h)),
        out_shape=jax.ShapeDtypeStruct((bsz, TM, nh * MLA_V), BF16),
        compiler_params=_cparams(("arbitrary", "arbitrary"), VMEM_LIMIT),
        name="mla_attention_ctx",
    )(q, kt, v)

    xs_new, hrow, aff_t = pl.pallas_call(
        functools.partial(_odd_out_kernel, n_lat_tiles=nlt),
        grid=(bsz, nt),
        in_specs=[x_spec, m_spec, pl.BlockSpec((1, TM, nh * MLA_V), lambda b, i: (b, jnp.minimum(i, nlt - 1), 0)),
                  pl.BlockSpec((1, TM, nh * MLA_V), lambda b, i: (b, 0, 0)),
                  _const_spec(w_out.shape), _const_spec((1, d)), _const_spec(rwt.shape)],
        out_specs=[x_spec, pl.BlockSpec((1, TM, SUBLANES, LANES), lambda b, i: (b, i, 0, 0)),
                   pl.BlockSpec((1, TM // LANES, N_EXPERTS, LANES), lambda b, i: (b, i, 0, 0))],
        out_shape=[jax.ShapeDtypeStruct(xs.shape, F32), jax.ShapeDtypeStruct((bsz, t_all, SUBLANES, LANES), F32),
                   jax.ShapeDtypeStruct((bsz, t_all // LANES, N_EXPERTS, LANES), F32)],
        input_output_aliases={0: 0},
        compiler_params=_cparams(("arbitrary", "arbitrary"), VMEM_LIMIT),
        name="odd_out",
    )(xs, modtab, o, o_ctx, w_out, n2g, rwt)
    return xs_new, hrow, aff_t


def _rope_tables(n_lat, n_ctx, nf):
    t = jnp.arange(n_lat)
    row = (t // GRID_W).astype(F32)
    col = (t % GRID_W).astype(F32)
    inv = ROPE_BASE ** (-(jnp.arange(nf, dtype=F32) / nf))
    ar = row[:, None] * inv[None, :]
    ac = col[:, None] * inv[None, :]
    cos = jnp.concatenate([jnp.cos(ar), jnp.cos(ar), jnp.cos(ac), jnp.cos(ac)], axis=-1)
    sin = jnp.concatenate([-jnp.sin(ar), jnp.sin(ar), -jnp.sin(ac), jnp.sin(ac)], axis=-1)
    cos = jnp.concatenate([cos, jnp.ones((n_ctx, 4 * nf), F32)], axis=0)
    sin = jnp.concatenate([sin, jnp.zeros((n_ctx, 4 * nf), F32)], axis=0)
    return cos, sin


def _swap_perm(nf):
    a = jnp.arange(nf)
    return jnp.concatenate([a + nf, a, a + 3 * nf, a + 2 * nf])


def kernel(x, c, ctx, c_ctx, mod_w, mod_b, norm1_g, norm2_g, ev_w_in, ret_decay_f, ret_decay_b, ret_gn_g, conv_w, conv_b, conv_ln_g, conv_ln_b, ev_w_out, od_w_in, mla_q_a_g, mla_kv_a_g, mla_w_uq, mla_w_ukv, mla_qk_g_q, mla_qk_g_k, od_w_out, router_w, moe_w_gate, moe_w_up, moe_w_down):
    bsz, n_lat, d = x.shape
    n_ctx = ctx.shape[1]
    depth = mod_w.shape[0]
    assert d == D_MODEL and n_ctx == TM and n_lat % ATT_TQ == 0 and bsz + 1 <= 8

    xs = jnp.concatenate([x, ctx], axis=1)
    cc = jnp.zeros((8, d), F32).at[:bsz].set(c).at[bsz].set(c_ctx)
    mod = _mod_all(cc, mod_w, mod_b)

    cos_e, sin_e = _rope_tables(n_lat, n_ctx, RET_DK // 4)
    cos_e, sin_e = jnp.tile(cos_e, (1, RET_HEADS)), jnp.tile(sin_e, (1, RET_HEADS))
    cos_o, sin_o = _rope_tables(n_lat, n_ctx, MLA_ROPE // 4)
    cos_o, sin_o = jnp.tile(cos_o, (1, 2)), jnp.tile(sin_o, (1, 2))
    perm_e = (jnp.arange(RET_HEADS)[:, None] * RET_DK + _swap_perm(RET_DK // 4)[None, :]).reshape(-1)
    perm_o = _swap_perm(MLA_ROPE // 4)

    for l in range(depth):
        i = l // 2
        modtab = jnp.stack([mod[l, :bsz], jnp.broadcast_to(mod[l, bsz], (bsz, 6 * d))], axis=1).reshape(bsz * 2, 1, 6 * d)
        n1g = norm1_g[l][None]
        n2g = norm2_g[l][None]
        rw = router_w[l]
        rw_hi = rw.astype(BF16)
        rw_lo = (rw - rw_hi.astype(F32)).astype(BF16)
        pad = jnp.zeros((d, 128 - N_EXPERTS), BF16)
        rwt = jnp.concatenate([rw_hi, pad, rw_lo, pad], axis=1)
        if l % 2 == 0:
            w = ev_w_in[i]
            w_ext = jnp.concatenate([w, w[:, 0:512][:, perm_e], w[:, 512:1024][:, perm_e]], axis=1).astype(BF16)
            lg = jnp.broadcast_to(jnp.stack([ret_decay_f[i], ret_decay_b[i]])[:, :, None, None], (2, RET_HEADS, 1, 128))
            cw = jnp.concatenate([conv_w[i], jnp.zeros((1, CONV_CH), F32)], axis=0)
            xs, hrow, aff_t = _even_layer(xs, modtab, n1g, n2g, rwt, w_ext, lg, ret_gn_g[i][None], cw, conv_b[i][None],
                                    conv_ln_g[i][None], conv_ln_b[i][None], ev_w_out[i].astype(BF16), cos_e, sin_e, n_lat)
        else:
            w = od_w_in[i]
            kpe = w[:, MLA_Q_LORA + MLA_KV_LORA:]
            w_ext = jnp.concatenate([w[:, :MLA_Q_LORA + MLA_KV_LORA], kpe, kpe, kpe[:, perm_o], kpe[:, perm_o]], axis=1).astype(BF16)
            wq = mla_w_uq[i].reshape(MLA_Q_LORA, MLA_HEADS, MLA_QK)
            wq_r = wq[:, :, MLA_NOPE:]
            w_uq_ext = jnp.concatenate([wq[:, :, :MLA_NOPE].reshape(MLA_Q_LORA, -1), wq_r.reshape(MLA_Q_LORA, -1),
                                        wq_r[:, :, perm_o].reshape(MLA_Q_LORA, -1)], axis=1).astype(BF16)
            wkv = mla_w_ukv[i].reshape(MLA_KV_LORA, MLA_HEADS, MLA_NOPE + MLA_V)
            w_ukv_p = jnp.concatenate([wkv[:, :, :MLA_NOPE].reshape(MLA_KV_LORA, -1),
                                       wkv[:, :, MLA_NOPE:].reshape(MLA_KV_LORA, -1)], axis=1).astype(BF16)

            def gains(g):
                gr = g[MLA_NOPE:]
                return jnp.stack([g[:MLA_NOPE], jnp.tile(gr, 2), jnp.tile(gr[perm_o], 2)])

            xs, hrow, aff_t = _odd_layer(xs, modtab, n1g, n2g, rwt, w_ext, mla_q_a_g[i][None], mla_kv_a_g[i][None], w_uq_ext,
                                   w_ukv_p, gains(mla_qk_g_q[i]), gains(mla_qk_g_k[i]), od_w_out[i].astype(BF16),
                                   cos_o, sin_o, n_lat)
        xs = _moe(xs, hrow, aff_t, modtab, l, moe_w_gate, moe_w_up, moe_w_down, n_lat, latent_only=l == depth - 1)
    return xs
```

```python
import functools

import jax
import jax.numpy as jnp
from jax import lax
from jax.experimental import pallas as pl
from jax.experimental.pallas import tpu as pltpu

F32 = jnp.float32
BF16 = jnp.bfloat16
I32 = jnp.int32

D_MODEL = 1024
GRID_W = 64
RET_HEADS = 4
RET_DK = 128
RET_CHUNK = 256
CONV_CH = 512
CONV_WIDTH = 31
CONV_HALO = 16
MLA_HEADS = 8
MLA_Q_LORA = 512
MLA_KV_LORA = 256
MLA_NOPE = 128
MLA_ROPE = 64
MLA_V = 128
MLA_QK = MLA_NOPE + MLA_ROPE
MLA_QK_PAD = 256
N_EXPERTS = 16
EC_CAPACITY_FACTOR = 2
EXPERT_FF = 1024
ROPE_BASE = 10000.0
LOG2E = 1.4426950408889634
EPS = 1e-6

LANES = 128
SUBLANES = 8
TM = 256
ATT_TQ = 2048
ATT_SUB = 512
ATT_GROUP = 1
ATT_UNROLL = 4
ATT_KT_PER_STEP = (3, 2, 1)
FF_TILE = 1024
SELECT_REFINE_STEPS = 29
F32_MIN_NORMAL_BITS = 0x00800000
ROW_GROUP = 8
VMEM_LIMIT = 56 * 2 ** 20

_NT = (((1,), (1,)), ((), ()))


def _cparams(sem, vmem=None):
    return pltpu.CompilerParams(dimension_semantics=sem, vmem_limit_bytes=vmem)


def _dot(a, b):
    return jnp.dot(a, b, preferred_element_type=F32)


def _split3(a):
    a1 = a.astype(BF16)
    r = a - a1.astype(F32)
    a2 = r.astype(BF16)
    a3 = (r - a2.astype(F32)).astype(BF16)
    return a1, a2, a3


def _dot_hi(a, b, dn):
    a1, a2, a3 = _split3(a)
    b1, b2, b3 = _split3(b)
    d = lambda x, y: lax.dot_general(x, y, dn, preferred_element_type=F32)
    return ((d(a3, b1) + d(a2, b2) + d(a1, b3)) + (d(a2, b1) + d(a1, b2))) + d(a1, b1)


def _silu(a):
    return a * jax.nn.sigmoid(a)


def _rms(x, g):
    return x * lax.rsqrt(jnp.mean(x * x, axis=-1, keepdims=True) + EPS) * g


def _modulate(x, g, shift, scale):
    return _rms(x, g) * (1.0 + scale) + shift


def _mod_kernel(c_ref, w_ref, b_ref, o_ref):
    s = _silu(c_ref[...])
    o_ref[0] = _dot_hi(s, w_ref[0], (((1,), (0,)), ((), ()))) + b_ref[0]


def _mod_all(cc, mod_w, mod_b):
    depth, d, n6 = mod_w.shape
    tn = 1536
    return pl.pallas_call(
        _mod_kernel,
        grid=(depth, n6 // tn),
        in_specs=[pl.BlockSpec((8, d), lambda l, j: (0, 0)),
                  pl.BlockSpec((1, d, tn), lambda l, j: (l, 0, j)),
                  pl.BlockSpec((1, 1, tn), lambda l, j: (l, 0, j))],
        out_specs=pl.BlockSpec((1, 8, tn), lambda l, j: (l, 0, j)),
        out_shape=jax.ShapeDtypeStruct((depth, 8, n6), F32),
        compiler_params=_cparams(("arbitrary", "arbitrary"), VMEM_LIMIT),
        name="mod_all",
    )(cc, mod_w, mod_b.reshape(depth, 1, n6))


def _epilogue(x, out, m, n2g, rwt, xo_ref, hrow_ref, aff_ref):
    d = D_MODEL
    xn = x + m[:, 2 * d:3 * d] * out
    xo_ref[0] = xn
    h2 = _modulate(xn, n2g, m[:, 3 * d:4 * d], m[:, 4 * d:5 * d])
    hrow_ref[0] = h2.reshape(TM, SUBLANES, LANES)
    h_hi = h2.astype(BF16)
    h_lo = (h2 - h_hi.astype(F32)).astype(BF16)
    z = _dot(h_hi, rwt)
    logits = (z[:, 0:128] + z[:, 128:256]) + _dot(h_lo, rwt[:, 0:128])
    lane = lax.broadcasted_iota(I32, logits.shape, 1)
    logits = jnp.where(lane < N_EXPERTS, logits, -jnp.inf)
    e = jnp.exp(logits - jnp.max(logits, axis=-1, keepdims=True))
    aff = e / jnp.sum(e, axis=-1, keepdims=True)
    aff_t = aff.T
    for k in range(TM // LANES):
        aff_ref[0, k] = aff_t[0:N_EXPERTS, k * LANES:(k + 1) * LANES]


def _even_in_kernel(x_ref, m_ref, g_ref, w_ref, cos_ref, sin_ref, o_ref):
    d = D_MODEL
    m = m_ref[0]
    h = _modulate(x_ref[0], g_ref[...], m[:, 0:d], m[:, d:2 * d]).astype(BF16)

    def grp(j):
        return _dot(h, w_ref[:, j * 512:(j + 1) * 512])

    c = cos_ref[...]
    s = sin_ref[...]
    o_ref[0, :, 0:512] = (grp(0) * c + grp(7) * s).astype(BF16)
    o_ref[0, :, 512:1024] = ((grp(1) * c + grp(8) * s) * (RET_DK ** -0.5)).astype(BF16)
    for j in range(2, 7):
        o_ref[0, :, j * 512:(j + 1) * 512] = grp(j).astype(BF16)


def _ret_kernel(lg_ref, qf_ref, kf_ref, vf_ref, qb_ref, kb_ref, vb_ref, of_ref, ob_ref,
                st_ref, intra_ref, xi_ref, zeta_ref, dc_ref):
    L = RET_CHUNK
    nc = TM // L

    @pl.when(pl.program_id(1) == 0)
    def _():
        st_ref[...] = jnp.zeros(st_ref.shape, F32)
        ii = lax.broadcasted_iota(I32, (L, L), 0).astype(F32)
        jj = lax.broadcasted_iota(I32, (L, L), 1).astype(F32)
        pos = lax.broadcasted_iota(I32, (L, LANES), 0).astype(F32)
        for dr in range(2):
            for h in range(RET_HEADS):
                x = lg_ref[dr, h]
                lg = jnp.minimum(x, 0.0) - jnp.log(1.0 + jnp.exp(-jnp.abs(x)))
                if dr == 0:
                    diff, xpos, zpos = ii - jj, pos + 1.0, (L - 1.0) - pos
                else:
                    diff, xpos, zpos = jj - ii, L - pos, pos
                intra_ref[dr, h] = jnp.where(diff >= 0, jnp.exp(lg[:, 0:1] * jnp.maximum(diff, 0.0)), 0.0)
                xi_ref[dr, h] = jnp.exp(lg * xpos)
                zeta_ref[dr, h] = jnp.exp(lg * zpos)
                dc_ref[dr, h] = jnp.exp(lg * float(L))

    for dr, (q_ref, k_ref, v_ref, o_ref) in enumerate(((qf_ref, kf_ref, vf_ref, of_ref),
                                                      (qb_ref, kb_ref, vb_ref, ob_ref))):
        order = range(nc) if dr == 0 else range(nc - 1, -1, -1)
        for c in order:
            for h in range(RET_HEADS):
                rs = slice(c * L, (c + 1) * L)
                cs = slice(h * 128, (h + 1) * 128)
                qc = q_ref[0, rs, cs]
                kc = k_ref[0, rs, cs]
                vc = v_ref[0, rs, cs]
                st = st_ref[dr, h]
                sc = lax.dot_general(qc, kc, _NT, preferred_element_type=F32) * intra_ref[dr, h]
                o = _dot(sc.astype(BF16), vc) + _dot(qc, st.astype(BF16)) * xi_ref[dr, h]
                kz = (kc.astype(F32) * zeta_ref[dr, h]).T.astype(BF16)
                st_ref[dr, h] = st * dc_ref[dr, h] + _dot(kz, vc)
                o_ref[0, rs, cs] = o.astype(BF16)


def _even_out_kernel(x_ref, m_ref, of_ref, ob_ref, gf_ref, gb_ref, bv_ref, bg_ref,
                     pv_ref, pg_ref, nv_ref, ng_ref, gn_ref, cw_ref, cb_ref, lng_ref, lnb_ref,
                     wo_ref, n2_ref, rwt_ref, xo_ref, hrow_ref, aff_ref, u_scr, ush_scr, *, n_lat_tiles):
    i = pl.program_id(1)
    hl = CONV_HALO

    def glu(v_ref, g_ref):
        return v_ref[0].astype(F32) * jax.nn.sigmoid(g_ref[0].astype(F32))

    prev_ok = jnp.logical_and(i >= 1, i < n_lat_tiles)
    next_ok = i < n_lat_tiles - 1
    u_scr[0:hl, :] = jnp.where(prev_ok, glu(pv_ref, pg_ref), 0.0)
    u_scr[hl:hl + TM, :] = glu(bv_ref, bg_ref)
    u_scr[hl + TM:2 * hl + TM, :] = jnp.where(next_ok, glu(nv_ref, ng_ref), 0.0)
    n_sh = TM + 2 * hl - SUBLANES
    for r in range(1, SUBLANES):
        ush_scr[r - 1] = u_scr[r:r + n_sh, :]
    y = jnp.zeros((TM, CONV_CH), F32) + cb_ref[...]
    off = hl - CONV_WIDTH // 2
    for k in range(CONV_WIDTH):
        r = (k + off) % SUBLANES
        a = k + off - r
        win = u_scr[a:a + TM, :] if r == 0 else ush_scr[r - 1, a:a + TM, :]
        y = y + cw_ref[k:k + 1, :] * win
    mu = jnp.mean(y, axis=-1, keepdims=True)
    yc = y - mu
    var = jnp.mean(yc * yc, axis=-1, keepdims=True)
    conv = _silu(yc * lax.rsqrt(var + EPS) * lng_ref[...] + lnb_ref[...])

    def gnorm(o_ref, g_ref):
        parts = []
        for h in range(RET_HEADS):
            cs = slice(h * 128, (h + 1) * 128)
            o = o_ref[0, :, cs].astype(F32)
            mu_h = jnp.mean(o, axis=-1, keepdims=True)
            oc = o - mu_h
            var_h = jnp.mean(oc * oc, axis=-1, keepdims=True)
            parts.append(oc * lax.rsqrt(var_h + EPS) * gn_ref[:, cs] * _silu(g_ref[0, :, cs].astype(F32)))
        return parts

    pf = gnorm(of_ref, gf_ref)
    pb = gnorm(ob_ref, gb_ref)
    out = _dot(conv.astype(BF16), wo_ref[512:1024, :])
    for h in range(RET_HEADS):
        out = out + _dot((pf[h] + pb[h]).astype(BF16), wo_ref[h * 128:(h + 1) * 128, :])
    _epilogue(x_ref[0], out, m_ref[0], n2_ref[...], rwt_ref[...], xo_ref, hrow_ref, aff_ref)


def _odd_in_kernel(x_ref, m_ref, g_ref, win_ref, qag_ref, kvag_ref, wuq_ref, wukv_ref,
                   gq_ref, gk_ref, cos_ref, sin_ref, q_ref, kt_ref, v_ref):
    d = D_MODEL
    nh = MLA_HEADS
    m = m_ref[0]
    h = _modulate(x_ref[0], g_ref[...], m[:, 0:d], m[:, d:2 * d]).astype(BF16)
    c = _dot(h, win_ref[...])
    cq = _rms(c[:, 0:512], qag_ref[...]).astype(BF16)
    ckv = _rms(c[:, 512:768], kvag_ref[...]).astype(BF16)
    kpe = c[:, 768:896]
    kpe_sw = c[:, 896:1024]
    qa = _dot(cq, wuq_ref[...])
    kva = _dot(ckv, wukv_ref[...])
    cos = cos_ref[...]
    sin = sin_ref[...]
    gq = gq_ref[...]
    gk = gk_ref[...]
    lane = lax.broadcasted_iota(I32, (TM, 128), 1)
    lo = lane < MLA_ROPE
    inv_d = 1.0 / MLA_QK

    ss_kpe = 0.5 * jnp.sum(kpe * kpe, axis=-1, keepdims=True)
    kr = kpe * gk[1:2] * cos + kpe_sw * gk[2:3] * sin
    for p in range(nh // 2):
        r = qa[:, 1024 + p * 128:1024 + (p + 1) * 128]
        r_sw = qa[:, 1536 + p * 128:1536 + (p + 1) * 128]
        r2 = r * r
        ss_r = (jnp.sum(jnp.where(lo, r2, 0.0), axis=-1, keepdims=True),
                jnp.sum(jnp.where(lo, 0.0, r2), axis=-1, keepdims=True))
        qr = r * gq[1:2] * cos + r_sw * gq[2:3] * sin
        for s in range(2):
            hd = 2 * p + s
            keep = lo if s == 0 else jnp.logical_not(lo)
            qn = qa[:, hd * 128:(hd + 1) * 128]
            nq = lax.rsqrt((jnp.sum(qn * qn, axis=-1, keepdims=True) + ss_r[s]) * inv_d + EPS) * (MLA_QK ** -0.5 * LOG2E)
            q_ref[0, hd, :, 0:128] = (qn * nq * gq[0:1]).astype(BF16)
            q_ref[0, hd, :, 128:256] = jnp.where(keep, qr * nq, 0.0).astype(BF16)
            kn = kva[:, hd * 128:(hd + 1) * 128]
            nk = lax.rsqrt((jnp.sum(kn * kn, axis=-1, keepdims=True) + ss_kpe) * inv_d + EPS)
            k_full = jnp.concatenate([kn * nk * gk[0:1], jnp.where(keep, kr * nk, 0.0)], axis=-1)
            kt_ref[0, hd, 0] = k_full.T.astype(BF16)
            v_ref[0, hd, :, 0:MLA_V] = kva[:, 1024 + hd * 128:1024 + (hd + 1) * 128].astype(BF16)
            v_ref[0, hd, :, MLA_V:2 * MLA_V] = jnp.ones((TM, MLA_V), BF16)


def _attn_kernel(q_ref, kt_ref, v_ref, o_ref, s_scr, m_scr, acc_scr, *, tps):
    tq = q_ref.shape[2]
    nsub = tq // ATT_SUB
    kw = kt_ref.shape[4]
    n_steps = kt_ref.shape[2] // tps
    n_units = (nsub // ATT_GROUP) * n_steps

    def split(u):
        g = u // n_steps
        return g, u - g * n_steps

    def qk(u, buf):
        g, j = split(u)
        for k in range(ATT_GROUP):
            q = q_ref[0, 0, pl.ds(pl.multiple_of((g * ATT_GROUP + k) * ATT_SUB, ATT_SUB), ATT_SUB), :]
            for c in range(tps):
                s_scr[buf, k, :, c * kw:(c + 1) * kw] = _dot(q, kt_ref[0, 0, j * tps + c])

    def smpv(u, buf):
        g, j = split(u)
        vj = v_ref[0, 0, pl.ds(pl.multiple_of(j * (tps * kw), tps * kw), tps * kw), :]
        for k in range(ATT_GROUP):
            s = g * ATT_GROUP + k
            sv = s_scr[buf, k]
            m = m_scr[s]
            tiles = [sv[:, c * LANES:(c + 1) * LANES] for c in range(sv.shape[1] // LANES)]
            mx = functools.reduce(jnp.maximum, tiles)
            m_new = jnp.maximum(m, jnp.broadcast_to(jnp.max(mx, axis=-1, keepdims=True), m.shape))
            alpha = jnp.exp2(m - m_new)
            p = jnp.concatenate([jnp.exp2(t - m_new).astype(BF16) for t in tiles], axis=-1)
            acc = acc_scr[s]
            pv = _dot(p, vj)
            acc_scr[s] = jnp.concatenate([alpha * acc[:, 0:MLA_V] + pv[:, 0:MLA_V],
                                          alpha * acc[:, MLA_V:2 * MLA_V] + pv[:, MLA_V:2 * MLA_V]], axis=-1)
            m_scr[s] = m_new

    m_scr[...] = jnp.full(m_scr.shape, -jnp.inf, F32)
    acc_scr[...] = jnp.zeros(acc_scr.shape, F32)
    qk(0, 0)

    def body(i, carry):
        for r in range(ATT_UNROLL):
            u = ATT_UNROLL * i + r
            qk(u + 1, (r + 1) % 2)
            smpv(u, r % 2)
        return carry

    n_loop = (n_units - 1) // ATT_UNROLL
    lax.fori_loop(0, n_loop, body, 0)
    for u in range(ATT_UNROLL * n_loop, n_units):
        if u + 1 < n_units:
            qk(u + 1, (u + 1) % 2)
        smpv(u, u % 2)
    for s in range(nsub):
        acc = acc_scr[s]
        o_ref[0, s * ATT_SUB:(s + 1) * ATT_SUB, :] = (acc[:, 0:MLA_V] / acc[:, MLA_V:2 * MLA_V]).astype(BF16)


def _attn_ctx_kernel(q_ref, kt_ref, v_ref, o_ref):
    s = _dot(q_ref[0, 0], kt_ref[0, 0, 0])
    p = jnp.exp2(s - jnp.max(s, axis=-1, keepdims=True))
    acc = _dot(p.astype(BF16), v_ref[0, 0])
    o_ref[0] = (acc[:, 0:MLA_V] / acc[:, MLA_V:2 * MLA_V]).astype(BF16)


def _odd_out_kernel(x_ref, m_ref, o_ref, oc_ref, wo_ref, n2_ref, rwt_ref, xo_ref, hrow_ref, aff_ref, *, n_lat_tiles):
    o = jnp.where(pl.program_id(1) == n_lat_tiles, oc_ref[0], o_ref[0])
    out = _dot(o, wo_ref[...])
    _epilogue(x_ref[0], out, m_ref[0], n2_ref[...], rwt_ref[...], xo_ref, hrow_ref, aff_ref)


def _ind(c):
    return jnp.where(c, 1.0, 0.0)


def _select_kernel(a_ref, idx_ref, gate_ref, sel_scr, l_scr, a_scr, *, segs):
    ne = N_EXPERTS
    ri = lax.broadcasted_iota(I32, (LANES, LANES), 0)
    ci = lax.broadcasted_iota(I32, (LANES, LANES), 1)
    upper = _ind(ri <= ci).astype(BF16)
    ones8 = jnp.ones((8, LANES), BF16)
    ones_sq = jnp.ones((LANES, LANES), BF16)
    for k0, nt, cap, c0 in segs:
        a = a_ref[0, k0:k0 + nt]
        def count_ge(v, a=a):
            return jnp.sum(jnp.sum(_ind(a >= v[None]), axis=0), axis=-1, keepdims=True)

        def sbody(i, prefix, cap=cap):
            cand = prefix | jnp.left_shift(jnp.int32(1), 30 - i)
            return jnp.where(count_ge(lax.bitcast_convert_type(cand, F32)) >= cap, cand, prefix)

        tbits = lax.fori_loop(0, 31, sbody, jnp.zeros((ne, 1), I32))

        def rbody(i, lohi, cap=cap):
            lo, hi = lohi
            mid = lo + (hi - lo) * 0.5
            ok = count_ge(mid) >= cap
            return jnp.where(ok, mid, lo), jnp.where(ok, hi, mid)

        hbits = jnp.maximum(tbits + 1, F32_MIN_NORMAL_BITS)
        lo, hi = lax.fori_loop(0, SELECT_REFINE_STEPS, rbody,
                               (lax.bitcast_convert_type(tbits, F32), lax.bitcast_convert_type(hbits, F32)))
        gt = a >= hi[None]
        eqf = _ind(a >= lo[None]) - _ind(gt)
        need = cap - jnp.sum(jnp.sum(_ind(gt), axis=0), axis=-1, keepdims=True)
        leq = _dot(eqf.reshape(nt * ne, LANES).astype(BF16), upper).reshape(nt, ne, LANES)
        carry = jnp.zeros((ne, 1), F32)
        sel = []
        for k in range(nt):
            rank = carry + leq[k] - eqf[k]
            sel.append(jnp.where(gt[k], 1.0, jnp.where(rank < need, eqf[k], 0.0)))
            carry = carry + leq[k][:, LANES - 1:LANES]
        sel = jnp.stack(sel)
        def put(scr, val, nt=nt):
            scr[:, 0:nt, :] = jnp.transpose(val, (1, 0, 2))
            scr[:, nt:LANES, :] = jnp.zeros((ne, LANES - nt, LANES), F32)

        put(sel_scr, sel)
        put(l_scr, _dot(sel.reshape(nt * ne, LANES).astype(BF16), upper).reshape(nt, ne, LANES))
        a_hi = a.astype(BF16).astype(F32)
        a_mid = (a - a_hi).astype(BF16).astype(F32)
        put(a_scr.at[0], a_hi)
        put(a_scr.at[1], a_mid)
        put(a_scr.at[2], a - a_hi - a_mid)

        rows = -(-cap // LANES) * LANES
        slot = lax.broadcasted_iota(I32, (rows, LANES), 0).astype(F32)
        lane = lax.broadcasted_iota(I32, (rows, LANES), 1)
        lane_f = lane.astype(F32)
        idx_all = jnp.zeros((rows, LANES), F32)
        gate_all = jnp.zeros((rows, LANES), F32)
        for e in range(ne):
            sel_e = sel_scr[e].astype(BF16)
            tot = lax.dot_general(ones8, sel_e, _NT, preferred_element_type=F32)
            pb = _dot(tot.astype(BF16), upper)[0:1]
            le = pb <= slot
            kc = _dot(_ind(le).astype(BF16), ones_sq)
            base = _dot(jnp.where(le, tot[0:1], 0.0).astype(BF16), ones_sq)
            onehot = _ind(lane_f == kc).astype(BF16)
            rhs = jnp.concatenate([l_scr[e], a_scr[0, e], a_scr[1, e], a_scr[2, e]], axis=-1).astype(BF16)
            g = _dot(onehot, rhs)
            off = jnp.sum(_ind(g[:, 0:LANES] <= slot - base), axis=-1, keepdims=True)
            aff = (g[:, LANES:2 * LANES] + g[:, 2 * LANES:3 * LANES]) + g[:, 3 * LANES:4 * LANES]
            gate = jnp.sum(jnp.where(lane_f == off, aff, 0.0), axis=-1, keepdims=True)
            idx_all = jnp.where(lane == e, (kc + k0) * LANES + off, idx_all)
            gate_all = jnp.where(lane == e, gate, gate_all)
        idx_ref[0, :, c0:c0 + cap] = idx_all.T[0:ne, 0:cap].astype(I32)
        gate_ref[0, :, c0:c0 + cap] = gate_all.T[0:ne, 0:cap]


def _gather_kernel(idx_ref, h_ref, o_ref, rows):
    c_tot = rows.shape[0]

    def body(i, carry):
        base = pl.multiple_of(i * ROW_GROUP, ROW_GROUP)
        for k in range(ROW_GROUP):
            rows[base + k] = h_ref[0, idx_ref[0, 0, base + k]]
        return carry

    lax.fori_loop(0, c_tot // ROW_GROUP, body, 0)
    o_ref[0, 0] = rows[...].reshape(c_tot, D_MODEL).astype(BF16)


def _ffn_kernel(x_ref, wg_ref, wu_ref, wd_ref, o_ref):
    f = pl.program_id(2)
    x = x_ref[0, 0]
    a = _dot(x, wg_ref[0, 0].astype(BF16))
    u = _dot(x, wu_ref[0, 0].astype(BF16))
    y = _dot((_silu(a) * u).astype(BF16), wd_ref[0, 0].astype(BF16))

    @pl.when(f == 0)
    def _():
        o_ref[0, 0] = y

    @pl.when(f > 0)
    def _():
        o_ref[0, 0] += y


def _combine_kernel(idx_ref, gate_ref, x_hbm, y_ref, m_ref, o_hbm, acc, sem, *, c_lat):
    d = D_MODEL
    b = pl.program_id(0)
    e = pl.program_id(1)
    c_tot = y_ref.shape[2]

    @pl.when(e == 0)
    def _():
        cp = pltpu.make_async_copy(x_hbm.at[b], acc, sem)
        cp.start()
        cp.wait()

    for lo_, hi_, mrow in ((0, c_lat, 0), (c_lat, c_tot, 1)):
        g2 = m_ref[mrow][:, 5 * d:6 * d]

        def body(i, carry, g2=g2, lo_=lo_):
            base = pl.multiple_of(lo_ + i * ROW_GROUP, ROW_GROUP)
            ts = [idx_ref[0, 0, base + k] for k in range(ROW_GROUP)]
            new = [acc[pl.ds(ts[k], 1), :] + y_ref[0, 0, pl.ds(base + k, 1), :] * (gate_ref[0, 0, base + k] * g2)
                   for k in range(ROW_GROUP)]
            for k in range(ROW_GROUP):
                acc[pl.ds(ts[k], 1), :] = new[k]
            return carry

        lax.fori_loop(0, (hi_ - lo_) // ROW_GROUP, body, 0)

    @pl.when(e == pl.num_programs(1) - 1)
    def _():
        cp = pltpu.make_async_copy(acc.at[pl.ds(0, o_hbm.shape[1])], o_hbm.at[b], sem)
        cp.start()
        cp.wait()


def _moe(xs, hrow, aff_t, modtab, l, w_gate, w_up, w_down, n_lat, latent_only):
    bsz, t_all, d = xs.shape
    n_ctx = t_all - n_lat
    ne = N_EXPERTS
    c_lat = max(1, (EC_CAPACITY_FACTOR * n_lat) // ne)
    c_ctx = max(1, (EC_CAPACITY_FACTOR * n_ctx) // ne)
    c_tot = c_lat + c_ctx
    assert c_lat % ROW_GROUP == 0 and c_ctx % ROW_GROUP == 0
    assert n_lat % LANES == 0 and n_ctx % LANES == 0 and t_all // LANES <= LANES
    segs = ((0, n_lat // LANES, c_lat, 0), (n_lat // LANES, n_ctx // LANES, c_ctx, c_lat))
    idx, gates = pl.pallas_call(
        functools.partial(_select_kernel, segs=segs),
        grid=(bsz,),
        in_specs=[pl.BlockSpec((1, t_all // LANES, ne, LANES), lambda b: (b, 0, 0, 0))],
        out_specs=[pl.BlockSpec((1, ne, c_tot), lambda b: (b, 0, 0)), pl.BlockSpec((1, ne, c_tot), lambda b: (b, 0, 0))],
        out_shape=[jax.ShapeDtypeStruct((bsz, ne, c_tot), I32), jax.ShapeDtypeStruct((bsz, ne, c_tot), F32)],
        scratch_shapes=[pltpu.VMEM((ne, LANES, LANES), F32), pltpu.VMEM((ne, LANES, LANES), F32),
                        pltpu.VMEM((3, ne, LANES, LANES), F32)],
        compiler_params=_cparams(("arbitrary",), VMEM_LIMIT),
        name="moe_select",
    )(aff_t)
    idx = idx.reshape(bsz * ne, 1, c_tot)
    gates = gates.reshape(bsz * ne, 1, c_tot)
    smem = functools.partial(pl.BlockSpec, memory_space=pltpu.SMEM)

    xg = pl.pallas_call(
        _gather_kernel,
        grid=(bsz, ne),
        in_specs=[smem((1, 1, c_tot), lambda b, e: (b * ne + e, 0, 0)),
                  pl.BlockSpec((1, t_all, SUBLANES, LANES), lambda b, e: (b, 0, 0, 0), pipeline_mode=pl.Buffered(1))],
        out_specs=pl.BlockSpec((1, 1, c_tot, d), lambda b, e: (b, e, 0, 0)),
        out_shape=jax.ShapeDtypeStruct((bsz, ne, c_tot, d), BF16),
        scratch_shapes=[pltpu.VMEM((c_tot, SUBLANES, LANES), F32)],
        compiler_params=_cparams(("arbitrary", "arbitrary"), VMEM_LIMIT),
        name="moe_gather",
    )(idx, hrow)

    nf = EXPERT_FF // FF_TILE
    y = pl.pallas_call(
        _ffn_kernel,
        grid=(bsz, ne, nf),
        in_specs=[pl.BlockSpec((1, 1, c_tot, d), lambda b, e, f: (b, e, 0, 0)),
                  pl.BlockSpec((1, 1, d, FF_TILE), lambda b, e, f: (l, e, 0, f)),
                  pl.BlockSpec((1, 1, d, FF_TILE), lambda b, e, f: (l, e, 0, f)),
                  pl.BlockSpec((1, 1, FF_TILE, d), lambda b, e, f: (l, e, f, 0))],
        out_specs=pl.BlockSpec((1, 1, c_tot, d), lambda b, e, f: (b, e, 0, 0)),
        out_shape=jax.ShapeDtypeStruct((bsz, ne, c_tot, d), F32),
        compiler_params=_cparams(("arbitrary", "arbitrary", "arbitrary"), VMEM_LIMIT),
        name="moe_ffn",
    )(xg, w_gate, w_up, w_down)

    return pl.pallas_call(
        functools.partial(_combine_kernel, c_lat=c_lat),
        grid=(bsz, ne),
        in_specs=[smem((1, 1, c_tot), lambda b, e: (b * ne + e, 0, 0)),
                  smem((1, 1, c_tot), lambda b, e: (b * ne + e, 0, 0)),
                  pl.BlockSpec(memory_space=pl.ANY),
                  pl.BlockSpec((1, 1, c_tot, d), lambda b, e: (b, e, 0, 0)),
                  pl.BlockSpec((2, 1, 6 * d), lambda b, e: (b, 0, 0))],
        out_specs=pl.BlockSpec(memory_space=pl.ANY),
        out_shape=jax.ShapeDtypeStruct((bsz, n_lat if latent_only else t_all, d), F32),
        scratch_shapes=[pltpu.VMEM((t_all, d), F32), pltpu.SemaphoreType.DMA(())],
        input_output_aliases={} if latent_only else {2: 0},
        compiler_params=_cparams(("arbitrary", "arbitrary"), VMEM_LIMIT),
        name="moe_combine",
    )(idx, gates, xs, y, modtab)


def _tile_specs(d, n_lat_tiles):
    x_spec = pl.BlockSpec((1, TM, d), lambda b, i: (b, i, 0))
    m_spec = pl.BlockSpec((1, 1, 6 * d), lambda b, i: (2 * b + i // n_lat_tiles, 0, 0))
    return x_spec, m_spec


def _const_spec(shape):
    return pl.BlockSpec(shape, lambda b, i: (0,) * len(shape))


def _even_layer(xs, modtab, n1g, n2g, rwt, w_in_ext, lg, gn_g, conv_w, conv_b, ln_g, ln_b, w_out, cos, sin, n_lat):
    bsz, t_all, d = xs.shape
    nt = t_all // TM
    nlt = n_lat // TM
    x_spec, m_spec = _tile_specs(d, nlt)
    n_cols = 7 * 512
    proj = pl.pallas_call(
        _even_in_kernel,
        grid=(bsz, nt),
        in_specs=[x_spec, m_spec, _const_spec((1, d)), _const_spec(w_in_ext.shape),
                  pl.BlockSpec((TM, 512), lambda b, i: (i, 0)), pl.BlockSpec((TM, 512), lambda b, i: (i, 0))],
        out_specs=pl.BlockSpec((1, TM, n_cols), lambda b, i: (b, i, 0)),
        out_shape=jax.ShapeDtypeStruct((bsz, t_all, n_cols), BF16),
        compiler_params=_cparams(("arbitrary", "arbitrary"), VMEM_LIMIT),
        name="even_in",
    )(xs, modtab, n1g, w_in_ext, cos, sin)

    fwd = lambda s: jnp.where(s == 0, nlt, s - 1)
    bwd = lambda s: jnp.where(s == 0, nlt, nlt - s)

    def col(order, j):
        return pl.BlockSpec((1, TM, 512), lambda b, s: (b, order(s), j))

    o_f, o_b = pl.pallas_call(
        _ret_kernel,
        grid=(bsz, nt),
        in_specs=[pl.BlockSpec(lg.shape, lambda b, s: (0, 0, 0, 0)),
                  col(fwd, 0), col(fwd, 1), col(fwd, 2), col(bwd, 0), col(bwd, 1), col(bwd, 2)],
        out_specs=[col(fwd, 0), col(bwd, 0)],
        out_shape=[jax.ShapeDtypeStruct((bsz, t_all, 512), BF16)] * 2,
        scratch_shapes=[pltpu.VMEM((2, RET_HEADS, RET_DK, 128), F32),
                        pltpu.VMEM((2, RET_HEADS, RET_CHUNK, RET_CHUNK), F32),
                        pltpu.VMEM((2, RET_HEADS, RET_CHUNK, 128), F32),
                        pltpu.VMEM((2, RET_HEADS, RET_CHUNK, 128), F32),
                        pltpu.VMEM((2, RET_HEADS, 1, 128), F32)],
        compiler_params=_cparams(("arbitrary", "arbitrary"), VMEM_LIMIT),
        name="retention",
    )(lg, proj, proj, proj, proj, proj, proj)

    hpt = TM // CONV_HALO
    n_halo = t_all // CONV_HALO

    def tcol(j):
        return pl.BlockSpec((1, TM, 512), lambda b, i: (b, i, j))

    def halo(j, nxt):
        if nxt:
            return pl.BlockSpec((1, CONV_HALO, 512), lambda b, i: (b, jnp.minimum((i + 1) * hpt, n_halo - 1), j))
        return pl.BlockSpec((1, CONV_HALO, 512), lambda b, i: (b, jnp.maximum(i * hpt - 1, 0), j))

    xs_new, hrow, aff_t = pl.pallas_call(
        functools.partial(_even_out_kernel, n_lat_tiles=nlt),
        grid=(bsz, nt),
        in_specs=[x_spec, m_spec, pl.BlockSpec((1, TM, 512), lambda b, i: (b, i, 0)),
                  pl.BlockSpec((1, TM, 512), lambda b, i: (b, i, 0)),
                  tcol(3), tcol(4), tcol(5), tcol(6), halo(5, False), halo(6, False), halo(5, True), halo(6, True),
                  _const_spec((1, 512)), _const_spec(conv_w.shape), _const_spec((1, 512)), _const_spec((1, 512)),
                  _const_spec((1, 512)), _const_spec(w_out.shape), _const_spec((1, d)), _const_spec(rwt.shape)],
        out_specs=[x_spec, pl.BlockSpec((1, TM, SUBLANES, LANES), lambda b, i: (b, i, 0, 0)),
                   pl.BlockSpec((1, TM // LANES, N_EXPERTS, LANES), lambda b, i: (b, i, 0, 0))],
        out_shape=[jax.ShapeDtypeStruct(xs.shape, F32), jax.ShapeDtypeStruct((bsz, t_all, SUBLANES, LANES), F32),
                   jax.ShapeDtypeStruct((bsz, t_all // LANES, N_EXPERTS, LANES), F32)],
        scratch_shapes=[pltpu.VMEM((TM + 2 * CONV_HALO, CONV_CH), F32),
                        pltpu.VMEM((SUBLANES - 1, TM + 2 * CONV_HALO - SUBLANES, CONV_CH), F32)],
        input_output_aliases={0: 0},
        compiler_params=_cparams(("arbitrary", "arbitrary"), VMEM_LIMIT),
        name="even_out",
    )(xs, modtab, o_f, o_b, proj, proj, proj, proj, proj, proj, proj, proj,
      gn_g, conv_w, conv_b, ln_g, ln_b, w_out, n2g, rwt)
    return xs_new, hrow, aff_t


def _odd_layer(xs, modtab, n1g, n2g, rwt, w_in_ext, qag, kvag, w_uq_ext, w_ukv_p, gq, gk, w_out, cos, sin, n_lat):
    bsz, t_all, d = xs.shape
    nt = t_all // TM
    nlt = n_lat // TM
    nh = MLA_HEADS
    x_spec, m_spec = _tile_specs(d, nlt)
    q, kt, v = pl.pallas_call(
        _odd_in_kernel,
        grid=(bsz, nt),
        in_specs=[x_spec, m_spec, _const_spec((1, d)), _const_spec(w_in_ext.shape), _const_spec(qag.shape),
                  _const_spec(kvag.shape), _const_spec(w_uq_ext.shape), _const_spec(w_ukv_p.shape),
                  _const_spec(gq.shape), _const_spec(gk.shape),
                  pl.BlockSpec((TM, 128), lambda b, i: (i, 0)), pl.BlockSpec((TM, 128), lambda b, i: (i, 0))],
        out_specs=[pl.BlockSpec((1, nh, TM, MLA_QK_PAD), lambda b, i: (b, 0, i, 0)),
                   pl.BlockSpec((1, nh, 1, MLA_QK_PAD, TM), lambda b, i: (b, 0, i, 0, 0)),
                   pl.BlockSpec((1, nh, TM, 2 * MLA_V), lambda b, i: (b, 0, i, 0))],
        out_shape=[jax.ShapeDtypeStruct((bsz, nh, t_all, MLA_QK_PAD), BF16),
                   jax.ShapeDtypeStruct((bsz, nh, nt, MLA_QK_PAD, TM), BF16),
                   jax.ShapeDtypeStruct((bsz, nh, t_all, 2 * MLA_V), BF16)],
        compiler_params=_cparams(("arbitrary", "arbitrary"), VMEM_LIMIT),
        name="odd_in",
    )(xs, modtab, n1g, w_in_ext, qag, kvag, w_uq_ext, w_ukv_p, gq, gk, cos, sin)

    tps = next(c for c in ATT_KT_PER_STEP if nt % c == 0)
    nsub = ATT_TQ // ATT_SUB
    o = pl.pallas_call(
        functools.partial(_attn_kernel, tps=tps),
        grid=(bsz, nh, n_lat // ATT_TQ),
        in_specs=[pl.BlockSpec((1, 1, ATT_TQ, MLA_QK_PAD), lambda b, h, i: (b, h, i, 0)),
                  pl.BlockSpec((1, 1, nt, MLA_QK_PAD, TM), lambda b, h, i: (b, h, 0, 0, 0)),
                  pl.BlockSpec((1, 1, t_all, 2 * MLA_V), lambda b, h, i: (b, h, 0, 0))],
        out_specs=pl.BlockSpec((1, ATT_TQ, MLA_V), lambda b, h, i: (b, i, h)),
        out_shape=jax.ShapeDtypeStruct((bsz, n_lat, nh * MLA_V), BF16),
        scratch_shapes=[pltpu.VMEM((2, ATT_GROUP, ATT_SUB, tps * TM), F32), pltpu.VMEM((nsub, ATT_SUB, LANES), F32),
                        pltpu.VMEM((nsub, ATT_SUB, 2 * MLA_V), F32)],
        compiler_params=_cparams(("arbitrary", "arbitrary", "arbitrary"), VMEM_LIMIT),
        name="mla_attention",
    )(q, kt, v)
    o_ctx = pl.pallas_call(
        _attn_ctx_kernel,
        grid=(bsz, nh),
        in_specs=[pl.BlockSpec((1, 1, TM, MLA_QK_PAD), lambda b, h: (b, h, nlt, 0)),
                  pl.BlockSpec((1, 1, 1, MLA_QK_PAD, TM), lambda b, h: (b, h, nlt, 0, 0)),
                  pl.BlockSpec((1, 1, TM, 2 * MLA_V), lambda b, h: (b, h, nlt, 0))],
        out_specs=pl.BlockSpec((1, TM, MLA_V), lambda b, h: (b, 0, h)),
        out_shape=jax.ShapeDtypeStruct((bsz, TM, nh * MLA_V), BF16),
        compiler_params=_cparams(("arbitrary", "arbitrary"), VMEM_LIMIT),
        name="mla_attention_ctx",
    )(q, kt, v)

    xs_new, hrow, aff_t = pl.pallas_call(
        functools.partial(_odd_out_kernel, n_lat_tiles=nlt),
        grid=(bsz, nt),
        in_specs=[x_spec, m_spec, pl.BlockSpec((1, TM, nh * MLA_V), lambda b, i: (b, jnp.minimum(i, nlt - 1), 0)),
                  pl.BlockSpec((1, TM, nh * MLA_V), lambda b, i: (b, 0, 0)),
                  _const_spec(w_out.shape), _const_spec((1, d)), _const_spec(rwt.shape)],
        out_specs=[x_spec, pl.BlockSpec((1, TM, SUBLANES, LANES), lambda b, i: (b, i, 0, 0)),
                   pl.BlockSpec((1, TM // LANES, N_EXPERTS, LANES), lambda b, i: (b, i, 0, 0))],
        out_shape=[jax.ShapeDtypeStruct(xs.shape, F32), jax.ShapeDtypeStruct((bsz, t_all, SUBLANES, LANES), F32),
                   jax.ShapeDtypeStruct((bsz, t_all // LANES, N_EXPERTS, LANES), F32)],
        input_output_aliases={0: 0},
        compiler_params=_cparams(("arbitrary", "arbitrary"), VMEM_LIMIT),
        name="odd_out",
    )(xs, modtab, o, o_ctx, w_out, n2g, rwt)
    return xs_new, hrow, aff_t


def _rope_tables(n_lat, n_ctx, nf):
    t = jnp.arange(n_lat)
    row = (t // GRID_W).astype(F32)
    col = (t % GRID_W).astype(F32)
    inv = ROPE_BASE ** (-(jnp.arange(nf, dtype=F32) / nf))
    ar = row[:, None] * inv[None, :]
    ac = col[:, None] * inv[None, :]
    cos = jnp.concatenate([jnp.cos(ar), jnp.cos(ar), jnp.cos(ac), jnp.cos(ac)], axis=-1)
    sin = jnp.concatenate([-jnp.sin(ar), jnp.sin(ar), -jnp.sin(ac), jnp.sin(ac)], axis=-1)
    cos = jnp.concatenate([cos, jnp.ones((n_ctx, 4 * nf), F32)], axis=0)
    sin = jnp.concatenate([sin, jnp.zeros((n_ctx, 4 * nf), F32)], axis=0)
    return cos, sin


def _swap_perm(nf):
    a = jnp.arange(nf)
    return jnp.concatenate([a + nf, a, a + 3 * nf, a + 2 * nf])


def kernel(x, c, ctx, c_ctx, mod_w, mod_b, norm1_g, norm2_g, ev_w_in, ret_decay_f, ret_decay_b, ret_gn_g, conv_w, conv_b, conv_ln_g, conv_ln_b, ev_w_out, od_w_in, mla_q_a_g, mla_kv_a_g, mla_w_uq, mla_w_ukv, mla_qk_g_q, mla_qk_g_k, od_w_out, router_w, moe_w_gate, moe_w_up, moe_w_down):
    bsz, n_lat, d = x.shape
    n_ctx = ctx.shape[1]
    depth = mod_w.shape[0]
    assert d == D_MODEL and n_ctx == TM and n_lat % ATT_TQ == 0 and bsz + 1 <= 8

    xs = jnp.concatenate([x, ctx], axis=1)
    cc = jnp.zeros((8, d), F32).at[:bsz].set(c).at[bsz].set(c_ctx)
    mod = _mod_all(cc, mod_w, mod_b)

    cos_e, sin_e = _rope_tables(n_lat, n_ctx, RET_DK // 4)
    cos_e, sin_e = jnp.tile(cos_e, (1, RET_HEADS)), jnp.tile(sin_e, (1, RET_HEADS))
    cos_o, sin_o = _rope_tables(n_lat, n_ctx, MLA_ROPE // 4)
    cos_o, sin_o = jnp.tile(cos_o, (1, 2)), jnp.tile(sin_o, (1, 2))
    perm_e = (jnp.arange(RET_HEADS)[:, None] * RET_DK + _swap_perm(RET_DK // 4)[None, :]).reshape(-1)
    perm_o = _swap_perm(MLA_ROPE // 4)

    for l in range(depth):
        i = l // 2
        modtab = jnp.stack([mod[l, :bsz], jnp.broadcast_to(mod[l, bsz], (bsz, 6 * d))], axis=1).reshape(bsz * 2, 1, 6 * d)
        n1g = norm1_g[l][None]
        n2g = norm2_g[l][None]
        rw = router_w[l]
        rw_hi = rw.astype(BF16)
        rw_lo = (rw - rw_hi.astype(F32)).astype(BF16)
        pad = jnp.zeros((d, 128 - N_EXPERTS), BF16)
        rwt = jnp.concatenate([rw_hi, pad, rw_lo, pad], axis=1)
        if l % 2 == 0:
            w = ev_w_in[i]
            w_ext = jnp.concatenate([w, w[:, 0:512][:, perm_e], w[:, 512:1024][:, perm_e]], axis=1).astype(BF16)
            lg = jnp.broadcast_to(jnp.stack([ret_decay_f[i], ret_decay_b[i]])[:, :, None, None], (2, RET_HEADS, 1, 128))
            cw = jnp.concatenate([conv_w[i], jnp.zeros((1, CONV_CH), F32)], axis=0)
            xs, hrow, aff_t = _even_layer(xs, modtab, n1g, n2g, rwt, w_ext, lg, ret_gn_g[i][None], cw, conv_b[i][None],
                                    conv_ln_g[i][None], conv_ln_b[i][None], ev_w_out[i].astype(BF16), cos_e, sin_e, n_lat)
        else:
            w = od_w_in[i]
            kpe = w[:, MLA_Q_LORA + MLA_KV_LORA:]
            w_ext = jnp.concatenate([w[:, :MLA_Q_LORA + MLA_KV_LORA], kpe, kpe, kpe[:, perm_o], kpe[:, perm_o]], axis=1).astype(BF16)
            wq = mla_w_uq[i].reshape(MLA_Q_LORA, MLA_HEADS, MLA_QK)
            wq_r = wq[:, :, MLA_NOPE:]
            w_uq_ext = jnp.concatenate([wq[:, :, :MLA_NOPE].reshape(MLA_Q_LORA, -1), wq_r.reshape(MLA_Q_LORA, -1),
                                        wq_r[:, :, perm_o].reshape(MLA_Q_LORA, -1)], axis=1).astype(BF16)
            wkv = mla_w_ukv[i].reshape(MLA_KV_LORA, MLA_HEADS, MLA_NOPE + MLA_V)
            w_ukv_p = jnp.concatenate([wkv[:, :, :MLA_NOPE].reshape(MLA_KV_LORA, -1),
                                       wkv[:, :, MLA_NOPE:].reshape(MLA_KV_LORA, -1)], axis=1).astype(BF16)

            def gains(g):
                gr = g[MLA_NOPE:]
                return jnp.stack([g[:MLA_NOPE], jnp.tile(gr, 2), jnp.tile(gr[perm_o], 2)])

            xs, hrow, aff_t = _odd_layer(xs, modtab, n1g, n2g, rwt, w_ext, mla_q_a_g[i][None], mla_kv_a_g[i][None], w_uq_ext,
                                   w_ukv_p, gains(mla_qk_g_q[i]), gains(mla_qk_g_k[i]), od_w_out[i].astype(BF16),
                                   cos_o, sin_o, n_lat)
        xs = _moe(xs, hrow, aff_t, modtab, l, moe_w_gate, moe_w_up, moe_w_down, n_lat, latent_only=l == depth - 1)
    return xs
```

```python
import functools

import jax
import jax.numpy as jnp
from jax import lax
from jax.experimental import pallas as pl
from jax.experimental.pallas import tpu as pltpu

F32 = jnp.float32
BF16 = jnp.bfloat16
I32 = jnp.int32

D_MODEL = 1024
GRID_W = 64
RET_HEADS = 4
RET_DK = 128
RET_CHUNK = 256
CONV_CH = 512
CONV_WIDTH = 31
CONV_HALO = 16
MLA_HEADS = 8
MLA_Q_LORA = 512
MLA_KV_LORA = 256
MLA_NOPE = 128
MLA_ROPE = 64
MLA_V = 128
MLA_QK = MLA_NOPE + MLA_ROPE
MLA_QK_PAD = 256
N_EXPERTS = 16
EC_CAPACITY_FACTOR = 2
EXPERT_FF = 1024
ROPE_BASE = 10000.0
LOG2E = 1.4426950408889634
EPS = 1e-6

LANES = 128
SUBLANES = 8
TM = 256
ATT_TQ = 2048
ATT_SUB = 512
ATT_GROUP = 1
ATT_UNROLL = 4
ATT_KT_PER_STEP = (3, 2, 1)
FF_TILE = 1024
SELECT_REFINE_STEPS = 29
F32_MIN_NORMAL_BITS = 0x00800000
ROW_GROUP = 8
VMEM_LIMIT = 56 * 2 ** 20

_NT = (((1,), (1,)), ((), ()))


def _cparams(sem, vmem=None):
    return pltpu.CompilerParams(dimension_semantics=sem, vmem_limit_bytes=vmem)


def _dot(a, b):
    return jnp.dot(a, b, preferred_element_type=F32)


def _split3(a):
    a1 = a.astype(BF16)
    r = a - a1.astype(F32)
    a2 = r.astype(BF16)
    a3 = (r - a2.astype(F32)).astype(BF16)
    return a1, a2, a3


def _dot_hi(a, b, dn):
    a1, a2, a3 = _split3(a)
    b1, b2, b3 = _split3(b)
    d = lambda x, y: lax.dot_general(x, y, dn, preferred_element_type=F32)
    return ((d(a3, b1) + d(a2, b2) + d(a1, b3)) + (d(a2, b1) + d(a1, b2))) + d(a1, b1)


def _silu(a):
    return a * jax.nn.sigmoid(a)


def _rms(x, g):
    return x * lax.rsqrt(jnp.mean(x * x, axis=-1, keepdims=True) + EPS) * g


def _modulate(x, g, shift, scale):
    return _rms(x, g) * (1.0 + scale) + shift


def _mod_kernel(c_ref, w_ref, b_ref, o_ref):
    s = _silu(c_ref[...])
    o_ref[0] = _dot_hi(s, w_ref[0], (((1,), (0,)), ((), ()))) + b_ref[0]


def _mod_all(cc, mod_w, mod_b):
    depth, d, n6 = mod_w.shape
    tn = 1536
    return pl.pallas_call(
        _mod_kernel,
        grid=(depth, n6 // tn),
        in_specs=[pl.BlockSpec((8, d), lambda l, j: (0, 0)),
                  pl.BlockSpec((1, d, tn), lambda l, j: (l, 0, j)),
                  pl.BlockSpec((1, 1, tn), lambda l, j: (l, 0, j))],
        out_specs=pl.BlockSpec((1, 8, tn), lambda l, j: (l, 0, j)),
        out_shape=jax.ShapeDtypeStruct((depth, 8, n6), F32),
        compiler_params=_cparams(("arbitrary", "arbitrary"), VMEM_LIMIT),
        name="mod_all",
    )(cc, mod_w, mod_b.reshape(depth, 1, n6))


def _epilogue(x, out, m, n2g, rwt, xo_ref, hrow_ref, aff_ref):
    d = D_MODEL
    xn = x + m[:, 2 * d:3 * d] * out
    xo_ref[0] = xn
    h2 = _modulate(xn, n2g, m[:, 3 * d:4 * d], m[:, 4 * d:5 * d])
    hrow_ref[0] = h2.reshape(TM, SUBLANES, LANES)
    h_hi = h2.astype(BF16)
    h_lo = (h2 - h_hi.astype(F32)).astype(BF16)
    z = _dot(h_hi, rwt)
    logits = (z[:, 0:128] + z[:, 128:256]) + _dot(h_lo, rwt[:, 0:128])
    lane = lax.broadcasted_iota(I32, logits.shape, 1)
    logits = jnp.where(lane < N_EXPERTS, logits, -jnp.inf)
    e = jnp.exp(logits - jnp.max(logits, axis=-1, keepdims=True))
    aff = e / jnp.sum(e, axis=-1, keepdims=True)
    aff_t = aff.T
    for k in range(TM // LANES):
        aff_ref[0, k] = aff_t[0:N_EXPERTS, k * LANES:(k + 1) * LANES]


def _even_in_kernel(x_ref, m_ref, g_ref, w_ref, cos_ref, sin_ref, o_ref):
    d = D_MODEL
    m = m_ref[0]
    h = _modulate(x_ref[0], g_ref[...], m[:, 0:d], m[:, d:2 * d]).astype(BF16)

    def grp(j):
        return _dot(h, w_ref[:, j * 512:(j + 1) * 512])

    c = cos_ref[...]
    s = sin_ref[...]
    nf = RET_DK // 4
    lane = lax.broadcasted_iota(I32, (TM, 512), 1)
    first = (lane % (2 * nf)) < nf

    def partner(v):
        return jnp.where(first, pltpu.roll(v, 512 - nf, axis=1), pltpu.roll(v, nf, axis=1))

    q = grp(0)
    k = grp(1)
    o_ref[0, :, 0:512] = (q * c + partner(q) * s).astype(BF16)
    o_ref[0, :, 512:1024] = ((k * c + partner(k) * s) * (RET_DK ** -0.5)).astype(BF16)
    for j in range(2, 7):
        o_ref[0, :, j * 512:(j + 1) * 512] = grp(j).astype(BF16)


def _ret_kernel(lg_ref, qf_ref, kf_ref, vf_ref, qb_ref, kb_ref, vb_ref, of_ref, ob_ref,
                st_ref, intra_ref, xi_ref, zeta_ref, dc_ref):
    L = RET_CHUNK
    nc = TM // L

    @pl.when(pl.program_id(1) == 0)
    def _():
        st_ref[...] = jnp.zeros(st_ref.shape, F32)
        ii = lax.broadcasted_iota(I32, (L, L), 0).astype(F32)
        jj = lax.broadcasted_iota(I32, (L, L), 1).astype(F32)
        pos = lax.broadcasted_iota(I32, (L, LANES), 0).astype(F32)
        for dr in range(2):
            for h in range(RET_HEADS):
                x = lg_ref[dr, h]
                lg = jnp.minimum(x, 0.0) - jnp.log(1.0 + jnp.exp(-jnp.abs(x)))
                if dr == 0:
                    diff, xpos, zpos = ii - jj, pos + 1.0, (L - 1.0) - pos
                else:
                    diff, xpos, zpos = jj - ii, L - pos, pos
                intra_ref[dr, h] = jnp.where(diff >= 0, jnp.exp(lg[:, 0:1] * jnp.maximum(diff, 0.0)), 0.0)
                xi_ref[dr, h] = jnp.exp(lg * xpos)
                zeta_ref[dr, h] = jnp.exp(lg * zpos)
                dc_ref[dr, h] = jnp.exp(lg * float(L))

    for dr, (q_ref, k_ref, v_ref, o_ref) in enumerate(((qf_ref, kf_ref, vf_ref, of_ref),
                                                      (qb_ref, kb_ref, vb_ref, ob_ref))):
        order = range(nc) if dr == 0 else range(nc - 1, -1, -1)
        for c in order:
            for h in range(RET_HEADS):
                rs = slice(c * L, (c + 1) * L)
                cs = slice(h * 128, (h + 1) * 128)
                qc = q_ref[0, rs, cs]
                kc = k_ref[0, rs, cs]
                vc = v_ref[0, rs, cs]
                st = st_ref[dr, h]
                sc = lax.dot_general(qc, kc, _NT, preferred_element_type=F32) * intra_ref[dr, h]
                o = _dot(sc.astype(BF16), vc) + _dot(qc, st.astype(BF16)) * xi_ref[dr, h]
                kz = (kc.astype(F32) * zeta_ref[dr, h]).T.astype(BF16)
                st_ref[dr, h] = st * dc_ref[dr, h] + _dot(kz, vc)
                o_ref[0, rs, cs] = o.astype(BF16)


def _even_out_kernel(x_ref, m_ref, of_ref, ob_ref, gf_ref, gb_ref, bv_ref, bg_ref,
                     pv_ref, pg_ref, nv_ref, ng_ref, gn_ref, cw_ref, cb_ref, lng_ref, lnb_ref,
                     wo_ref, n2_ref, rwt_ref, xo_ref, hrow_ref, aff_ref, u_scr, ush_scr, *, n_lat_tiles):
    i = pl.program_id(1)
    hl = CONV_HALO

    def glu(v_ref, g_ref):
        return v_ref[0].astype(F32) * jax.nn.sigmoid(g_ref[0].astype(F32))

    prev_ok = jnp.logical_and(i >= 1, i < n_lat_tiles)
    next_ok = i < n_lat_tiles - 1
    u_scr[0:hl, :] = jnp.where(prev_ok, glu(pv_ref, pg_ref), 0.0)
    u_scr[hl:hl + TM, :] = glu(bv_ref, bg_ref)
    u_scr[hl + TM:2 * hl + TM, :] = jnp.where(next_ok, glu(nv_ref, ng_ref), 0.0)
    n_sh = TM + 2 * hl - SUBLANES
    for r in range(1, SUBLANES):
        ush_scr[r - 1] = u_scr[r:r + n_sh, :]
    y = jnp.zeros((TM, CONV_CH), F32) + cb_ref[...]
    off = hl - CONV_WIDTH // 2
    for k in range(CONV_WIDTH):
        r = (k + off) % SUBLANES
        a = k + off - r
        win = u_scr[a:a + TM, :] if r == 0 else ush_scr[r - 1, a:a + TM, :]
        y = y + cw_ref[k:k + 1, :] * win
    mu = jnp.mean(y, axis=-1, keepdims=True)
    yc = y - mu
    var = jnp.mean(yc * yc, axis=-1, keepdims=True)
    conv = _silu(yc * lax.rsqrt(var + EPS) * lng_ref[...] + lnb_ref[...])

    def gnorm(o_ref, g_ref):
        parts = []
        for h in range(RET_HEADS):
            cs = slice(h * 128, (h + 1) * 128)
            o = o_ref[0, :, cs].astype(F32)
            mu_h = jnp.mean(o, axis=-1, keepdims=True)
            oc = o - mu_h
            var_h = jnp.mean(oc * oc, axis=-1, keepdims=True)
            parts.append(oc * lax.rsqrt(var_h + EPS) * gn_ref[:, cs] * _silu(g_ref[0, :, cs].astype(F32)))
        return parts

    pf = gnorm(of_ref, gf_ref)
    pb = gnorm(ob_ref, gb_ref)
    out = _dot(conv.astype(BF16), wo_ref[512:1024, :])
    for h in range(RET_HEADS):
        out = out + _dot((pf[h] + pb[h]).astype(BF16), wo_ref[h * 128:(h + 1) * 128, :])
    _epilogue(x_ref[0], out, m_ref[0], n2_ref[...], rwt_ref[...], xo_ref, hrow_ref, aff_ref)


def _odd_in_kernel(x_ref, m_ref, g_ref, win_ref, qag_ref, kvag_ref, wuq_ref, wukv_ref,
                   gq_ref, gk_ref, cos_ref, sin_ref, q_ref, kt_ref, v_ref):
    d = D_MODEL
    nh = MLA_HEADS
    m = m_ref[0]
    h = _modulate(x_ref[0], g_ref[...], m[:, 0:d], m[:, d:2 * d]).astype(BF16)
    c = _dot(h, win_ref[...])
    cq = _rms(c[:, 0:512], qag_ref[...]).astype(BF16)
    ckv = _rms(c[:, 512:768], kvag_ref[...]).astype(BF16)
    kpe = c[:, 768:896]
    kpe_sw = c[:, 896:1024]
    qa = _dot(cq, wuq_ref[...])
    kva = _dot(ckv, wukv_ref[...])
    cos = cos_ref[...]
    sin = sin_ref[...]
    gq = gq_ref[...]
    gk = gk_ref[...]
    lane = lax.broadcasted_iota(I32, (TM, 128), 1)
    lo = lane < MLA_ROPE
    inv_d = 1.0 / MLA_QK

    ss_kpe = 0.5 * jnp.sum(kpe * kpe, axis=-1, keepdims=True)
    kr = kpe * gk[1:2] * cos + kpe_sw * gk[2:3] * sin
    for p in range(nh // 2):
        r = qa[:, 1024 + p * 128:1024 + (p + 1) * 128]
        r_sw = qa[:, 1536 + p * 128:1536 + (p + 1) * 128]
        r2 = r * r
        ss_r = (jnp.sum(jnp.where(lo, r2, 0.0), axis=-1, keepdims=True),
                jnp.sum(jnp.where(lo, 0.0, r2), axis=-1, keepdims=True))
        qr = r * gq[1:2] * cos + r_sw * gq[2:3] * sin
        for s in range(2):
            hd = 2 * p + s
            keep = lo if s == 0 else jnp.logical_not(lo)
            qn = qa[:, hd * 128:(hd + 1) * 128]
            nq = lax.rsqrt((jnp.sum(qn * qn, axis=-1, keepdims=True) + ss_r[s]) * inv_d + EPS) * (MLA_QK ** -0.5 * LOG2E)
            q_ref[0, hd, :, 0:128] = (qn * nq * gq[0:1]).astype(BF16)
            q_ref[0, hd, :, 128:256] = jnp.where(keep, qr * nq, 0.0).astype(BF16)
            kn = kva[:, hd * 128:(hd + 1) * 128]
            nk = lax.rsqrt((jnp.sum(kn * kn, axis=-1, keepdims=True) + ss_kpe) * inv_d + EPS)
            k_full = jnp.concatenate([kn * nk * gk[0:1], jnp.where(keep, kr * nk, 0.0)], axis=-1)
            kt_ref[0, hd, 0] = k_full.T.astype(BF16)
            v_ref[0, hd, :, 0:MLA_V] = kva[:, 1024 + hd * 128:1024 + (hd + 1) * 128].astype(BF16)
            v_ref[0, hd, :, MLA_V:2 * MLA_V] = jnp.ones((TM, MLA_V), BF16)


def _attn_kernel(q_ref, kt_ref, v_ref, o_ref, s_scr, m_scr, acc_scr, *, tps):
    tq = q_ref.shape[2]
    nsub = tq // ATT_SUB
    kw = kt_ref.shape[4]
    n_steps = kt_ref.shape[2] // tps
    n_units = (nsub // ATT_GROUP) * n_steps

    def split(u):
        g = u // n_steps
        return g, u - g * n_steps

    def qk(u, buf):
        g, j = split(u)
        for k in range(ATT_GROUP):
            q = q_ref[0, 0, pl.ds(pl.multiple_of((g * ATT_GROUP + k) * ATT_SUB, ATT_SUB), ATT_SUB), :]
            for c in range(tps):
                s_scr[buf, k, :, c * kw:(c + 1) * kw] = _dot(q, kt_ref[0, 0, j * tps + c])

    def smpv(u, buf):
        g, j = split(u)
        vj = v_ref[0, 0, pl.ds(pl.multiple_of(j * (tps * kw), tps * kw), tps * kw), :]
        for k in range(ATT_GROUP):
            s = g * ATT_GROUP + k
            sv = s_scr[buf, k]
            m = m_scr[s]
            tiles = [sv[:, c * LANES:(c + 1) * LANES] for c in range(sv.shape[1] // LANES)]
            mx = functools.reduce(jnp.maximum, tiles)
            m_new = jnp.maximum(m, jnp.broadcast_to(jnp.max(mx, axis=-1, keepdims=True), m.shape))
            alpha = jnp.exp2(m - m_new)
            p = jnp.concatenate([jnp.exp2(t - m_new).astype(BF16) for t in tiles], axis=-1)
            acc = acc_scr[s]
            pv = _dot(p, vj)
            acc_scr[s] = jnp.concatenate([alpha * acc[:, 0:MLA_V] + pv[:, 0:MLA_V],
                                          alpha * acc[:, MLA_V:2 * MLA_V] + pv[:, MLA_V:2 * MLA_V]], axis=-1)
            m_scr[s] = m_new

    m_scr[...] = jnp.full(m_scr.shape, -jnp.inf, F32)
    acc_scr[...] = jnp.zeros(acc_scr.shape, F32)
    qk(0, 0)

    def body(i, carry):
        for r in range(ATT_UNROLL):
            u = ATT_UNROLL * i + r
            qk(u + 1, (r + 1) % 2)
            smpv(u, r % 2)
        return carry

    n_loop = (n_units - 1) // ATT_UNROLL
    lax.fori_loop(0, n_loop, body, 0)
    for u in range(ATT_UNROLL * n_loop, n_units):
        if u + 1 < n_units:
            qk(u + 1, (u + 1) % 2)
        smpv(u, u % 2)
    for s in range(nsub):
        acc = acc_scr[s]
        o_ref[0, s * ATT_SUB:(s + 1) * ATT_SUB, :] = (acc[:, 0:MLA_V] / acc[:, MLA_V:2 * MLA_V]).astype(BF16)


def _attn_ctx_kernel(q_ref, kt_ref, v_ref, o_ref):
    s = _dot(q_ref[0, 0], kt_ref[0, 0, 0])
    p = jnp.exp2(s - jnp.max(s, axis=-1, keepdims=True))
    acc = _dot(p.astype(BF16), v_ref[0, 0])
    o_ref[0] = (acc[:, 0:MLA_V] / acc[:, MLA_V:2 * MLA_V]).astype(BF16)


def _odd_out_kernel(x_ref, m_ref, o_ref, oc_ref, wo_ref, n2_ref, rwt_ref, xo_ref, hrow_ref, aff_ref, *, n_lat_tiles):
    o = jnp.where(pl.program_id(1) == n_lat_tiles, oc_ref[0], o_ref[0])
    out = _dot(o, wo_ref[...])
    _epilogue(x_ref[0], out, m_ref[0], n2_ref[...], rwt_ref[...], xo_ref, hrow_ref, aff_ref)


def _ind(c):
    return jnp.where(c, 1.0, 0.0)


def _select_kernel(a_ref, idx_ref, gate_ref, sel_scr, l_scr, a_scr, *, segs):
    ne = N_EXPERTS
    ri = lax.broadcasted_iota(I32, (LANES, LANES), 0)
    ci = lax.broadcasted_iota(I32, (LANES, LANES), 1)
    upper = _ind(ri <= ci).astype(BF16)
    ones8 = jnp.ones((8, LANES), BF16)
    ones_sq = jnp.ones((LANES, LANES), BF16)
    for k0, nt, cap, c0 in segs:
        a = a_ref[0, k0:k0 + nt]
        def count_ge(v, a=a):
            return jnp.sum(jnp.sum(_ind(a >= v[None]), axis=0), axis=-1, keepdims=True)

        def sbody(i, prefix, cap=cap):
            cand = prefix | jnp.left_shift(jnp.int32(1), 30 - i)
            return jnp.where(count_ge(lax.bitcast_convert_type(cand, F32)) >= cap, cand, prefix)

        tbits = lax.fori_loop(0, 31, sbody, jnp.zeros((ne, 1), I32))

        def rbody(i, lohi, cap=cap):
            lo, hi = lohi
            mid = lo + (hi - lo) * 0.5
            ok = count_ge(mid) >= cap
            return jnp.where(ok, mid, lo), jnp.where(ok, hi, mid)

        hbits = jnp.maximum(tbits + 1, F32_MIN_NORMAL_BITS)
        lo, hi = lax.fori_loop(0, SELECT_REFINE_STEPS, rbody,
                               (lax.bitcast_convert_type(tbits, F32), lax.bitcast_convert_type(hbits, F32)))
        gt = a >= hi[None]
        eqf = _ind(a >= lo[None]) - _ind(gt)
        need = cap - jnp.sum(jnp.sum(_ind(gt), axis=0), axis=-1, keepdims=True)
        leq = _dot(eqf.reshape(nt * ne, LANES).astype(BF16), upper).reshape(nt, ne, LANES)
        carry = jnp.zeros((ne, 1), F32)
        sel = []
        for k in range(nt):
            rank = carry + leq[k] - eqf[k]
            sel.append(jnp.where(gt[k], 1.0, jnp.where(rank < need, eqf[k], 0.0)))
            carry = carry + leq[k][:, LANES - 1:LANES]
        sel = jnp.stack(sel)
        def put(scr, val, nt=nt):
            scr[:, 0:nt, :] = jnp.transpose(val, (1, 0, 2))
            scr[:, nt:LANES, :] = jnp.zeros((ne, LANES - nt, LANES), F32)

        put(sel_scr, sel)
        put(l_scr, _dot(sel.reshape(nt * ne, LANES).astype(BF16), upper).reshape(nt, ne, LANES))
        a_hi = a.astype(BF16).astype(F32)
        a_mid = (a - a_hi).astype(BF16).astype(F32)
        put(a_scr.at[0], a_hi)
        put(a_scr.at[1], a_mid)
        put(a_scr.at[2], a - a_hi - a_mid)

        rows = -(-cap // LANES) * LANES
        slot = lax.broadcasted_iota(I32, (rows, LANES), 0).astype(F32)
        lane = lax.broadcasted_iota(I32, (rows, LANES), 1)
        lane_f = lane.astype(F32)
        idx_all = jnp.zeros((rows, LANES), F32)
        gate_all = jnp.zeros((rows, LANES), F32)
        for e in range(ne):
            sel_e = sel_scr[e].astype(BF16)
            tot = lax.dot_general(ones8, sel_e, _NT, preferred_element_type=F32)
            pb = _dot(tot.astype(BF16), upper)[0:1]
            le = pb <= slot
            kc = _dot(_ind(le).astype(BF16), ones_sq)
            base = _dot(jnp.where(le, tot[0:1], 0.0).astype(BF16), ones_sq)
            onehot = _ind(lane_f == kc).astype(BF16)
            rhs = jnp.concatenate([l_scr[e], a_scr[0, e], a_scr[1, e], a_scr[2, e]], axis=-1).astype(BF16)
            g = _dot(onehot, rhs)
            off = jnp.sum(_ind(g[:, 0:LANES] <= slot - base), axis=-1, keepdims=True)
            aff = (g[:, LANES:2 * LANES] + g[:, 2 * LANES:3 * LANES]) + g[:, 3 * LANES:4 * LANES]
            gate = jnp.sum(jnp.where(lane_f == off, aff, 0.0), axis=-1, keepdims=True)
            idx_all = jnp.where(lane == e, (kc + k0) * LANES + off, idx_all)
            gate_all = jnp.where(lane == e, gate, gate_all)
        idx_ref[0, :, c0:c0 + cap] = idx_all.T[0:ne, 0:cap].astype(I32)
        gate_ref[0, :, c0:c0 + cap] = gate_all.T[0:ne, 0:cap]


def _gather_kernel(idx_ref, h_ref, o_ref, rows):
    c_tot = rows.shape[0]

    def body(i, carry):
        base = pl.multiple_of(i * ROW_GROUP, ROW_GROUP)
        for k in range(ROW_GROUP):
            rows[base + k] = h_ref[0, idx_ref[0, 0, base + k]]
        return carry

    lax.fori_loop(0, c_tot // ROW_GROUP, body, 0)
    o_ref[0, 0] = rows[...].reshape(c_tot, D_MODEL).astype(BF16)


def _ffn_kernel(x_ref, wg_ref, wu_ref, wd_ref, o_ref):
    f = pl.program_id(2)
    x = x_ref[0, 0]
    a = _dot(x, wg_ref[0, 0].astype(BF16))
    u = _dot(x, wu_ref[0, 0].astype(BF16))
    y = _dot((_silu(a) * u).astype(BF16), wd_ref[0, 0].astype(BF16))

    @pl.when(f == 0)
    def _():
        o_ref[0, 0] = y

    @pl.when(f > 0)
    def _():
        o_ref[0, 0] += y


def _combine_kernel(idx_ref, gate_ref, x_hbm, y_ref, m_ref, o_hbm, acc, sem, *, c_lat):
    d = D_MODEL
    b = pl.program_id(0)
    e = pl.program_id(1)
    c_tot = y_ref.shape[2]

    @pl.when(e == 0)
    def _():
        cp = pltpu.make_async_copy(x_hbm.at[b], acc, sem)
        cp.start()
        cp.wait()

    for lo_, hi_, mrow in ((0, c_lat, 0), (c_lat, c_tot, 1)):
        g2 = m_ref[mrow][:, 5 * d:6 * d]

        def body(i, carry, g2=g2, lo_=lo_):
            base = pl.multiple_of(lo_ + i * ROW_GROUP, ROW_GROUP)
            ts = [idx_ref[0, 0, base + k] for k in range(ROW_GROUP)]
            new = [acc[pl.ds(ts[k], 1), :] + y_ref[0, 0, pl.ds(base + k, 1), :] * (gate_ref[0, 0, base + k] * g2)
                   for k in range(ROW_GROUP)]
            for k in range(ROW_GROUP):
                acc[pl.ds(ts[k], 1), :] = new[k]
            return carry

        lax.fori_loop(0, (hi_ - lo_) // ROW_GROUP, body, 0)

    @pl.when(e == pl.num_programs(1) - 1)
    def _():
        cp = pltpu.make_async_copy(acc.at[pl.ds(0, o_hbm.shape[1])], o_hbm.at[b], sem)
        cp.start()
        cp.wait()


def _moe(xs, hrow, aff_t, modtab, l, w_gate, w_up, w_down, n_lat, latent_only):
    bsz, t_all, d = xs.shape
    n_ctx = t_all - n_lat
    ne = N_EXPERTS
    c_lat = max(1, (EC_CAPACITY_FACTOR * n_lat) // ne)
    c_ctx = max(1, (EC_CAPACITY_FACTOR * n_ctx) // ne)
    c_tot = c_lat + c_ctx
    assert c_lat % ROW_GROUP == 0 and c_ctx % ROW_GROUP == 0
    assert n_lat % LANES == 0 and n_ctx % LANES == 0 and t_all // LANES <= LANES
    segs = ((0, n_lat // LANES, c_lat, 0), (n_lat // LANES, n_ctx // LANES, c_ctx, c_lat))
    idx, gates = pl.pallas_call(
        functools.partial(_select_kernel, segs=segs),
        grid=(bsz,),
        in_specs=[pl.BlockSpec((1, t_all // LANES, ne, LANES), lambda b: (b, 0, 0, 0))],
        out_specs=[pl.BlockSpec((1, ne, c_tot), lambda b: (b, 0, 0)), pl.BlockSpec((1, ne, c_tot), lambda b: (b, 0, 0))],
        out_shape=[jax.ShapeDtypeStruct((bsz, ne, c_tot), I32), jax.ShapeDtypeStruct((bsz, ne, c_tot), F32)],
        scratch_shapes=[pltpu.VMEM((ne, LANES, LANES), F32), pltpu.VMEM((ne, LANES, LANES), F32),
                        pltpu.VMEM((3, ne, LANES, LANES), F32)],
        compiler_params=_cparams(("arbitrary",), VMEM_LIMIT),
        name="moe_select",
    )(aff_t)
    idx = idx.reshape(bsz * ne, 1, c_tot)
    gates = gates.reshape(bsz * ne, 1, c_tot)
    smem = functools.partial(pl.BlockSpec, memory_space=pltpu.SMEM)

    xg = pl.pallas_call(
        _gather_kernel,
        grid=(bsz, ne),
        in_specs=[smem((1, 1, c_tot), lambda b, e: (b * ne + e, 0, 0)),
                  pl.BlockSpec((1, t_all, SUBLANES, LANES), lambda b, e: (b, 0, 0, 0), pipeline_mode=pl.Buffered(1))],
        out_specs=pl.BlockSpec((1, 1, c_tot, d), lambda b, e: (b, e, 0, 0)),
        out_shape=jax.ShapeDtypeStruct((bsz, ne, c_tot, d), BF16),
        scratch_shapes=[pltpu.VMEM((c_tot, SUBLANES, LANES), F32)],
        compiler_params=_cparams(("arbitrary", "arbitrary"), VMEM_LIMIT),
        name="moe_gather",
    )(idx, hrow)

    nf = EXPERT_FF // FF_TILE
    y = pl.pallas_call(
        _ffn_kernel,
        grid=(bsz, ne, nf),
        in_specs=[pl.BlockSpec((1, 1, c_tot, d), lambda b, e, f: (b, e, 0, 0)),
                  pl.BlockSpec((1, 1, d, FF_TILE), lambda b, e, f: (l, e, 0, f)),
                  pl.BlockSpec((1, 1, d, FF_TILE), lambda b, e, f: (l, e, 0, f)),
                  pl.BlockSpec((1, 1, FF_TILE, d), lambda b, e, f: (l, e, f, 0))],
        out_specs=pl.BlockSpec((1, 1, c_tot, d), lambda b, e, f: (b, e, 0, 0)),
        out_shape=jax.ShapeDtypeStruct((bsz, ne, c_tot, d), F32),
        compiler_params=_cparams(("arbitrary", "arbitrary", "arbitrary"), VMEM_LIMIT),
        name="moe_ffn",
    )(xg, w_gate, w_up, w_down)

    return pl.pallas_call(
        functools.partial(_combine_kernel, c_lat=c_lat),
        grid=(bsz, ne),
        in_specs=[smem((1, 1, c_tot), lambda b, e: (b * ne + e, 0, 0)),
                  smem((1, 1, c_tot), lambda b, e: (b * ne + e, 0, 0)),
                  pl.BlockSpec(memory_space=pl.ANY),
                  pl.BlockSpec((1, 1, c_tot, d), lambda b, e: (b, e, 0, 0)),
                  pl.BlockSpec((2, 1, 6 * d), lambda b, e: (b, 0, 0))],
        out_specs=pl.BlockSpec(memory_space=pl.ANY),
        out_shape=jax.ShapeDtypeStruct((bsz, n_lat if latent_only else t_all, d), F32),
        scratch_shapes=[pltpu.VMEM((t_all, d), F32), pltpu.SemaphoreType.DMA(())],
        input_output_aliases={} if latent_only else {2: 0},
        compiler_params=_cparams(("arbitrary", "arbitrary"), VMEM_LIMIT),
        name="moe_combine",
    )(idx, gates, xs, y, modtab)


def _tile_specs(d, n_lat_tiles):
    x_spec = pl.BlockSpec((1, TM, d), lambda b, i: (b, i, 0))
    m_spec = pl.BlockSpec((1, 1, 6 * d), lambda b, i: (2 * b + i // n_lat_tiles, 0, 0))
    return x_spec, m_spec


def _const_spec(shape):
    return pl.BlockSpec(shape, lambda b, i: (0,) * len(shape))


def _even_layer(xs, modtab, n1g, n2g, rwt, w_in_ext, lg, gn_g, conv_w, conv_b, ln_g, ln_b, w_out, cos, sin, n_lat):
    bsz, t_all, d = xs.shape
    nt = t_all // TM
    nlt = n_lat // TM
    x_spec, m_spec = _tile_specs(d, nlt)
    n_cols = 7 * 512
    proj = pl.pallas_call(
        _even_in_kernel,
        grid=(bsz, nt),
        in_specs=[x_spec, m_spec, _const_spec((1, d)), _const_spec(w_in_ext.shape),
                  pl.BlockSpec((TM, 512), lambda b, i: (i, 0)), pl.BlockSpec((TM, 512), lambda b, i: (i, 0))],
        out_specs=pl.BlockSpec((1, TM, n_cols), lambda b, i: (b, i, 0)),
        out_shape=jax.ShapeDtypeStruct((bsz, t_all, n_cols), BF16),
        compiler_params=_cparams(("arbitrary", "arbitrary"), VMEM_LIMIT),
        name="even_in",
    )(xs, modtab, n1g, w_in_ext, cos, sin)

    fwd = lambda s: jnp.where(s == 0, nlt, s - 1)
    bwd = lambda s: jnp.where(s == 0, nlt, nlt - s)

    def col(order, j):
        return pl.BlockSpec((1, TM, 512), lambda b, s: (b, order(s), j))

    o_f, o_b = pl.pallas_call(
        _ret_kernel,
        grid=(bsz, nt),
        in_specs=[pl.BlockSpec(lg.shape, lambda b, s: (0, 0, 0, 0)),
                  col(fwd, 0), col(fwd, 1), col(fwd, 2), col(bwd, 0), col(bwd, 1), col(bwd, 2)],
        out_specs=[col(fwd, 0), col(bwd, 0)],
        out_shape=[jax.ShapeDtypeStruct((bsz, t_all, 512), BF16)] * 2,
        scratch_shapes=[pltpu.VMEM((2, RET_HEADS, RET_DK, 128), F32),
                        pltpu.VMEM((2, RET_HEADS, RET_CHUNK, RET_CHUNK), F32),
                        pltpu.VMEM((2, RET_HEADS, RET_CHUNK, 128), F32),
                        pltpu.VMEM((2, RET_HEADS, RET_CHUNK, 128), F32),
                        pltpu.VMEM((2, RET_HEADS, 1, 128), F32)],
        compiler_params=_cparams(("arbitrary", "arbitrary"), VMEM_LIMIT),
        name="retention",
    )(lg, proj, proj, proj, proj, proj, proj)

    hpt = TM // CONV_HALO
    n_halo = t_all // CONV_HALO

    def tcol(j):
        return pl.BlockSpec((1, TM, 512), lambda b, i: (b, i, j))

    def halo(j, nxt):
        if nxt:
            return pl.BlockSpec((1, CONV_HALO, 512), lambda b, i: (b, jnp.minimum((i + 1) * hpt, n_halo - 1), j))
        return pl.BlockSpec((1, CONV_HALO, 512), lambda b, i: (b, jnp.maximum(i * hpt - 1, 0), j))

    xs_new, hrow, aff_t = pl.pallas_call(
        functools.partial(_even_out_kernel, n_lat_tiles=nlt),
        grid=(bsz, nt),
        in_specs=[x_spec, m_spec, pl.BlockSpec((1, TM, 512), lambda b, i: (b, i, 0)),
                  pl.BlockSpec((1, TM, 512), lambda b, i: (b, i, 0)),
                  tcol(3), tcol(4), tcol(5), tcol(6), halo(5, False), halo(6, False), halo(5, True), halo(6, True),
                  _const_spec((1, 512)), _const_spec(conv_w.shape), _const_spec((1, 512)), _const_spec((1, 512)),
                  _const_spec((1, 512)), _const_spec(w_out.shape), _const_spec((1, d)), _const_spec(rwt.shape)],
        out_specs=[x_spec, pl.BlockSpec((1, TM, SUBLANES, LANES), lambda b, i: (b, i, 0, 0)),
                   pl.BlockSpec((1, TM // LANES, N_EXPERTS, LANES), lambda b, i: (b, i, 0, 0))],
        out_shape=[jax.ShapeDtypeStruct(xs.shape, F32), jax.ShapeDtypeStruct((bsz, t_all, SUBLANES, LANES), F32),
                   jax.ShapeDtypeStruct((bsz, t_all // LANES, N_EXPERTS, LANES), F32)],
        scratch_shapes=[pltpu.VMEM((TM + 2 * CONV_HALO, CONV_CH), F32),
                        pltpu.VMEM((SUBLANES - 1, TM + 2 * CONV_HALO - SUBLANES, CONV_CH), F32)],
        input_output_aliases={0: 0},
        compiler_params=_cparams(("arbitrary", "arbitrary"), VMEM_LIMIT),
        name="even_out",
    )(xs, modtab, o_f, o_b, proj, proj, proj, proj, proj, proj, proj, proj,
      gn_g, conv_w, conv_b, ln_g, ln_b, w_out, n2g, rwt)
    return xs_new, hrow, aff_t


def _odd_layer(xs, modtab, n1g, n2g, rwt, w_in_ext, qag, kvag, w_uq_ext, w_ukv_p, gq, gk, w_out, cos, sin, n_lat):
    bsz, t_all, d = xs.shape
    nt = t_all // TM
    nlt = n_lat // TM
    nh = MLA_HEADS
    x_spec, m_spec = _tile_specs(d, nlt)
    q, kt, v = pl.pallas_call(
        _odd_in_kernel,
        grid=(bsz, nt),
        in_specs=[x_spec, m_spec, _const_spec((1, d)), _const_spec(w_in_ext.shape), _const_spec(qag.shape),
                  _const_spec(kvag.shape), _const_spec(w_uq_ext.shape), _const_spec(w_ukv_p.shape),
                  _const_spec(gq.shape), _const_spec(gk.shape),
                  pl.BlockSpec((TM, 128), lambda b, i: (i, 0)), pl.BlockSpec((TM, 128), lambda b, i: (i, 0))],
        out_specs=[pl.BlockSpec((1, nh, TM, MLA_QK_PAD), lambda b, i: (b, 0, i, 0)),
                   pl.BlockSpec((1, nh, 1, MLA_QK_PAD, TM), lambda b, i: (b, 0, i, 0, 0)),
                   pl.BlockSpec((1, nh, TM, 2 * MLA_V), lambda b, i: (b, 0, i, 0))],
        out_shape=[jax.ShapeDtypeStruct((bsz, nh, t_all, MLA_QK_PAD), BF16),
                   jax.ShapeDtypeStruct((bsz, nh, nt, MLA_QK_PAD, TM), BF16),
                   jax.ShapeDtypeStruct((bsz, nh, t_all, 2 * MLA_V), BF16)],
        compiler_params=_cparams(("arbitrary", "arbitrary"), VMEM_LIMIT),
        name="odd_in",
    )(xs, modtab, n1g, w_in_ext, qag, kvag, w_uq_ext, w_ukv_p, gq, gk, cos, sin)

    tps = next(c for c in ATT_KT_PER_STEP if nt % c == 0)
    nsub = ATT_TQ // ATT_SUB
    o = pl.pallas_call(
        functools.partial(_attn_kernel, tps=tps),
        grid=(bsz, nh, n_lat // ATT_TQ),
        in_specs=[pl.BlockSpec((1, 1, ATT_TQ, MLA_QK_PAD), lambda b, h, i: (b, h, i, 0)),
                  pl.BlockSpec((1, 1, nt, MLA_QK_PAD, TM), lambda b, h, i: (b, h, 0, 0, 0)),
                  pl.BlockSpec((1, 1, t_all, 2 * MLA_V), lambda b, h, i: (b, h, 0, 0))],
        out_specs=pl.BlockSpec((1, ATT_TQ, MLA_V), lambda b, h, i: (b, i, h)),
        out_shape=jax.ShapeDtypeStruct((bsz, n_lat, nh * MLA_V), BF16),
        scratch_shapes=[pltpu.VMEM((2, ATT_GROUP, ATT_SUB, tps * TM), F32), pltpu.VMEM((nsub, ATT_SUB, LANES), F32),
                        pltpu.VMEM((nsub, ATT_SUB, 2 * MLA_V), F32)],
        compiler_params=_cparams(("arbitrary", "arbitrary", "arbitrary"), VMEM_LIMIT),
        name="mla_attention",
    )(q, kt, v)
    o_ctx = pl.pallas_call(
        _attn_ctx_kernel,
        grid=(bsz, nh),
        in_specs=[pl.BlockSpec((1, 1, TM, MLA_QK_PAD), lambda b, h: (b, h, nlt, 0)),
                  pl.BlockSpec((1, 1, 1, MLA_QK_PAD, TM), lambda b, h: (b, h, nlt, 0, 0)),
                  pl.BlockSpec((1, 1, TM, 2 * MLA_V), lambda b, h: (b, h, nlt, 0))],
        out_specs=pl.BlockSpec((1, TM, MLA_V), lambda b, h: (b, 0, h)),
        out_shape=jax.ShapeDtypeStruct((bsz, TM, nh * MLA_V), BF16),
        compiler_params=_cparams(("arbitrary", "arbitrary"), VMEM_LIMIT),
        name="mla_attention_ctx",
    )(q, kt, v)

    xs_new, hrow, aff_t = pl.pallas_call(
        functools.partial(_odd_out_kernel, n_lat_tiles=nlt),
        grid=(bsz, nt),
        in_specs=[x_spec, m_spec, pl.BlockSpec((1, TM, nh * MLA_V), lambda b, i: (b, jnp.minimum(i, nlt - 1), 0)),
                  pl.BlockSpec((1, TM, nh * MLA_V), lambda b, i: (b, 0, 0)),
                  _const_spec(w_out.shape), _const_spec((1, d)), _const_spec(rwt.shape)],
        out_specs=[x_spec, pl.BlockSpec((1, TM, SUBLANES, LANES), lambda b, i: (b, i, 0, 0)),
                   pl.BlockSpec((1, TM // LANES, N_EXPERTS, LANES), lambda b, i: (b, i, 0, 0))],
        out_shape=[jax.ShapeDtypeStruct(xs.shape, F32), jax.ShapeDtypeStruct((bsz, t_all, SUBLANES, LANES), F32),
                   jax.ShapeDtypeStruct((bsz, t_all // LANES, N_EXPERTS, LANES), F32)],
        input_output_aliases={0: 0},
        compiler_params=_cparams(("arbitrary", "arbitrary"), VMEM_LIMIT),
        name="odd_out",
    )(xs, modtab, o, o_ctx, w_out, n2g, rwt)
    return xs_new, hrow, aff_t


def _rope_tables(n_lat, n_ctx, nf):
    t = jnp.arange(n_lat)
    row = (t // GRID_W).astype(F32)
    col = (t % GRID_W).astype(F32)
    inv = ROPE_BASE ** (-(jnp.arange(nf, dtype=F32) / nf))
    ar = row[:, None] * inv[None, :]
    ac = col[:, None] * inv[None, :]
    cos = jnp.concatenate([jnp.cos(ar), jnp.cos(ar), jnp.cos(ac), jnp.cos(ac)], axis=-1)
    sin = jnp.concatenate([-jnp.sin(ar), jnp.sin(ar), -jnp.sin(ac), jnp.sin(ac)], axis=-1)
    cos = jnp.concatenate([cos, jnp.ones((n_ctx, 4 * nf), F32)], axis=0)
    sin = jnp.concatenate([sin, jnp.zeros((n_ctx, 4 * nf), F32)], axis=0)
    return cos, sin


def _swap_perm(nf):
    a = jnp.arange(nf)
    return jnp.concatenate([a + nf, a, a + 3 * nf, a + 2 * nf])


def kernel(x, c, ctx, c_ctx, mod_w, mod_b, norm1_g, norm2_g, ev_w_in, ret_decay_f, ret_decay_b, ret_gn_g, conv_w, conv_b, conv_ln_g, conv_ln_b, ev_w_out, od_w_in, mla_q_a_g, mla_kv_a_g, mla_w_uq, mla_w_ukv, mla_qk_g_q, mla_qk_g_k, od_w_out, router_w, moe_w_gate, moe_w_up, moe_w_down):
    bsz, n_lat, d = x.shape
    n_ctx = ctx.shape[1]
    depth = mod_w.shape[0]
    assert d == D_MODEL and n_ctx == TM and n_lat % ATT_TQ == 0 and bsz + 1 <= 8

    xs = jnp.concatenate([x, ctx], axis=1)
    cc = jnp.zeros((8, d), F32).at[:bsz].set(c).at[bsz].set(c_ctx)
    mod = _mod_all(cc, mod_w, mod_b)

    cos_e, sin_e = _rope_tables(n_lat, n_ctx, RET_DK // 4)
    cos_e, sin_e = jnp.tile(cos_e, (1, RET_HEADS)), jnp.tile(sin_e, (1, RET_HEADS))
    cos_o, sin_o = _rope_tables(n_lat, n_ctx, MLA_ROPE // 4)
    cos_o, sin_o = jnp.tile(cos_o, (1, 2)), jnp.tile(sin_o, (1, 2))
    perm_o = _swap_perm(MLA_ROPE // 4)

    for l in range(depth):
        i = l // 2
        modtab = jnp.stack([mod[l, :bsz], jnp.broadcast_to(mod[l, bsz], (bsz, 6 * d))], axis=1).reshape(bsz * 2, 1, 6 * d)
        n1g = norm1_g[l][None]
        n2g = norm2_g[l][None]
        rw = router_w[l]
        rw_hi = rw.astype(BF16)
        rw_lo = (rw - rw_hi.astype(F32)).astype(BF16)
        pad = jnp.zeros((d, 128 - N_EXPERTS), BF16)
        rwt = jnp.concatenate([rw_hi, pad, rw_lo, pad], axis=1)
        if l % 2 == 0:
            w_ext = ev_w_in[i].astype(BF16)
            lg = jnp.broadcast_to(jnp.stack([ret_decay_f[i], ret_decay_b[i]])[:, :, None, None], (2, RET_HEADS, 1, 128))
            cw = jnp.concatenate([conv_w[i], jnp.zeros((1, CONV_CH), F32)], axis=0)
            xs, hrow, aff_t = _even_layer(xs, modtab, n1g, n2g, rwt, w_ext, lg, ret_gn_g[i][None], cw, conv_b[i][None],
                                    conv_ln_g[i][None], conv_ln_b[i][None], ev_w_out[i].astype(BF16), cos_e, sin_e, n_lat)
        else:
            w = od_w_in[i]
            kpe = w[:, MLA_Q_LORA + MLA_KV_LORA:]
            w_ext = jnp.concatenate([w[:, :MLA_Q_LORA + MLA_KV_LORA], kpe, kpe, kpe[:, perm_o], kpe[:, perm_o]], axis=1).astype(BF16)
            wq = mla_w_uq[i].reshape(MLA_Q_LORA, MLA_HEADS, MLA_QK)
            wq_r = wq[:, :, MLA_NOPE:]
            w_uq_ext = jnp.concatenate([wq[:, :, :MLA_NOPE].reshape(MLA_Q_LORA, -1), wq_r.reshape(MLA_Q_LORA, -1),
                                        wq_r[:, :, perm_o].reshape(MLA_Q_LORA, -1)], axis=1).astype(BF16)
            wkv = mla_w_ukv[i].reshape(MLA_KV_LORA, MLA_HEADS, MLA_NOPE + MLA_V)
            w_ukv_p = jnp.concatenate([wkv[:, :, :MLA_NOPE].reshape(MLA_KV_LORA, -1),
                                       wkv[:, :, MLA_NOPE:].reshape(MLA_KV_LORA, -1)], axis=1).astype(BF16)

            def gains(g):
                gr = g[MLA_NOPE:]
                return jnp.stack([g[:MLA_NOPE], jnp.tile(gr, 2), jnp.tile(gr[perm_o], 2)])

            xs, hrow, aff_t = _odd_layer(xs, modtab, n1g, n2g, rwt, w_ext, mla_q_a_g[i][None], mla_kv_a_g[i][None], w_uq_ext,
                                   w_ukv_p, gains(mla_qk_g_q[i]), gains(mla_qk_g_k[i]), od_w_out[i].astype(BF16),
                                   cos_o, sin_o, n_lat)
        xs = _moe(xs, hrow, aff_t, modtab, l, moe_w_gate, moe_w_up, moe_w_down, n_lat, latent_only=l == depth - 1)
    return xs
```

```python
import functools

import jax
import jax.numpy as jnp
from jax import lax
from jax.experimental import pallas as pl
from jax.experimental.pallas import tpu as pltpu

F32 = jnp.float32
BF16 = jnp.bfloat16
I32 = jnp.int32

D_MODEL = 1024
GRID_W = 64
RET_HEADS = 4
RET_DK = 128
RET_CHUNK = 256
CONV_CH = 512
CONV_WIDTH = 31
CONV_HALO = 16
MLA_HEADS = 8
MLA_Q_LORA = 512
MLA_KV_LORA = 256
MLA_NOPE = 128
MLA_ROPE = 64
MLA_V = 128
MLA_QK = MLA_NOPE + MLA_ROPE
MLA_QK_PAD = 256
N_EXPERTS = 16
EC_CAPACITY_FACTOR = 2
EXPERT_FF = 1024
ROPE_BASE = 10000.0
LOG2E = 1.4426950408889634
EPS = 1e-6

LANES = 128
SUBLANES = 8
TM = 256
ATT_TQ = 2048
ATT_SUB = 512
ATT_GROUP = 1
ATT_UNROLL = 4
ATT_KT_PER_STEP = (3, 2, 1)
FF_TILE = 1024
SELECT_REFINE_STEPS = 29
F32_MIN_NORMAL_BITS = 0x00800000
TILES_PER_STEP = (3, 1)
ROW_GROUP = 8
SCATTER_GROUP = 4
VMEM_LIMIT = 56 * 2 ** 20

_NT = (((1,), (1,)), ((), ()))


def _cparams(sem, vmem=None):
    return pltpu.CompilerParams(dimension_semantics=sem, vmem_limit_bytes=vmem)


def _dot(a, b):
    return jnp.dot(a, b, preferred_element_type=F32)


def _split3(a):
    a1 = a.astype(BF16)
    r = a - a1.astype(F32)
    a2 = r.astype(BF16)
    a3 = (r - a2.astype(F32)).astype(BF16)
    return a1, a2, a3


def _dot_hi(a, b, dn):
    a1, a2, a3 = _split3(a)
    b1, b2, b3 = _split3(b)
    d = lambda x, y: lax.dot_general(x, y, dn, preferred_element_type=F32)
    return ((d(a3, b1) + d(a2, b2) + d(a1, b3)) + (d(a2, b1) + d(a1, b2))) + d(a1, b1)


def _silu(a):
    return a * jax.nn.sigmoid(a)


def _rms(x, g):
    return x * lax.rsqrt(jnp.mean(x * x, axis=-1, keepdims=True) + EPS) * g


def _modulate(x, g, shift, scale):
    return _rms(x, g) * (1.0 + scale) + shift


def _mod_kernel(c_ref, w_ref, b_ref, o_ref):
    s = _silu(c_ref[...])
    o_ref[0] = _dot_hi(s, w_ref[0], (((1,), (0,)), ((), ()))) + b_ref[0]


def _mod_all(cc, mod_w, mod_b):
    depth, d, n6 = mod_w.shape
    tn = 1536
    return pl.pallas_call(
        _mod_kernel,
        grid=(depth, n6 // tn),
        in_specs=[pl.BlockSpec((8, d), lambda l, j: (0, 0)),
                  pl.BlockSpec((1, d, tn), lambda l, j: (l, 0, j)),
                  pl.BlockSpec((1, 1, tn), lambda l, j: (l, 0, j))],
        out_specs=pl.BlockSpec((1, 8, tn), lambda l, j: (l, 0, j)),
        out_shape=jax.ShapeDtypeStruct((depth, 8, n6), F32),
        compiler_params=_cparams(("arbitrary", "arbitrary"), VMEM_LIMIT),
        name="mod_all",
    )(cc, mod_w, mod_b.reshape(depth, 1, n6))


def _epilogue(x, out, m, n2g, rwt, xo_ref, hrow_ref, aff_ref, j=0):
    d = D_MODEL
    rs = slice(j * TM, (j + 1) * TM)
    xn = x + m[:, 2 * d:3 * d] * out
    xo_ref[0, rs, :] = xn
    h2 = _modulate(xn, n2g, m[:, 3 * d:4 * d], m[:, 4 * d:5 * d])
    hrow_ref[0, rs] = h2.reshape(TM, SUBLANES, LANES)
    h_hi = h2.astype(BF16)
    h_lo = (h2 - h_hi.astype(F32)).astype(BF16)
    z = _dot(h_hi, rwt)
    logits = (z[:, 0:128] + z[:, 128:256]) + _dot(h_lo, rwt[:, 0:128])
    lane = lax.broadcasted_iota(I32, logits.shape, 1)
    logits = jnp.where(lane < N_EXPERTS, logits, -jnp.inf)
    e = jnp.exp(logits - jnp.max(logits, axis=-1, keepdims=True))
    aff = e / jnp.sum(e, axis=-1, keepdims=True)
    aff_t = aff.T
    for k in range(TM // LANES):
        aff_ref[0, j * (TM // LANES) + k] = aff_t[0:N_EXPERTS, k * LANES:(k + 1) * LANES]


def _step_mod(m_ref, tile, n_lat_tiles):
    return jnp.where(tile == n_lat_tiles, m_ref[1], m_ref[0])


def _even_in_kernel(x_ref, m_ref, g_ref, w_ref, cos_ref, sin_ref, o_ref, *, n_lat_tiles, s_tiles):
    for j in range(s_tiles):
        _even_in_tile(x_ref, _step_mod(m_ref, pl.program_id(1) * s_tiles + j, n_lat_tiles), g_ref, w_ref,
                      cos_ref, sin_ref, o_ref, slice(j * TM, (j + 1) * TM))


def _even_in_tile(x_ref, m, g_ref, w_ref, cos_ref, sin_ref, o_ref, rs):
    d = D_MODEL
    h = _modulate(x_ref[0, rs, :], g_ref[...], m[:, 0:d], m[:, d:2 * d]).astype(BF16)

    def grp(j):
        return _dot(h, w_ref[:, j * 512:(j + 1) * 512])

    c = cos_ref[rs, :]
    s = sin_ref[rs, :]
    nf = RET_DK // 4
    lane = lax.broadcasted_iota(I32, (TM, 512), 1)
    first = (lane % (2 * nf)) < nf

    def partner(v):
        return jnp.where(first, pltpu.roll(v, 512 - nf, axis=1), pltpu.roll(v, nf, axis=1))

    q = grp(0)
    k = grp(1)
    o_ref[0, rs, 0:512] = (q * c + partner(q) * s).astype(BF16)
    o_ref[0, rs, 512:1024] = ((k * c + partner(k) * s) * (RET_DK ** -0.5)).astype(BF16)
    for j in range(2, 7):
        o_ref[0, rs, j * 512:(j + 1) * 512] = grp(j).astype(BF16)


def _ret_kernel(lg_ref, qf_ref, kf_ref, vf_ref, qb_ref, kb_ref, vb_ref, of_ref, ob_ref,
                st_ref, intra_ref, xi_ref, zeta_ref, dc_ref):
    L = RET_CHUNK
    nc = TM // L

    @pl.when(pl.program_id(1) == 0)
    def _():
        st_ref[...] = jnp.zeros(st_ref.shape, F32)
        ii = lax.broadcasted_iota(I32, (L, L), 0).astype(F32)
        jj = lax.broadcasted_iota(I32, (L, L), 1).astype(F32)
        pos = lax.broadcasted_iota(I32, (L, LANES), 0).astype(F32)
        for dr in range(2):
            for h in range(RET_HEADS):
                x = lg_ref[dr, h]
                lg = jnp.minimum(x, 0.0) - jnp.log(1.0 + jnp.exp(-jnp.abs(x)))
                if dr == 0:
                    diff, xpos, zpos = ii - jj, pos + 1.0, (L - 1.0) - pos
                else:
                    diff, xpos, zpos = jj - ii, L - pos, pos
                intra_ref[dr, h] = jnp.where(diff >= 0, jnp.exp(lg[:, 0:1] * jnp.maximum(diff, 0.0)), 0.0)
                xi_ref[dr, h] = jnp.exp(lg * xpos)
                zeta_ref[dr, h] = jnp.exp(lg * zpos)
                dc_ref[dr, h] = jnp.exp(lg * float(L))

    for dr, (q_ref, k_ref, v_ref, o_ref) in enumerate(((qf_ref, kf_ref, vf_ref, of_ref),
                                                      (qb_ref, kb_ref, vb_ref, ob_ref))):
        order = range(nc) if dr == 0 else range(nc - 1, -1, -1)
        for c in order:
            for h in range(RET_HEADS):
                rs = slice(c * L, (c + 1) * L)
                cs = slice(h * 128, (h + 1) * 128)
                qc = q_ref[0, rs, cs]
                kc = k_ref[0, rs, cs]
                vc = v_ref[0, rs, cs]
                st = st_ref[dr, h]
                sc = lax.dot_general(qc, kc, _NT, preferred_element_type=F32) * intra_ref[dr, h]
                o = _dot(sc.astype(BF16), vc) + _dot(qc, st.astype(BF16)) * xi_ref[dr, h]
                kz = (kc.astype(F32) * zeta_ref[dr, h]).T.astype(BF16)
                st_ref[dr, h] = st * dc_ref[dr, h] + _dot(kz, vc)
                o_ref[0, rs, cs] = o.astype(BF16)


def _even_out_kernel(x_ref, m_ref, of_ref, ob_ref, gf_ref, gb_ref, bv_ref, bg_ref,
                     pv_ref, pg_ref, nv_ref, ng_ref, gn_ref, cw_ref, cb_ref, lng_ref, lnb_ref,
                     wo_ref, n2_ref, rwt_ref, xo_ref, hrow_ref, aff_ref, u_scr, ush_scr, *, n_lat_tiles):
    i = pl.program_id(1)
    hl = CONV_HALO

    def glu(v_ref, g_ref):
        return v_ref[0].astype(F32) * jax.nn.sigmoid(g_ref[0].astype(F32))

    prev_ok = jnp.logical_and(i >= 1, i < n_lat_tiles)
    next_ok = i < n_lat_tiles - 1
    u_scr[0:hl, :] = jnp.where(prev_ok, glu(pv_ref, pg_ref), 0.0)
    u_scr[hl:hl + TM, :] = glu(bv_ref, bg_ref)
    u_scr[hl + TM:2 * hl + TM, :] = jnp.where(next_ok, glu(nv_ref, ng_ref), 0.0)
    n_sh = TM + 2 * hl - SUBLANES
    for r in range(1, SUBLANES):
        ush_scr[r - 1] = u_scr[r:r + n_sh, :]
    y = jnp.zeros((TM, CONV_CH), F32) + cb_ref[...]
    off = hl - CONV_WIDTH // 2
    for k in range(CONV_WIDTH):
        r = (k + off) % SUBLANES
        a = k + off - r
        win = u_scr[a:a + TM, :] if r == 0 else ush_scr[r - 1, a:a + TM, :]
        y = y + cw_ref[k:k + 1, :] * win
    mu = jnp.mean(y, axis=-1, keepdims=True)
    yc = y - mu
    var = jnp.mean(yc * yc, axis=-1, keepdims=True)
    conv = _silu(yc * lax.rsqrt(var + EPS) * lng_ref[...] + lnb_ref[...])

    def gnorm(o_ref, g_ref):
        parts = []
        for h in range(RET_HEADS):
            cs = slice(h * 128, (h + 1) * 128)
            o = o_ref[0, :, cs].astype(F32)
            mu_h = jnp.mean(o, axis=-1, keepdims=True)
            oc = o - mu_h
            var_h = jnp.mean(oc * oc, axis=-1, keepdims=True)
            parts.append(oc * lax.rsqrt(var_h + EPS) * gn_ref[:, cs] * _silu(g_ref[0, :, cs].astype(F32)))
        return parts

    pf = gnorm(of_ref, gf_ref)
    pb = gnorm(ob_ref, gb_ref)
    out = _dot(conv.astype(BF16), wo_ref[512:1024, :])
    for h in range(RET_HEADS):
        out = out + _dot((pf[h] + pb[h]).astype(BF16), wo_ref[h * 128:(h + 1) * 128, :])
    _epilogue(x_ref[0], out, m_ref[0], n2_ref[...], rwt_ref[...], xo_ref, hrow_ref, aff_ref)


def _odd_in_kernel(x_ref, m_ref, g_ref, win_ref, qag_ref, kvag_ref, wuq_ref, wukv_ref,
                   gq_ref, gk_ref, cos_ref, sin_ref, q_ref, kt_ref, v_ref, *, n_lat_tiles, s_tiles):
    for j in range(s_tiles):
        _odd_in_tile(x_ref, _step_mod(m_ref, pl.program_id(1) * s_tiles + j, n_lat_tiles), g_ref, win_ref, qag_ref,
                     kvag_ref, wuq_ref, wukv_ref, gq_ref, gk_ref, cos_ref, sin_ref, q_ref, kt_ref, v_ref, j)


def _odd_in_tile(x_ref, m, g_ref, win_ref, qag_ref, kvag_ref, wuq_ref, wukv_ref,
                 gq_ref, gk_ref, cos_ref, sin_ref, q_ref, kt_ref, v_ref, j):
    d = D_MODEL
    nh = MLA_HEADS
    rs = slice(j * TM, (j + 1) * TM)
    h = _modulate(x_ref[0, rs, :], g_ref[...], m[:, 0:d], m[:, d:2 * d]).astype(BF16)
    c = _dot(h, win_ref[...])
    cq = _rms(c[:, 0:512], qag_ref[...]).astype(BF16)
    ckv = _rms(c[:, 512:768], kvag_ref[...]).astype(BF16)
    kpe = c[:, 768:896]
    kpe_sw = c[:, 896:1024]
    qa = _dot(cq, wuq_ref[...])
    kva = _dot(ckv, wukv_ref[...])
    cos = cos_ref[rs, :]
    sin = sin_ref[rs, :]
    gq = gq_ref[...]
    gk = gk_ref[...]
    lane = lax.broadcasted_iota(I32, (TM, 128), 1)
    lo = lane < MLA_ROPE
    inv_d = 1.0 / MLA_QK

    ss_kpe = 0.5 * jnp.sum(kpe * kpe, axis=-1, keepdims=True)
    kr = kpe * gk[1:2] * cos + kpe_sw * gk[2:3] * sin
    for p in range(nh // 2):
        r = qa[:, 1024 + p * 128:1024 + (p + 1) * 128]
        r_sw = qa[:, 1536 + p * 128:1536 + (p + 1) * 128]
        r2 = r * r
        ss_r = (jnp.sum(jnp.where(lo, r2, 0.0), axis=-1, keepdims=True),
                jnp.sum(jnp.where(lo, 0.0, r2), axis=-1, keepdims=True))
        qr = r * gq[1:2] * cos + r_sw * gq[2:3] * sin
        for s in range(2):
            hd = 2 * p + s
            keep = lo if s == 0 else jnp.logical_not(lo)
            qn = qa[:, hd * 128:(hd + 1) * 128]
            nq = lax.rsqrt((jnp.sum(qn * qn, axis=-1, keepdims=True) + ss_r[s]) * inv_d + EPS) * (MLA_QK ** -0.5 * LOG2E)
            q_ref[0, hd, rs, 0:128] = (qn * nq * gq[0:1]).astype(BF16)
            q_ref[0, hd, rs, 128:256] = jnp.where(keep, qr * nq, 0.0).astype(BF16)
            kn = kva[:, hd * 128:(hd + 1) * 128]
            nk = lax.rsqrt((jnp.sum(kn * kn, axis=-1, keepdims=True) + ss_kpe) * inv_d + EPS)
            k_full = jnp.concatenate([kn * nk * gk[0:1], jnp.where(keep, kr * nk, 0.0)], axis=-1)
            kt_ref[0, hd, j] = k_full.T.astype(BF16)
            v_ref[0, hd, rs, 0:MLA_V] = kva[:, 1024 + hd * 128:1024 + (hd + 1) * 128].astype(BF16)
            v_ref[0, hd, rs, MLA_V:2 * MLA_V] = jnp.ones((TM, MLA_V), BF16)


def _attn_kernel(q_ref, kt_ref, v_ref, o_ref, s_scr, m_scr, acc_scr, *, tps):
    tq = q_ref.shape[2]
    nsub = tq // ATT_SUB
    kw = kt_ref.shape[4]
    n_steps = kt_ref.shape[2] // tps
    n_units = (nsub // ATT_GROUP) * n_steps

    def split(u):
        g = u // n_steps
        return g, u - g * n_steps

    def qk(u, buf):
        g, j = split(u)
        for k in range(ATT_GROUP):
            q = q_ref[0, 0, pl.ds(pl.multiple_of((g * ATT_GROUP + k) * ATT_SUB, ATT_SUB), ATT_SUB), :]
            for c in range(tps):
                s_scr[buf, k, :, c * kw:(c + 1) * kw] = _dot(q, kt_ref[0, 0, j * tps + c])

    def smpv(u, buf):
        g, j = split(u)
        vj = v_ref[0, 0, pl.ds(pl.multiple_of(j * (tps * kw), tps * kw), tps * kw), :]
        for k in range(ATT_GROUP):
            s = g * ATT_GROUP + k
            sv = s_scr[buf, k]
            m = m_scr[s]
            tiles = [sv[:, c * LANES:(c + 1) * LANES] for c in range(sv.shape[1] // LANES)]
            mx = functools.reduce(jnp.maximum, tiles)
            m_new = jnp.maximum(m, jnp.broadcast_to(jnp.max(mx, axis=-1, keepdims=True), m.shape))
            alpha = jnp.exp2(m - m_new)
            p = jnp.concatenate([jnp.exp2(t - m_new).astype(BF16) for t in tiles], axis=-1)
            acc = acc_scr[s]
            pv = _dot(p, vj)
            acc_scr[s] = jnp.concatenate([alpha * acc[:, 0:MLA_V] + pv[:, 0:MLA_V],
                                          alpha * acc[:, MLA_V:2 * MLA_V] + pv[:, MLA_V:2 * MLA_V]], axis=-1)
            m_scr[s] = m_new

    m_scr[...] = jnp.full(m_scr.shape, -jnp.inf, F32)
    acc_scr[...] = jnp.zeros(acc_scr.shape, F32)
    qk(0, 0)

    def body(i, carry):
        for r in range(ATT_UNROLL):
            u = ATT_UNROLL * i + r
            qk(u + 1, (r + 1) % 2)
            smpv(u, r % 2)
        return carry

    n_loop = (n_units - 1) // ATT_UNROLL
    lax.fori_loop(0, n_loop, body, 0)
    for u in range(ATT_UNROLL * n_loop, n_units):
        if u + 1 < n_units:
            qk(u + 1, (u + 1) % 2)
        smpv(u, u % 2)
    for s in range(nsub):
        acc = acc_scr[s]
        o_ref[0, s * ATT_SUB:(s + 1) * ATT_SUB, :] = (acc[:, 0:MLA_V] / acc[:, MLA_V:2 * MLA_V]).astype(BF16)


def _attn_ctx_kernel(q_ref, kt_ref, v_ref, o_ref):
    s = _dot(q_ref[0, 0], kt_ref[0, 0, 0])
    p = jnp.exp2(s - jnp.max(s, axis=-1, keepdims=True))
    acc = _dot(p.astype(BF16), v_ref[0, 0])
    o_ref[0] = (acc[:, 0:MLA_V] / acc[:, MLA_V:2 * MLA_V]).astype(BF16)


def _odd_out_kernel(*refs, n_lat_tiles, s_tiles):
    x_ref, m_ref = refs[0:2]
    o_refs = refs[2:2 + s_tiles]
    oc_ref, wo_ref, n2_ref, rwt_ref, xo_ref, hrow_ref, aff_ref = refs[2 + s_tiles:]
    for j in range(s_tiles):
        tile = pl.program_id(1) * s_tiles + j
        o = jnp.where(tile == n_lat_tiles, oc_ref[0], o_refs[j][0])
        out = _dot(o, wo_ref[...])
        _epilogue(x_ref[0, j * TM:(j + 1) * TM, :], out, _step_mod(m_ref, tile, n_lat_tiles), n2_ref[...], rwt_ref[...],
                  xo_ref, hrow_ref, aff_ref, j)


def _ind(c):
    return jnp.where(c, 1.0, 0.0)


def _select_kernel(a_ref, idx_ref, gate_ref, sel_scr, l_scr, a_scr, *, segs):
    ne = N_EXPERTS
    ri = lax.broadcasted_iota(I32, (LANES, LANES), 0)
    ci = lax.broadcasted_iota(I32, (LANES, LANES), 1)
    upper = _ind(ri <= ci).astype(BF16)
    ones8 = jnp.ones((8, LANES), BF16)
    ones_sq = jnp.ones((LANES, LANES), BF16)
    for k0, nt, cap, c0 in segs:
        a = a_ref[0, k0:k0 + nt]
        def count_ge(v, a=a):
            return jnp.sum(jnp.sum(_ind(a >= v[None]), axis=0), axis=-1, keepdims=True)

        def sbody(i, prefix, cap=cap):
            cand = prefix | jnp.left_shift(jnp.int32(1), 30 - i)
            return jnp.where(count_ge(lax.bitcast_convert_type(cand, F32)) >= cap, cand, prefix)

        tbits = lax.fori_loop(0, 31, sbody, jnp.zeros((ne, 1), I32))

        def rbody(i, lohi, cap=cap):
            lo, hi = lohi
            mid = lo + (hi - lo) * 0.5
            ok = count_ge(mid) >= cap
            return jnp.where(ok, mid, lo), jnp.where(ok, hi, mid)

        hbits = jnp.maximum(tbits + 1, F32_MIN_NORMAL_BITS)
        lo, hi = lax.fori_loop(0, SELECT_REFINE_STEPS, rbody,
                               (lax.bitcast_convert_type(tbits, F32), lax.bitcast_convert_type(hbits, F32)))
        gt = a >= hi[None]
        eqf = _ind(a >= lo[None]) - _ind(gt)
        need = cap - jnp.sum(jnp.sum(_ind(gt), axis=0), axis=-1, keepdims=True)
        leq = _dot(eqf.reshape(nt * ne, LANES).astype(BF16), upper).reshape(nt, ne, LANES)
        carry = jnp.zeros((ne, 1), F32)
        sel = []
        for k in range(nt):
            rank = carry + leq[k] - eqf[k]
            sel.append(jnp.where(gt[k], 1.0, jnp.where(rank < need, eqf[k], 0.0)))
            carry = carry + leq[k][:, LANES - 1:LANES]
        sel = jnp.stack(sel)
        def put(scr, val, nt=nt):
            scr[:, 0:nt, :] = jnp.transpose(val, (1, 0, 2))
            scr[:, nt:LANES, :] = jnp.zeros((ne, LANES - nt, LANES), F32)

        put(sel_scr, sel)
        put(l_scr, _dot(sel.reshape(nt * ne, LANES).astype(BF16), upper).reshape(nt, ne, LANES))
        a_hi = a.astype(BF16).astype(F32)
        a_mid = (a - a_hi).astype(BF16).astype(F32)
        put(a_scr.at[0], a_hi)
        put(a_scr.at[1], a_mid)
        put(a_scr.at[2], a - a_hi - a_mid)

        rows = -(-cap // LANES) * LANES
        slot = lax.broadcasted_iota(I32, (rows, LANES), 0).astype(F32)
        lane = lax.broadcasted_iota(I32, (rows, LANES), 1)
        lane_f = lane.astype(F32)
        idx_all = jnp.zeros((rows, LANES), F32)
        gate_all = jnp.zeros((rows, LANES), F32)
        for e in range(ne):
            sel_e = sel_scr[e].astype(BF16)
            tot = lax.dot_general(ones8, sel_e, _NT, preferred_element_type=F32)
            pb = _dot(tot.astype(BF16), upper)[0:1]
            le = pb <= slot
            kc = _dot(_ind(le).astype(BF16), ones_sq)
            base = _dot(jnp.where(le, tot[0:1], 0.0).astype(BF16), ones_sq)
            onehot = _ind(lane_f == kc).astype(BF16)
            rhs = jnp.concatenate([l_scr[e], a_scr[0, e], a_scr[1, e], a_scr[2, e]], axis=-1).astype(BF16)
            g = _dot(onehot, rhs)
            off = jnp.sum(_ind(g[:, 0:LANES] <= slot - base), axis=-1, keepdims=True)
            aff = (g[:, LANES:2 * LANES] + g[:, 2 * LANES:3 * LANES]) + g[:, 3 * LANES:4 * LANES]
            gate = jnp.sum(jnp.where(lane_f == off, aff, 0.0), axis=-1, keepdims=True)
            idx_all = jnp.where(lane == e, (kc + k0) * LANES + off, idx_all)
            gate_all = jnp.where(lane == e, gate, gate_all)
        idx_ref[0, :, c0:c0 + cap] = idx_all.T[0:ne, 0:cap].astype(I32)
        gate_ref[0, :, c0:c0 + cap] = gate_all.T[0:ne, 0:cap]


def _gather_kernel(idx_ref, h_ref, o_ref, rows):
    c_tot = rows.shape[0]

    def body(i, carry):
        base = pl.multiple_of(i * ROW_GROUP, ROW_GROUP)
        for k in range(ROW_GROUP):
            rows[base + k] = h_ref[0, idx_ref[0, 0, base + k]]
        return carry

    lax.fori_loop(0, c_tot // ROW_GROUP, body, 0)
    o_ref[0, 0] = rows[...].reshape(c_tot, D_MODEL).astype(BF16)


def _ffn_kernel(x_ref, wg_ref, wu_ref, wd_ref, o_ref):
    f = pl.program_id(2)
    x = x_ref[0, 0]
    a = _dot(x, wg_ref[0, 0].astype(BF16))
    u = _dot(x, wu_ref[0, 0].astype(BF16))
    y = _dot((_silu(a) * u).astype(BF16), wd_ref[0, 0].astype(BF16))

    @pl.when(f == 0)
    def _():
        o_ref[0, 0] = y

    @pl.when(f > 0)
    def _():
        o_ref[0, 0] += y


def _combine_kernel(idx_ref, gate_ref, x_hbm, y_ref, m_ref, o_hbm, acc, sem, *, c_lat):
    d = D_MODEL
    b = pl.program_id(0)
    e = pl.program_id(1)
    c_tot = y_ref.shape[2]

    @pl.when(e == 0)
    def _():
        cp = pltpu.make_async_copy(x_hbm.at[b], acc, sem)
        cp.start()
        cp.wait()

    for lo_, hi_, mrow in ((0, c_lat, 0), (c_lat, c_tot, 1)):
        g2 = m_ref[mrow][:, 5 * d:6 * d]

        def body(i, carry, g2=g2, lo_=lo_):
            base = pl.multiple_of(lo_ + i * SCATTER_GROUP, SCATTER_GROUP)
            ts = [idx_ref[0, 0, base + k] for k in range(SCATTER_GROUP)]
            new = [acc[pl.ds(ts[k], 1), :] + y_ref[0, 0, pl.ds(base + k, 1), :] * (gate_ref[0, 0, base + k] * g2)
                   for k in range(SCATTER_GROUP)]
            for k in range(SCATTER_GROUP):
                acc[pl.ds(ts[k], 1), :] = new[k]
            return carry

        lax.fori_loop(0, (hi_ - lo_) // SCATTER_GROUP, body, 0)

    @pl.when(e == pl.num_programs(1) - 1)
    def _():
        cp = pltpu.make_async_copy(acc.at[pl.ds(0, o_hbm.shape[1])], o_hbm.at[b], sem)
        cp.start()
        cp.wait()


def _moe(xs, hrow, aff_t, modtab, l, w_gate, w_up, w_down, n_lat, latent_only):
    bsz, t_all, d = xs.shape
    n_ctx = t_all - n_lat
    ne = N_EXPERTS
    c_lat = max(1, (EC_CAPACITY_FACTOR * n_lat) // ne)
    c_ctx = max(1, (EC_CAPACITY_FACTOR * n_ctx) // ne)
    c_tot = c_lat + c_ctx
    assert c_lat % ROW_GROUP == 0 and c_ctx % ROW_GROUP == 0
    assert n_lat % LANES == 0 and n_ctx % LANES == 0 and t_all // LANES <= LANES
    segs = ((0, n_lat // LANES, c_lat, 0), (n_lat // LANES, n_ctx // LANES, c_ctx, c_lat))
    idx, gates = pl.pallas_call(
        functools.partial(_select_kernel, segs=segs),
        grid=(bsz,),
        in_specs=[pl.BlockSpec((1, t_all // LANES, ne, LANES), lambda b: (b, 0, 0, 0))],
        out_specs=[pl.BlockSpec((1, ne, c_tot), lambda b: (b, 0, 0)), pl.BlockSpec((1, ne, c_tot), lambda b: (b, 0, 0))],
        out_shape=[jax.ShapeDtypeStruct((bsz, ne, c_tot), I32), jax.ShapeDtypeStruct((bsz, ne, c_tot), F32)],
        scratch_shapes=[pltpu.VMEM((ne, LANES, LANES), F32), pltpu.VMEM((ne, LANES, LANES), F32),
                        pltpu.VMEM((3, ne, LANES, LANES), F32)],
        compiler_params=_cparams(("arbitrary",), VMEM_LIMIT),
        name="moe_select",
    )(aff_t)
    idx = idx.reshape(bsz * ne, 1, c_tot)
    gates = gates.reshape(bsz * ne, 1, c_tot)
    smem = functools.partial(pl.BlockSpec, memory_space=pltpu.SMEM)

    xg = pl.pallas_call(
        _gather_kernel,
        grid=(bsz, ne),
        in_specs=[smem((1, 1, c_tot), lambda b, e: (b * ne + e, 0, 0)),
                  pl.BlockSpec((1, t_all, SUBLANES, LANES), lambda b, e: (b, 0, 0, 0), pipeline_mode=pl.Buffered(1))],
        out_specs=pl.BlockSpec((1, 1, c_tot, d), lambda b, e: (b, e, 0, 0)),
        out_shape=jax.ShapeDtypeStruct((bsz, ne, c_tot, d), BF16),
        scratch_shapes=[pltpu.VMEM((c_tot, SUBLANES, LANES), F32)],
        compiler_params=_cparams(("arbitrary", "arbitrary"), VMEM_LIMIT),
        name="moe_gather",
    )(idx, hrow)

    nf = EXPERT_FF // FF_TILE
    y = pl.pallas_call(
        _ffn_kernel,
        grid=(bsz, ne, nf),
        in_specs=[pl.BlockSpec((1, 1, c_tot, d), lambda b, e, f: (b, e, 0, 0)),
                  pl.BlockSpec((1, 1, d, FF_TILE), lambda b, e, f: (l, e, 0, f)),
                  pl.BlockSpec((1, 1, d, FF_TILE), lambda b, e, f: (l, e, 0, f)),
                  pl.BlockSpec((1, 1, FF_TILE, d), lambda b, e, f: (l, e, f, 0))],
        out_specs=pl.BlockSpec((1, 1, c_tot, d), lambda b, e, f: (b, e, 0, 0)),
        out_shape=jax.ShapeDtypeStruct((bsz, ne, c_tot, d), F32),
        compiler_params=_cparams(("arbitrary", "arbitrary", "arbitrary"), VMEM_LIMIT),
        name="moe_ffn",
    )(xg, w_gate, w_up, w_down)

    return pl.pallas_call(
        functools.partial(_combine_kernel, c_lat=c_lat),
        grid=(bsz, ne),
        in_specs=[smem((1, 1, c_tot), lambda b, e: (b * ne + e, 0, 0)),
                  smem((1, 1, c_tot), lambda b, e: (b * ne + e, 0, 0)),
                  pl.BlockSpec(memory_space=pl.ANY),
                  pl.BlockSpec((1, 1, c_tot, d), lambda b, e: (b, e, 0, 0)),
                  pl.BlockSpec((2, 1, 6 * d), lambda b, e: (b, 0, 0))],
        out_specs=pl.BlockSpec(memory_space=pl.ANY),
        out_shape=jax.ShapeDtypeStruct((bsz, n_lat if latent_only else t_all, d), F32),
        scratch_shapes=[pltpu.VMEM((t_all, d), F32), pltpu.SemaphoreType.DMA(())],
        input_output_aliases={} if latent_only else {2: 0},
        compiler_params=_cparams(("arbitrary", "arbitrary"), VMEM_LIMIT),
        name="moe_combine",
    )(idx, gates, xs, y, modtab)


def _tile_specs(d, n_lat_tiles):
    x_spec = pl.BlockSpec((1, TM, d), lambda b, i: (b, i, 0))
    m_spec = pl.BlockSpec((1, 1, 6 * d), lambda b, i: (2 * b + i // n_lat_tiles, 0, 0))
    return x_spec, m_spec


def _step_specs(d, s_tiles):
    x_spec = pl.BlockSpec((1, s_tiles * TM, d), lambda b, i: (b, i, 0))
    m_spec = pl.BlockSpec((2, 1, 6 * d), lambda b, i: (b, 0, 0))
    return x_spec, m_spec


def _tiles_per_step(nt):
    return next(s for s in TILES_PER_STEP if nt % s == 0)


def _const_spec(shape):
    return pl.BlockSpec(shape, lambda b, i: (0,) * len(shape))


def _even_layer(xs, modtab, n1g, n2g, rwt, w_in_ext, lg, gn_g, conv_w, conv_b, ln_g, ln_b, w_out, cos, sin, n_lat):
    bsz, t_all, d = xs.shape
    nt = t_all // TM
    nlt = n_lat // TM
    x_spec, m_spec = _tile_specs(d, nlt)
    n_cols = 7 * 512
    st = _tiles_per_step(nt)
    xs_spec, ms_spec = _step_specs(d, st)
    proj = pl.pallas_call(
        functools.partial(_even_in_kernel, n_lat_tiles=nlt, s_tiles=st),
        grid=(bsz, nt // st),
        in_specs=[xs_spec, ms_spec, _const_spec((1, d)), _const_spec(w_in_ext.shape),
                  pl.BlockSpec((st * TM, 512), lambda b, i: (i, 0)), pl.BlockSpec((st * TM, 512), lambda b, i: (i, 0))],
        out_specs=pl.BlockSpec((1, st * TM, n_cols), lambda b, i: (b, i, 0)),
        out_shape=jax.ShapeDtypeStruct((bsz, t_all, n_cols), BF16),
        compiler_params=_cparams(("arbitrary", "arbitrary"), VMEM_LIMIT),
        name="even_in",
    )(xs, modtab, n1g, w_in_ext, cos, sin)

    fwd = lambda s: jnp.where(s == 0, nlt, s - 1)
    bwd = lambda s: jnp.where(s == 0, nlt, nlt - s)

    def col(order, j):
        return pl.BlockSpec((1, TM, 512), lambda b, s: (b, order(s), j))

    o_f, o_b = pl.pallas_call(
        _ret_kernel,
        grid=(bsz, nt),
        in_specs=[pl.BlockSpec(lg.shape, lambda b, s: (0, 0, 0, 0)),
                  col(fwd, 0), col(fwd, 1), col(fwd, 2), col(bwd, 0), col(bwd, 1), col(bwd, 2)],
        out_specs=[col(fwd, 0), col(bwd, 0)],
        out_shape=[jax.ShapeDtypeStruct((bsz, t_all, 512), BF16)] * 2,
        scratch_shapes=[pltpu.VMEM((2, RET_HEADS, RET_DK, 128), F32),
                        pltpu.VMEM((2, RET_HEADS, RET_CHUNK, RET_CHUNK), F32),
                        pltpu.VMEM((2, RET_HEADS, RET_CHUNK, 128), F32),
                        pltpu.VMEM((2, RET_HEADS, RET_CHUNK, 128), F32),
                        pltpu.VMEM((2, RET_HEADS, 1, 128), F32)],
        compiler_params=_cparams(("arbitrary", "arbitrary"), VMEM_LIMIT),
        name="retention",
    )(lg, proj, proj, proj, proj, proj, proj)

    hpt = TM // CONV_HALO
    n_halo = t_all // CONV_HALO

    def tcol(j):
        return pl.BlockSpec((1, TM, 512), lambda b, i: (b, i, j))

    def halo(j, nxt):
        if nxt:
            return pl.BlockSpec((1, CONV_HALO, 512), lambda b, i: (b, jnp.minimum((i + 1) * hpt, n_halo - 1), j))
        return pl.BlockSpec((1, CONV_HALO, 512), lambda b, i: (b, jnp.maximum(i * hpt - 1, 0), j))

    xs_new, hrow, aff_t = pl.pallas_call(
        functools.partial(_even_out_kernel, n_lat_tiles=nlt),
        grid=(bsz, nt),
        in_specs=[x_spec, m_spec, pl.BlockSpec((1, TM, 512), lambda b, i: (b, i, 0)),
                  pl.BlockSpec((1, TM, 512), lambda b, i: (b, i, 0)),
                  tcol(3), tcol(4), tcol(5), tcol(6), halo(5, False), halo(6, False), halo(5, True), halo(6, True),
                  _const_spec((1, 512)), _const_spec(conv_w.shape), _const_spec((1, 512)), _const_spec((1, 512)),
                  _const_spec((1, 512)), _const_spec(w_out.shape), _const_spec((1, d)), _const_spec(rwt.shape)],
        out_specs=[x_spec, pl.BlockSpec((1, TM, SUBLANES, LANES), lambda b, i: (b, i, 0, 0)),
                   pl.BlockSpec((1, TM // LANES, N_EXPERTS, LANES), lambda b, i: (b, i, 0, 0))],
        out_shape=[jax.ShapeDtypeStruct(xs.shape, F32), jax.ShapeDtypeStruct((bsz, t_all, SUBLANES, LANES), F32),
                   jax.ShapeDtypeStruct((bsz, t_all // LANES, N_EXPERTS, LANES), F32)],
        scratch_shapes=[pltpu.VMEM((TM + 2 * CONV_HALO, CONV_CH), F32),
                        pltpu.VMEM((SUBLANES - 1, TM + 2 * CONV_HALO - SUBLANES, CONV_CH), F32)],
        input_output_aliases={0: 0},
        compiler_params=_cparams(("arbitrary", "arbitrary"), VMEM_LIMIT),
        name="even_out",
    )(xs, modtab, o_f, o_b, proj, proj, proj, proj, proj, proj, proj, proj,
      gn_g, conv_w, conv_b, ln_g, ln_b, w_out, n2g, rwt)
    return xs_new, hrow, aff_t


def _odd_layer(xs, modtab, n1g, n2g, rwt, w_in_ext, qag, kvag, w_uq_ext, w_ukv_p, gq, gk, w_out, cos, sin, n_lat):
    bsz, t_all, d = xs.shape
    nt = t_all // TM
    nlt = n_lat // TM
    nh = MLA_HEADS
    st = _tiles_per_step(nt)
    xs_spec, ms_spec = _step_specs(d, st)
    q, kt, v = pl.pallas_call(
        functools.partial(_odd_in_kernel, n_lat_tiles=nlt, s_tiles=st),
        grid=(bsz, nt // st),
        in_specs=[xs_spec, ms_spec, _const_spec((1, d)), _const_spec(w_in_ext.shape), _const_spec(qag.shape),
                  _const_spec(kvag.shape), _const_spec(w_uq_ext.shape), _const_spec(w_ukv_p.shape),
                  _const_spec(gq.shape), _const_spec(gk.shape),
                  pl.BlockSpec((st * TM, 128), lambda b, i: (i, 0)), pl.BlockSpec((st * TM, 128), lambda b, i: (i, 0))],
        out_specs=[pl.BlockSpec((1, nh, st * TM, MLA_QK_PAD), lambda b, i: (b, 0, i, 0)),
                   pl.BlockSpec((1, nh, st, MLA_QK_PAD, TM), lambda b, i: (b, 0, i, 0, 0)),
                   pl.BlockSpec((1, nh, st * TM, 2 * MLA_V), lambda b, i: (b, 0, i, 0))],
        out_shape=[jax.ShapeDtypeStruct((bsz, nh, t_all, MLA_QK_PAD), BF16),
                   jax.ShapeDtypeStruct((bsz, nh, nt, MLA_QK_PAD, TM), BF16),
                   jax.ShapeDtypeStruct((bsz, nh, t_all, 2 * MLA_V), BF16)],
        compiler_params=_cparams(("arbitrary", "arbitrary"), VMEM_LIMIT),
        name="odd_in",
    )(xs, modtab, n1g, w_in_ext, qag, kvag, w_uq_ext, w_ukv_p, gq, gk, cos, sin)

    tps = next(c for c in ATT_KT_PER_STEP if nt % c == 0)
    nsub = ATT_TQ // ATT_SUB
    o = pl.pallas_call(
        functools.partial(_attn_kernel, tps=tps),
        grid=(bsz, nh, n_lat // ATT_TQ),
        in_specs=[pl.BlockSpec((1, 1, ATT_TQ, MLA_QK_PAD), lambda b, h, i: (b, h, i, 0)),
                  pl.BlockSpec((1, 1, nt, MLA_QK_PAD, TM), lambda b, h, i: (b, h, 0, 0, 0)),
                  pl.BlockSpec((1, 1, t_all, 2 * MLA_V), lambda b, h, i: (b, h, 0, 0))],
        out_specs=pl.BlockSpec((1, ATT_TQ, MLA_V), lambda b, h, i: (b, i, h)),
        out_shape=jax.ShapeDtypeStruct((bsz, n_lat, nh * MLA_V), BF16),
        scratch_shapes=[pltpu.VMEM((2, ATT_GROUP, ATT_SUB, tps * TM), F32), pltpu.VMEM((nsub, ATT_SUB, LANES), F32),
                        pltpu.VMEM((nsub, ATT_SUB, 2 * MLA_V), F32)],
        compiler_params=_cparams(("arbitrary", "arbitrary", "arbitrary"), VMEM_LIMIT),
        name="mla_attention",
    )(q, kt, v)
    o_ctx = pl.pallas_call(
        _attn_ctx_kernel,
        grid=(bsz, nh),
        in_specs=[pl.BlockSpec((1, 1, TM, MLA_QK_PAD), lambda b, h: (b, h, nlt, 0)),
                  pl.BlockSpec((1, 1, 1, MLA_QK_PAD, TM), lambda b, h: (b, h, nlt, 0, 0)),
                  pl.BlockSpec((1, 1, TM, 2 * MLA_V), lambda b, h: (b, h, nlt, 0))],
        out_specs=pl.BlockSpec((1, TM, MLA_V), lambda b, h: (b, 0, h)),
        out_shape=jax.ShapeDtypeStruct((bsz, TM, nh * MLA_V), BF16),
        compiler_params=_cparams(("arbitrary", "arbitrary"), VMEM_LIMIT),
        name="mla_attention_ctx",
    )(q, kt, v)

    o_specs = [pl.BlockSpec((1, TM, nh * MLA_V), lambda b, i, j=j: (b, jnp.minimum(i * st + j, nlt - 1), 0))
               for j in range(st)]
    xs_new, hrow, aff_t = pl.pallas_call(
        functools.partial(_odd_out_kernel, n_lat_tiles=nlt, s_tiles=st),
        grid=(bsz, nt // st),
        in_specs=[xs_spec, ms_spec, *o_specs, pl.BlockSpec((1, TM, nh * MLA_V), lambda b, i: (b, 0, 0)),
                  _const_spec(w_out.shape), _const_spec((1, d)), _const_spec(rwt.shape)],
        out_specs=[xs_spec, pl.BlockSpec((1, st * TM, SUBLANES, LANES), lambda b, i: (b, i, 0, 0)),
                   pl.BlockSpec((1, st * TM // LANES, N_EXPERTS, LANES), lambda b, i: (b, i, 0, 0))],
        out_shape=[jax.ShapeDtypeStruct(xs.shape, F32), jax.ShapeDtypeStruct((bsz, t_all, SUBLANES, LANES), F32),
                   jax.ShapeDtypeStruct((bsz, t_all // LANES, N_EXPERTS, LANES), F32)],
        input_output_aliases={0: 0},
        compiler_params=_cparams(("arbitrary", "arbitrary"), VMEM_LIMIT),
        name="odd_out",
    )(xs, modtab, *([o] * st), o_ctx, w_out, n2g, rwt)
    return xs_new, hrow, aff_t


def _rope_tables(n_lat, n_ctx, nf):
    t = jnp.arange(n_lat)
    row = (t // GRID_W).astype(F32)
    col = (t % GRID_W).astype(F32)
    inv = ROPE_BASE ** (-(jnp.arange(nf, dtype=F32) / nf))
    ar = row[:, None] * inv[None, :]
    ac = col[:, None] * inv[None, :]
    cos = jnp.concatenate([jnp.cos(ar), jnp.cos(ar), jnp.cos(ac), jnp.cos(ac)], axis=-1)
    sin = jnp.concatenate([-jnp.sin(ar), jnp.sin(ar), -jnp.sin(ac), jnp.sin(ac)], axis=-1)
    cos = jnp.concatenate([cos, jnp.ones((n_ctx, 4 * nf), F32)], axis=0)
    sin = jnp.concatenate([sin, jnp.zeros((n_ctx, 4 * nf), F32)], axis=0)
    return cos, sin


def _swap_perm(nf):
    a = jnp.arange(nf)
    return jnp.concatenate([a + nf, a, a + 3 * nf, a + 2 * nf])


def kernel(x, c, ctx, c_ctx, mod_w, mod_b, norm1_g, norm2_g, ev_w_in, ret_decay_f, ret_decay_b, ret_gn_g, conv_w, conv_b, conv_ln_g, conv_ln_b, ev_w_out, od_w_in, mla_q_a_g, mla_kv_a_g, mla_w_uq, mla_w_ukv, mla_qk_g_q, mla_qk_g_k, od_w_out, router_w, moe_w_gate, moe_w_up, moe_w_down):
    bsz, n_lat, d = x.shape
    n_ctx = ctx.shape[1]
    depth = mod_w.shape[0]
    assert d == D_MODEL and n_ctx == TM and n_lat % ATT_TQ == 0 and bsz + 1 <= 8

    xs = jnp.concatenate([x, ctx], axis=1)
    cc = jnp.zeros((8, d), F32).at[:bsz].set(c).at[bsz].set(c_ctx)
    mod = _mod_all(cc, mod_w, mod_b)

    cos_e, sin_e = _rope_tables(n_lat, n_ctx, RET_DK // 4)
    cos_e, sin_e = jnp.tile(cos_e, (1, RET_HEADS)), jnp.tile(sin_e, (1, RET_HEADS))
    cos_o, sin_o = _rope_tables(n_lat, n_ctx, MLA_ROPE // 4)
    cos_o, sin_o = jnp.tile(cos_o, (1, 2)), jnp.tile(sin_o, (1, 2))
    perm_o = _swap_perm(MLA_ROPE // 4)

    for l in range(depth):
        i = l // 2
        modtab = jnp.stack([mod[l, :bsz], jnp.broadcast_to(mod[l, bsz], (bsz, 6 * d))], axis=1).reshape(bsz * 2, 1, 6 * d)
        n1g = norm1_g[l][None]
        n2g = norm2_g[l][None]
        rw = router_w[l]
        rw_hi = rw.astype(BF16)
        rw_lo = (rw - rw_hi.astype(F32)).astype(BF16)
        pad = jnp.zeros((d, 128 - N_EXPERTS), BF16)
        rwt = jnp.concatenate([rw_hi, pad, rw_lo, pad], axis=1)
        if l % 2 == 0:
            w_ext = ev_w_in[i].astype(BF16)
            lg = jnp.broadcast_to(jnp.stack([ret_decay_f[i], ret_decay_b[i]])[:, :, None, None], (2, RET_HEADS, 1, 128))
            cw = jnp.concatenate([conv_w[i], jnp.zeros((1, CONV_CH), F32)], axis=0)
            xs, hrow, aff_t = _even_layer(xs, modtab, n1g, n2g, rwt, w_ext, lg, ret_gn_g[i][None], cw, conv_b[i][None],
                                    conv_ln_g[i][None], conv_ln_b[i][None], ev_w_out[i].astype(BF16), cos_e, sin_e, n_lat)
        else:
            w = od_w_in[i]
            kpe = w[:, MLA_Q_LORA + MLA_KV_LORA:]
            w_ext = jnp.concatenate([w[:, :MLA_Q_LORA + MLA_KV_LORA], kpe, kpe, kpe[:, perm_o], kpe[:, perm_o]], axis=1).astype(BF16)
            wq = mla_w_uq[i].reshape(MLA_Q_LORA, MLA_HEADS, MLA_QK)
            wq_r = wq[:, :, MLA_NOPE:]
            w_uq_ext = jnp.concatenate([wq[:, :, :MLA_NOPE].reshape(MLA_Q_LORA, -1), wq_r.reshape(MLA_Q_LORA, -1),
                                        wq_r[:, :, perm_o].reshape(MLA_Q_LORA, -1)], axis=1).astype(BF16)
            wkv = mla_w_ukv[i].reshape(MLA_KV_LORA, MLA_HEADS, MLA_NOPE + MLA_V)
            w_ukv_p = jnp.concatenate([wkv[:, :, :MLA_NOPE].reshape(MLA_KV_LORA, -1),
                                       wkv[:, :, MLA_NOPE:].reshape(MLA_KV_LORA, -1)], axis=1).astype(BF16)

            def gains(g):
                gr = g[MLA_NOPE:]
                return jnp.stack([g[:MLA_NOPE], jnp.tile(gr, 2), jnp.tile(gr[perm_o], 2)])

            xs, hrow, aff_t = _odd_layer(xs, modtab, n1g, n2g, rwt, w_ext, mla_q_a_g[i][None], mla_kv_a_g[i][None], w_uq_ext,
                                   w_ukv_p, gains(mla_qk_g_q[i]), gains(mla_qk_g_k[i]), od_w_out[i].astype(BF16),
                                   cos_o, sin_o, n_lat)
        xs = _moe(xs, hrow, aff_t, modtab, l, moe_w_gate, moe_w_up, moe_w_down, n_lat, latent_only=l == depth - 1)
    return xs
```

```python
import functools

import jax
import jax.numpy as jnp
from jax import lax
from jax.experimental import pallas as pl
from jax.experimental.pallas import tpu as pltpu

F32 = jnp.float32
BF16 = jnp.bfloat16
I32 = jnp.int32

D_MODEL = 1024
GRID_W = 64
RET_HEADS = 4
RET_DK = 128
RET_CHUNK = 256
CONV_CH = 512
CONV_WIDTH = 31
CONV_HALO = 16
MLA_HEADS = 8
MLA_Q_LORA = 512
MLA_KV_LORA = 256
MLA_NOPE = 128
MLA_ROPE = 64
MLA_V = 128
MLA_QK = MLA_NOPE + MLA_ROPE
MLA_QK_PAD = 256
N_EXPERTS = 16
EC_CAPACITY_FACTOR = 2
EXPERT_FF = 1024
ROPE_BASE = 10000.0
LOG2E = 1.4426950408889634
EPS = 1e-6

LANES = 128
SUBLANES = 8
TM = 256
ATT_TQ = 2048
ATT_SUB = 512
ATT_GROUP = 1
ATT_UNROLL = 4
ATT_KT_PER_STEP = (3, 2, 1)
FF_TILE = 1024
SELECT_REFINE_STEPS = 29
F32_MIN_NORMAL_BITS = 0x00800000
TILES_PER_STEP = (3, 1)
ROW_GROUP = 8
SCATTER_GROUP = 4
VMEM_LIMIT = 56 * 2 ** 20

_NT = (((1,), (1,)), ((), ()))


def _cparams(sem, vmem=None):
    return pltpu.CompilerParams(dimension_semantics=sem, vmem_limit_bytes=vmem)


def _dot(a, b):
    return jnp.dot(a, b, preferred_element_type=F32)


def _split3(a):
    a1 = a.astype(BF16)
    r = a - a1.astype(F32)
    a2 = r.astype(BF16)
    a3 = (r - a2.astype(F32)).astype(BF16)
    return a1, a2, a3


def _dot_hi(a, b, dn):
    a1, a2, a3 = _split3(a)
    b1, b2, b3 = _split3(b)
    d = lambda x, y: lax.dot_general(x, y, dn, preferred_element_type=F32)
    return ((d(a3, b1) + d(a2, b2) + d(a1, b3)) + (d(a2, b1) + d(a1, b2))) + d(a1, b1)


def _silu(a):
    return a * jax.nn.sigmoid(a)


def _rms(x, g):
    return x * lax.rsqrt(jnp.mean(x * x, axis=-1, keepdims=True) + EPS) * g


def _modulate(x, g, shift, scale):
    return _rms(x, g) * (1.0 + scale) + shift


def _mod_kernel(c_ref, w_ref, b_ref, o_ref):
    s = _silu(c_ref[...])
    o_ref[0] = _dot_hi(s, w_ref[0], (((1,), (0,)), ((), ()))) + b_ref[0]


def _mod_all(cc, mod_w, mod_b):
    depth, d, n6 = mod_w.shape
    tn = 1536
    return pl.pallas_call(
        _mod_kernel,
        grid=(depth, n6 // tn),
        in_specs=[pl.BlockSpec((8, d), lambda l, j: (0, 0)),
                  pl.BlockSpec((1, d, tn), lambda l, j: (l, 0, j)),
                  pl.BlockSpec((1, 1, tn), lambda l, j: (l, 0, j))],
        out_specs=pl.BlockSpec((1, 8, tn), lambda l, j: (l, 0, j)),
        out_shape=jax.ShapeDtypeStruct((depth, 8, n6), F32),
        compiler_params=_cparams(("arbitrary", "arbitrary"), VMEM_LIMIT),
        name="mod_all",
    )(cc, mod_w, mod_b.reshape(depth, 1, n6))


def _epilogue(x, out, m, n2g, rwt, xo_ref, hrow_ref, aff_ref, j=0):
    d = D_MODEL
    rs = slice(j * TM, (j + 1) * TM)
    xn = x + m[:, 2 * d:3 * d] * out
    xo_ref[0, rs, :] = xn
    h2 = _modulate(xn, n2g, m[:, 3 * d:4 * d], m[:, 4 * d:5 * d])
    hrow_ref[0, rs] = h2.reshape(TM, SUBLANES, LANES)
    h_hi = h2.astype(BF16)
    h_lo = (h2 - h_hi.astype(F32)).astype(BF16)
    z = _dot(h_hi, rwt)
    logits = (z[:, 0:128] + z[:, 128:256]) + _dot(h_lo, rwt[:, 0:128])
    lane = lax.broadcasted_iota(I32, logits.shape, 1)
    logits = jnp.where(lane < N_EXPERTS, logits, -jnp.inf)
    e = jnp.exp(logits - jnp.max(logits, axis=-1, keepdims=True))
    aff = e / jnp.sum(e, axis=-1, keepdims=True)
    aff_t = aff.T
    for k in range(TM // LANES):
        aff_ref[0, j * (TM // LANES) + k] = aff_t[0:N_EXPERTS, k * LANES:(k + 1) * LANES]


def _step_mod(m_ref, tile, n_lat_tiles):
    return jnp.where(tile == n_lat_tiles, m_ref[1], m_ref[0])


def _even_in_kernel(x_ref, m_ref, g_ref, w_ref, cos_ref, sin_ref, o_ref, *, n_lat_tiles, s_tiles):
    for j in range(s_tiles):
        _even_in_tile(x_ref, _step_mod(m_ref, pl.program_id(1) * s_tiles + j, n_lat_tiles), g_ref, w_ref,
                      cos_ref, sin_ref, o_ref, slice(j * TM, (j + 1) * TM))


def _even_in_tile(x_ref, m, g_ref, w_ref, cos_ref, sin_ref, o_ref, rs):
    d = D_MODEL
    h = _modulate(x_ref[0, rs, :], g_ref[...], m[:, 0:d], m[:, d:2 * d]).astype(BF16)

    def grp(j):
        return _dot(h, w_ref[:, j * 512:(j + 1) * 512])

    c = cos_ref[rs, :]
    s = sin_ref[rs, :]
    nf = RET_DK // 4
    lane = lax.broadcasted_iota(I32, (TM, 512), 1)
    first = (lane % (2 * nf)) < nf

    def partner(v):
        return jnp.where(first, pltpu.roll(v, 512 - nf, axis=1), pltpu.roll(v, nf, axis=1))

    q = grp(0)
    k = grp(1)
    o_ref[0, rs, 0:512] = (q * c + partner(q) * s).astype(BF16)
    o_ref[0, rs, 512:1024] = ((k * c + partner(k) * s) * (RET_DK ** -0.5)).astype(BF16)
    for j in range(2, 7):
        o_ref[0, rs, j * 512:(j + 1) * 512] = grp(j).astype(BF16)


def _ret_kernel(lg_ref, qf_ref, kf_ref, vf_ref, qb_ref, kb_ref, vb_ref, of_ref, ob_ref,
                st_ref, intra_ref, xi_ref, zeta_ref, dc_ref):
    L = RET_CHUNK
    nc = TM // L

    @pl.when(pl.program_id(1) == 0)
    def _():
        st_ref[...] = jnp.zeros(st_ref.shape, F32)
        ii = lax.broadcasted_iota(I32, (L, L), 0).astype(F32)
        jj = lax.broadcasted_iota(I32, (L, L), 1).astype(F32)
        pos = lax.broadcasted_iota(I32, (L, LANES), 0).astype(F32)
        for dr in range(2):
            for h in range(RET_HEADS):
                x = lg_ref[dr, h]
                lg = jnp.minimum(x, 0.0) - jnp.log(1.0 + jnp.exp(-jnp.abs(x)))
                if dr == 0:
                    diff, xpos, zpos = ii - jj, pos + 1.0, (L - 1.0) - pos
                else:
                    diff, xpos, zpos = jj - ii, L - pos, pos
                intra_ref[dr, h] = jnp.where(diff >= 0, jnp.exp(lg[:, 0:1] * jnp.maximum(diff, 0.0)), 0.0)
                xi_ref[dr, h] = jnp.exp(lg * xpos)
                zeta_ref[dr, h] = jnp.exp(lg * zpos)
                dc_ref[dr, h] = jnp.exp(lg * float(L))

    for dr, (q_ref, k_ref, v_ref, o_ref) in enumerate(((qf_ref, kf_ref, vf_ref, of_ref),
                                                      (qb_ref, kb_ref, vb_ref, ob_ref))):
        order = range(nc) if dr == 0 else range(nc - 1, -1, -1)
        for c in order:
            for h in range(RET_HEADS):
                rs = slice(c * L, (c + 1) * L)
                cs = slice(h * 128, (h + 1) * 128)
                qc = q_ref[0, rs, cs]
                kc = k_ref[0, rs, cs]
                vc = v_ref[0, rs, cs]
                st = st_ref[dr, h]
                sc = lax.dot_general(qc, kc, _NT, preferred_element_type=F32) * intra_ref[dr, h]
                o = _dot(sc.astype(BF16), vc) + _dot(qc, st.astype(BF16)) * xi_ref[dr, h]
                kz = (kc.astype(F32) * zeta_ref[dr, h]).T.astype(BF16)
                st_ref[dr, h] = st * dc_ref[dr, h] + _dot(kz, vc)
                o_ref[0, rs, cs] = o.astype(BF16)


def _even_out_kernel(x_ref, m_ref, of_ref, ob_ref, gf_ref, gb_ref, bv_ref, bg_ref,
                     pv_ref, pg_ref, nv_ref, ng_ref, gn_ref, cw_ref, cb_ref, lng_ref, lnb_ref,
                     wo_ref, n2_ref, rwt_ref, xo_ref, hrow_ref, aff_ref, u_scr, ush_scr, *, n_lat_tiles):
    i = pl.program_id(1)
    hl = CONV_HALO

    def glu(v_ref, g_ref):
        return v_ref[0].astype(F32) * jax.nn.sigmoid(g_ref[0].astype(F32))

    prev_ok = jnp.logical_and(i >= 1, i < n_lat_tiles)
    next_ok = i < n_lat_tiles - 1
    u_scr[0:hl, :] = jnp.where(prev_ok, glu(pv_ref, pg_ref), 0.0)
    u_scr[hl:hl + TM, :] = glu(bv_ref, bg_ref)
    u_scr[hl + TM:2 * hl + TM, :] = jnp.where(next_ok, glu(nv_ref, ng_ref), 0.0)
    n_sh = TM + 2 * hl - SUBLANES
    for r in range(1, SUBLANES):
        ush_scr[r - 1] = u_scr[r:r + n_sh, :]
    y = jnp.zeros((TM, CONV_CH), F32) + cb_ref[...]
    off = hl - CONV_WIDTH // 2
    for k in range(CONV_WIDTH):
        r = (k + off) % SUBLANES
        a = k + off - r
        win = u_scr[a:a + TM, :] if r == 0 else ush_scr[r - 1, a:a + TM, :]
        y = y + cw_ref[k:k + 1, :] * win
    mu = jnp.mean(y, axis=-1, keepdims=True)
    yc = y - mu
    var = jnp.mean(yc * yc, axis=-1, keepdims=True)
    conv = _silu(yc * lax.rsqrt(var + EPS) * lng_ref[...] + lnb_ref[...])

    def gnorm(o_ref, g_ref):
        parts = []
        for h in range(RET_HEADS):
            cs = slice(h * 128, (h + 1) * 128)
            o = o_ref[0, :, cs].astype(F32)
            mu_h = jnp.mean(o, axis=-1, keepdims=True)
            oc = o - mu_h
            var_h = jnp.mean(oc * oc, axis=-1, keepdims=True)
            parts.append(oc * lax.rsqrt(var_h + EPS) * gn_ref[:, cs] * _silu(g_ref[0, :, cs].astype(F32)))
        return parts

    pf = gnorm(of_ref, gf_ref)
    pb = gnorm(ob_ref, gb_ref)
    out = _dot(conv.astype(BF16), wo_ref[512:1024, :])
    for h in range(RET_HEADS):
        out = out + _dot((pf[h] + pb[h]).astype(BF16), wo_ref[h * 128:(h + 1) * 128, :])
    _epilogue(x_ref[0], out, m_ref[0], n2_ref[...], rwt_ref[...], xo_ref, hrow_ref, aff_ref)


def _odd_in_kernel(x_ref, m_ref, g_ref, win_ref, qag_ref, kvag_ref, wuq_ref, wukv_ref,
                   gq_ref, gk_ref, cos_ref, sin_ref, q_ref, kt_ref, v_ref, *, n_lat_tiles, s_tiles):
    for j in range(s_tiles):
        _odd_in_tile(x_ref, _step_mod(m_ref, pl.program_id(1) * s_tiles + j, n_lat_tiles), g_ref, win_ref, qag_ref,
                     kvag_ref, wuq_ref, wukv_ref, gq_ref, gk_ref, cos_ref, sin_ref, q_ref, kt_ref, v_ref, j)


def _odd_in_tile(x_ref, m, g_ref, win_ref, qag_ref, kvag_ref, wuq_ref, wukv_ref,
                 gq_ref, gk_ref, cos_ref, sin_ref, q_ref, kt_ref, v_ref, j):
    d = D_MODEL
    nh = MLA_HEADS
    rs = slice(j * TM, (j + 1) * TM)
    h = _modulate(x_ref[0, rs, :], g_ref[...], m[:, 0:d], m[:, d:2 * d]).astype(BF16)
    c = _dot(h, win_ref[...])
    cq = _rms(c[:, 0:512], qag_ref[...]).astype(BF16)
    ckv = _rms(c[:, 512:768], kvag_ref[...]).astype(BF16)
    kpe = c[:, 768:896]
    kpe_sw = c[:, 896:1024]
    qa = _dot(cq, wuq_ref[...])
    kva = _dot(ckv, wukv_ref[...])
    cos = cos_ref[rs, :]
    sin = sin_ref[rs, :]
    gq = gq_ref[...]
    gk = gk_ref[...]
    lane = lax.broadcasted_iota(I32, (TM, 128), 1)
    lo = lane < MLA_ROPE
    inv_d = 1.0 / MLA_QK

    ss_kpe = 0.5 * jnp.sum(kpe * kpe, axis=-1, keepdims=True)
    kr = kpe * gk[1:2] * cos + kpe_sw * gk[2:3] * sin
    for p in range(nh // 2):
        r = qa[:, 1024 + p * 128:1024 + (p + 1) * 128]
        r_sw = qa[:, 1536 + p * 128:1536 + (p + 1) * 128]
        r2 = r * r
        ss_r = (jnp.sum(jnp.where(lo, r2, 0.0), axis=-1, keepdims=True),
                jnp.sum(jnp.where(lo, 0.0, r2), axis=-1, keepdims=True))
        qr = r * gq[1:2] * cos + r_sw * gq[2:3] * sin
        for s in range(2):
            hd = 2 * p + s
            keep = lo if s == 0 else jnp.logical_not(lo)
            qn = qa[:, hd * 128:(hd + 1) * 128]
            nq = lax.rsqrt((jnp.sum(qn * qn, axis=-1, keepdims=True) + ss_r[s]) * inv_d + EPS) * (MLA_QK ** -0.5 * LOG2E)
            q_ref[0, hd, rs, 0:128] = (qn * nq * gq[0:1]).astype(BF16)
            q_ref[0, hd, rs, 128:256] = jnp.where(keep, qr * nq, 0.0).astype(BF16)
            kn = kva[:, hd * 128:(hd + 1) * 128]
            nk = lax.rsqrt((jnp.sum(kn * kn, axis=-1, keepdims=True) + ss_kpe) * inv_d + EPS)
            k_full = jnp.concatenate([kn * nk * gk[0:1], jnp.where(keep, kr * nk, 0.0)], axis=-1)
            kt_ref[0, hd, j] = k_full.T.astype(BF16)
            v_ref[0, hd, rs, 0:MLA_V] = kva[:, 1024 + hd * 128:1024 + (hd + 1) * 128].astype(BF16)
            v_ref[0, hd, rs, MLA_V:2 * MLA_V] = jnp.ones((TM, MLA_V), BF16)


def _attn_kernel(q_ref, kt_ref, v_ref, o_ref, s_scr, m_scr, acc_scr, *, tps):
    tq = q_ref.shape[2]
    nsub = tq // ATT_SUB
    kw = kt_ref.shape[4]
    n_steps = kt_ref.shape[2] // tps
    n_units = (nsub // ATT_GROUP) * n_steps

    def split(u):
        g = u // n_steps
        return g, u - g * n_steps

    def qk(u, buf):
        g, j = split(u)
        for k in range(ATT_GROUP):
            q = q_ref[0, 0, pl.ds(pl.multiple_of((g * ATT_GROUP + k) * ATT_SUB, ATT_SUB), ATT_SUB), :]
            for c in range(tps):
                s_scr[buf, k, :, c * kw:(c + 1) * kw] = _dot(q, kt_ref[0, 0, j * tps + c])

    def smpv(u, buf):
        g, j = split(u)
        vj = v_ref[0, 0, pl.ds(pl.multiple_of(j * (tps * kw), tps * kw), tps * kw), :]
        for k in range(ATT_GROUP):
            s = g * ATT_GROUP + k
            sv = s_scr[buf, k]
            m = m_scr[s]
            tiles = [sv[:, c * LANES:(c + 1) * LANES] for c in range(sv.shape[1] // LANES)]
            mx = functools.reduce(jnp.maximum, tiles)
            m_new = jnp.maximum(m, jnp.broadcast_to(jnp.max(mx, axis=-1, keepdims=True), m.shape))
            alpha = jnp.exp2(m - m_new)
            p = jnp.concatenate([jnp.exp2(t - m_new).astype(BF16) for t in tiles], axis=-1)
            acc = acc_scr[s]
            pv = _dot(p, vj)
            acc_scr[s] = jnp.concatenate([alpha * acc[:, 0:MLA_V] + pv[:, 0:MLA_V],
                                          alpha * acc[:, MLA_V:2 * MLA_V] + pv[:, MLA_V:2 * MLA_V]], axis=-1)
            m_scr[s] = m_new

    m_scr[...] = jnp.full(m_scr.shape, -jnp.inf, F32)
    acc_scr[...] = jnp.zeros(acc_scr.shape, F32)
    qk(0, 0)

    def body(i, carry):
        for r in range(ATT_UNROLL):
            u = ATT_UNROLL * i + r
            qk(u + 1, (r + 1) % 2)
            smpv(u, r % 2)
        return carry

    n_loop = (n_units - 1) // ATT_UNROLL
    lax.fori_loop(0, n_loop, body, 0)
    for u in range(ATT_UNROLL * n_loop, n_units):
        if u + 1 < n_units:
            qk(u + 1, (u + 1) % 2)
        smpv(u, u % 2)
    for s in range(nsub):
        acc = acc_scr[s]
        o_ref[0, s * ATT_SUB:(s + 1) * ATT_SUB, :] = (acc[:, 0:MLA_V] / acc[:, MLA_V:2 * MLA_V]).astype(BF16)


def _attn_ctx_kernel(q_ref, kt_ref, v_ref, o_ref):
    s = _dot(q_ref[0, 0], kt_ref[0, 0, 0])
    p = jnp.exp2(s - jnp.max(s, axis=-1, keepdims=True))
    acc = _dot(p.astype(BF16), v_ref[0, 0])
    o_ref[0] = (acc[:, 0:MLA_V] / acc[:, MLA_V:2 * MLA_V]).astype(BF16)


def _odd_out_kernel(*refs, n_lat_tiles, s_tiles):
    x_ref, m_ref = refs[0:2]
    o_refs = refs[2:2 + s_tiles]
    oc_ref, wo_ref, n2_ref, rwt_ref, xo_ref, hrow_ref, aff_ref = refs[2 + s_tiles:]
    for j in range(s_tiles):
        tile = pl.program_id(1) * s_tiles + j
        o = jnp.where(tile == n_lat_tiles, oc_ref[0], o_refs[j][0])
        out = _dot(o, wo_ref[...])
        _epilogue(x_ref[0, j * TM:(j + 1) * TM, :], out, _step_mod(m_ref, tile, n_lat_tiles), n2_ref[...], rwt_ref[...],
                  xo_ref, hrow_ref, aff_ref, j)


def _ind(c):
    return jnp.where(c, 1.0, 0.0)


def _select_kernel(a_ref, idx_ref, gate_ref, sel_scr, l_scr, a_scr, *, segs):
    ne = N_EXPERTS
    ri = lax.broadcasted_iota(I32, (LANES, LANES), 0)
    ci = lax.broadcasted_iota(I32, (LANES, LANES), 1)
    upper = _ind(ri <= ci).astype(BF16)
    ones8 = jnp.ones((8, LANES), BF16)
    ones_sq = jnp.ones((LANES, LANES), BF16)
    for k0, nt, cap, c0 in segs:
        a = a_ref[0, k0:k0 + nt]
        def count_ge(v, a=a):
            return jnp.sum(jnp.sum(_ind(a >= v[None]), axis=0), axis=-1, keepdims=True)

        def sbody(i, prefix, cap=cap):
            cand = prefix | jnp.left_shift(jnp.int32(1), 30 - i)
            return jnp.where(count_ge(lax.bitcast_convert_type(cand, F32)) >= cap, cand, prefix)

        tbits = lax.fori_loop(0, 31, sbody, jnp.zeros((ne, 1), I32))

        def rbody(i, lohi, cap=cap):
            lo, hi = lohi
            mid = lo + (hi - lo) * 0.5
            ok = count_ge(mid) >= cap
            return jnp.where(ok, mid, lo), jnp.where(ok, hi, mid)

        hbits = jnp.maximum(tbits + 1, F32_MIN_NORMAL_BITS)
        lo, hi = lax.fori_loop(0, SELECT_REFINE_STEPS, rbody,
                               (lax.bitcast_convert_type(tbits, F32), lax.bitcast_convert_type(hbits, F32)))
        gt = a >= hi[None]
        eqf = _ind(a >= lo[None]) - _ind(gt)
        need = cap - jnp.sum(jnp.sum(_ind(gt), axis=0), axis=-1, keepdims=True)
        leq = _dot(eqf.reshape(nt * ne, LANES).astype(BF16), upper).reshape(nt, ne, LANES)
        carry = jnp.zeros((ne, 1), F32)
        sel = []
        for k in range(nt):
            rank = carry + leq[k] - eqf[k]
            sel.append(jnp.where(gt[k], 1.0, jnp.where(rank < need, eqf[k], 0.0)))
            carry = carry + leq[k][:, LANES - 1:LANES]
        sel = jnp.stack(sel)
        def put(scr, val, nt=nt):
            scr[:, 0:nt, :] = jnp.transpose(val, (1, 0, 2))
            scr[:, nt:LANES, :] = jnp.zeros((ne, LANES - nt, LANES), F32)

        put(sel_scr, sel)
        put(l_scr, _dot(sel.reshape(nt * ne, LANES).astype(BF16), upper).reshape(nt, ne, LANES))
        a_hi = a.astype(BF16).astype(F32)
        a_mid = (a - a_hi).astype(BF16).astype(F32)
        put(a_scr.at[0], a_hi)
        put(a_scr.at[1], a_mid)
        put(a_scr.at[2], a - a_hi - a_mid)

        rows = -(-cap // LANES) * LANES
        slot = lax.broadcasted_iota(I32, (rows, LANES), 0).astype(F32)
        lane = lax.broadcasted_iota(I32, (rows, LANES), 1)
        lane_f = lane.astype(F32)
        idx_all = jnp.zeros((rows, LANES), F32)
        for e in range(ne):
            sel_e = sel_scr[e].astype(BF16)
            tot = lax.dot_general(ones8, sel_e, _NT, preferred_element_type=F32)
            pb = _dot(tot.astype(BF16), upper)[0:1]
            le = pb <= slot
            kc = _dot(_ind(le).astype(BF16), ones_sq)
            base = _dot(jnp.where(le, tot[0:1], 0.0).astype(BF16), ones_sq)
            onehot = _ind(lane_f == kc).astype(BF16)
            rhs = jnp.concatenate([l_scr[e], a_scr[0, e], a_scr[1, e], a_scr[2, e]], axis=-1).astype(BF16)
            g = _dot(onehot, rhs)
            off = jnp.sum(_ind(g[:, 0:LANES] <= slot - base), axis=-1, keepdims=True)
            aff = (g[:, LANES:2 * LANES] + g[:, 2 * LANES:3 * LANES]) + g[:, 3 * LANES:4 * LANES]
            gate = jnp.sum(jnp.where(lane_f == off, aff, 0.0), axis=-1, keepdims=True)
            idx_all = jnp.where(lane == e, (kc + k0) * LANES + off, idx_all)
            gate_ref[0, e, c0:c0 + cap, :] = jnp.broadcast_to(gate[0:cap], (cap, LANES))
        idx_ref[0, :, c0:c0 + cap] = idx_all.T[0:ne, 0:cap].astype(I32)


def _gather_kernel(idx_ref, h_ref, o_ref, rows):
    c_tot = rows.shape[0]

    def body(i, carry):
        base = pl.multiple_of(i * ROW_GROUP, ROW_GROUP)
        for k in range(ROW_GROUP):
            rows[base + k] = h_ref[0, idx_ref[0, 0, base + k]]
        return carry

    lax.fori_loop(0, c_tot // ROW_GROUP, body, 0)
    o_ref[0, 0] = rows[...].reshape(c_tot, D_MODEL).astype(BF16)


def _ffn_kernel(x_ref, gate_ref, m_ref, wg_ref, wu_ref, wd_ref, o_ref, *, c_lat):
    d = D_MODEL
    f = pl.program_id(2)
    x = x_ref[0, 0]
    c_tot = x.shape[0]
    a = _dot(x, wg_ref[0, 0].astype(BF16))
    u = _dot(x, wu_ref[0, 0].astype(BF16))
    y = _dot((_silu(a) * u).astype(BF16), wd_ref[0, 0].astype(BF16))
    gate = gate_ref[0, 0]
    for lo_, hi_, mrow in ((0, c_lat, 0), (c_lat, c_tot, 1)):
        scale = jnp.concatenate([gate[lo_:hi_]] * (d // LANES), axis=-1) * m_ref[mrow][:, 5 * d:6 * d]
        part = y[lo_:hi_] * scale

        @pl.when(f == 0)
        def _(part=part, lo_=lo_, hi_=hi_):
            o_ref[0, 0, lo_:hi_, :] = part

        @pl.when(f > 0)
        def _(part=part, lo_=lo_, hi_=hi_):
            o_ref[0, 0, lo_:hi_, :] += part


def _combine_kernel(idx_ref, x_hbm, y_ref, o_hbm, acc, sem):
    b = pl.program_id(0)
    e = pl.program_id(1)
    c_tot = y_ref.shape[2]

    @pl.when(e == 0)
    def _():
        cp = pltpu.make_async_copy(x_hbm.at[b], acc, sem)
        cp.start()
        cp.wait()

    def body(i, carry):
        base = pl.multiple_of(i * SCATTER_GROUP, SCATTER_GROUP)
        ts = [idx_ref[0, 0, base + k] for k in range(SCATTER_GROUP)]
        new = [acc[pl.ds(ts[k], 1), :] + y_ref[0, 0, pl.ds(base + k, 1), :] for k in range(SCATTER_GROUP)]
        for k in range(SCATTER_GROUP):
            acc[pl.ds(ts[k], 1), :] = new[k]
        return carry

    lax.fori_loop(0, c_tot // SCATTER_GROUP, body, 0)

    @pl.when(e == pl.num_programs(1) - 1)
    def _():
        cp = pltpu.make_async_copy(acc.at[pl.ds(0, o_hbm.shape[1])], o_hbm.at[b], sem)
        cp.start()
        cp.wait()


def _moe(xs, hrow, aff_t, modtab, l, w_gate, w_up, w_down, n_lat, latent_only):
    bsz, t_all, d = xs.shape
    n_ctx = t_all - n_lat
    ne = N_EXPERTS
    c_lat = max(1, (EC_CAPACITY_FACTOR * n_lat) // ne)
    c_ctx = max(1, (EC_CAPACITY_FACTOR * n_ctx) // ne)
    c_tot = c_lat + c_ctx
    assert c_lat % ROW_GROUP == 0 and c_ctx % ROW_GROUP == 0
    assert n_lat % LANES == 0 and n_ctx % LANES == 0 and t_all // LANES <= LANES
    segs = ((0, n_lat // LANES, c_lat, 0), (n_lat // LANES, n_ctx // LANES, c_ctx, c_lat))
    idx, gates = pl.pallas_call(
        functools.partial(_select_kernel, segs=segs),
        grid=(bsz,),
        in_specs=[pl.BlockSpec((1, t_all // LANES, ne, LANES), lambda b: (b, 0, 0, 0))],
        out_specs=[pl.BlockSpec((1, ne, c_tot), lambda b: (b, 0, 0)),
                   pl.BlockSpec((1, ne, c_tot, LANES), lambda b: (b, 0, 0, 0))],
        out_shape=[jax.ShapeDtypeStruct((bsz, ne, c_tot), I32), jax.ShapeDtypeStruct((bsz, ne, c_tot, LANES), F32)],
        scratch_shapes=[pltpu.VMEM((ne, LANES, LANES), F32), pltpu.VMEM((ne, LANES, LANES), F32),
                        pltpu.VMEM((3, ne, LANES, LANES), F32)],
        compiler_params=_cparams(("arbitrary",), VMEM_LIMIT),
        name="moe_select",
    )(aff_t)
    idx = idx.reshape(bsz * ne, 1, c_tot)
    smem = functools.partial(pl.BlockSpec, memory_space=pltpu.SMEM)

    xg = pl.pallas_call(
        _gather_kernel,
        grid=(bsz, ne),
        in_specs=[smem((1, 1, c_tot), lambda b, e: (b * ne + e, 0, 0)),
                  pl.BlockSpec((1, t_all, SUBLANES, LANES), lambda b, e: (b, 0, 0, 0), pipeline_mode=pl.Buffered(1))],
        out_specs=pl.BlockSpec((1, 1, c_tot, d), lambda b, e: (b, e, 0, 0)),
        out_shape=jax.ShapeDtypeStruct((bsz, ne, c_tot, d), BF16),
        scratch_shapes=[pltpu.VMEM((c_tot, SUBLANES, LANES), F32)],
        compiler_params=_cparams(("arbitrary", "arbitrary"), VMEM_LIMIT),
        name="moe_gather",
    )(idx, hrow)

    nf = EXPERT_FF // FF_TILE
    y = pl.pallas_call(
        functools.partial(_ffn_kernel, c_lat=c_lat),
        grid=(bsz, ne, nf),
        in_specs=[pl.BlockSpec((1, 1, c_tot, d), lambda b, e, f: (b, e, 0, 0)),
                  pl.BlockSpec((1, 1, c_tot, LANES), lambda b, e, f: (b, e, 0, 0)),
                  pl.BlockSpec((2, 1, 6 * d), lambda b, e, f: (b, 0, 0)),
                  pl.BlockSpec((1, 1, d, FF_TILE), lambda b, e, f: (l, e, 0, f)),
                  pl.BlockSpec((1, 1, d, FF_TILE), lambda b, e, f: (l, e, 0, f)),
                  pl.BlockSpec((1, 1, FF_TILE, d), lambda b, e, f: (l, e, f, 0))],
        out_specs=pl.BlockSpec((1, 1, c_tot, d), lambda b, e, f: (b, e, 0, 0)),
        out_shape=jax.ShapeDtypeStruct((bsz, ne, c_tot, d), F32),
        compiler_params=_cparams(("arbitrary", "arbitrary", "arbitrary"), VMEM_LIMIT),
        name="moe_ffn",
    )(xg, gates, modtab, w_gate, w_up, w_down)

    return pl.pallas_call(
        _combine_kernel,
        grid=(bsz, ne),
        in_specs=[smem((1, 1, c_tot), lambda b, e: (b * ne + e, 0, 0)),
                  pl.BlockSpec(memory_space=pl.ANY),
                  pl.BlockSpec((1, 1, c_tot, d), lambda b, e: (b, e, 0, 0))],
        out_specs=pl.BlockSpec(memory_space=pl.ANY),
        out_shape=jax.ShapeDtypeStruct((bsz, n_lat if latent_only else t_all, d), F32),
        scratch_shapes=[pltpu.VMEM((t_all, d), F32), pltpu.SemaphoreType.DMA(())],
        input_output_aliases={} if latent_only else {1: 0},
        compiler_params=_cparams(("arbitrary", "arbitrary"), VMEM_LIMIT),
        name="moe_combine",
    )(idx, xs, y)


def _tile_specs(d, n_lat_tiles):
    x_spec = pl.BlockSpec((1, TM, d), lambda b, i: (b, i, 0))
    m_spec = pl.BlockSpec((1, 1, 6 * d), lambda b, i: (2 * b + i // n_lat_tiles, 0, 0))
    return x_spec, m_spec


def _step_specs(d, s_tiles):
    x_spec = pl.BlockSpec((1, s_tiles * TM, d), lambda b, i: (b, i, 0))
    m_spec = pl.BlockSpec((2, 1, 6 * d), lambda b, i: (b, 0, 0))
    return x_spec, m_spec


def _tiles_per_step(nt):
    return next(s for s in TILES_PER_STEP if nt % s == 0)


def _const_spec(shape):
    return pl.BlockSpec(shape, lambda b, i: (0,) * len(shape))


def _even_layer(xs, modtab, n1g, n2g, rwt, w_in_ext, lg, gn_g, conv_w, conv_b, ln_g, ln_b, w_out, cos, sin, n_lat):
    bsz, t_all, d = xs.shape
    nt = t_all // TM
    nlt = n_lat // TM
    x_spec, m_spec = _tile_specs(d, nlt)
    n_cols = 7 * 512
    st = _tiles_per_step(nt)
    xs_spec, ms_spec = _step_specs(d, st)
    proj = pl.pallas_call(
        functools.partial(_even_in_kernel, n_lat_tiles=nlt, s_tiles=st),
        grid=(bsz, nt // st),
        in_specs=[xs_spec, ms_spec, _const_spec((1, d)), _const_spec(w_in_ext.shape),
                  pl.BlockSpec((st * TM, 512), lambda b, i: (i, 0)), pl.BlockSpec((st * TM, 512), lambda b, i: (i, 0))],
        out_specs=pl.BlockSpec((1, st * TM, n_cols), lambda b, i: (b, i, 0)),
        out_shape=jax.ShapeDtypeStruct((bsz, t_all, n_cols), BF16),
        compiler_params=_cparams(("arbitrary", "arbitrary"), VMEM_LIMIT),
        name="even_in",
    )(xs, modtab, n1g, w_in_ext, cos, sin)

    fwd = lambda s: jnp.where(s == 0, nlt, s - 1)
    bwd = lambda s: jnp.where(s == 0, nlt, nlt - s)

    def col(order, j):
        return pl.BlockSpec((1, TM, 512), lambda b, s: (b, order(s), j))

    o_f, o_b = pl.pallas_call(
        _ret_kernel,
        grid=(bsz, nt),
        in_specs=[pl.BlockSpec(lg.shape, lambda b, s: (0, 0, 0, 0)),
                  col(fwd, 0), col(fwd, 1), col(fwd, 2), col(bwd, 0), col(bwd, 1), col(bwd, 2)],
        out_specs=[col(fwd, 0), col(bwd, 0)],
        out_shape=[jax.ShapeDtypeStruct((bsz, t_all, 512), BF16)] * 2,
        scratch_shapes=[pltpu.VMEM((2, RET_HEADS, RET_DK, 128), F32),
                        pltpu.VMEM((2, RET_HEADS, RET_CHUNK, RET_CHUNK), F32),
                        pltpu.VMEM((2, RET_HEADS, RET_CHUNK, 128), F32),
                        pltpu.VMEM((2, RET_HEADS, RET_CHUNK, 128), F32),
                        pltpu.VMEM((2, RET_HEADS, 1, 128), F32)],
        compiler_params=_cparams(("arbitrary", "arbitrary"), VMEM_LIMIT),
        name="retention",
    )(lg, proj, proj, proj, proj, proj, proj)

    hpt = TM // CONV_HALO
    n_halo = t_all // CONV_HALO

    def tcol(j):
        return pl.BlockSpec((1, TM, 512), lambda b, i: (b, i, j))

    def halo(j, nxt):
        if nxt:
            return pl.BlockSpec((1, CONV_HALO, 512), lambda b, i: (b, jnp.minimum((i + 1) * hpt, n_halo - 1), j))
        return pl.BlockSpec((1, CONV_HALO, 512), lambda b, i: (b, jnp.maximum(i * hpt - 1, 0), j))

    xs_new, hrow, aff_t = pl.pallas_call(
        functools.partial(_even_out_kernel, n_lat_tiles=nlt),
        grid=(bsz, nt),
        in_specs=[x_spec, m_spec, pl.BlockSpec((1, TM, 512), lambda b, i: (b, i, 0)),
                  pl.BlockSpec((1, TM, 512), lambda b, i: (b, i, 0)),
                  tcol(3), tcol(4), tcol(5), tcol(6), halo(5, False), halo(6, False), halo(5, True), halo(6, True),
                  _const_spec((1, 512)), _const_spec(conv_w.shape), _const_spec((1, 512)), _const_spec((1, 512)),
                  _const_spec((1, 512)), _const_spec(w_out.shape), _const_spec((1, d)), _const_spec(rwt.shape)],
        out_specs=[x_spec, pl.BlockSpec((1, TM, SUBLANES, LANES), lambda b, i: (b, i, 0, 0)),
                   pl.BlockSpec((1, TM // LANES, N_EXPERTS, LANES), lambda b, i: (b, i, 0, 0))],
        out_shape=[jax.ShapeDtypeStruct(xs.shape, F32), jax.ShapeDtypeStruct((bsz, t_all, SUBLANES, LANES), F32),
                   jax.ShapeDtypeStruct((bsz, t_all // LANES, N_EXPERTS, LANES), F32)],
        scratch_shapes=[pltpu.VMEM((TM + 2 * CONV_HALO, CONV_CH), F32),
                        pltpu.VMEM((SUBLANES - 1, TM + 2 * CONV_HALO - SUBLANES, CONV_CH), F32)],
        input_output_aliases={0: 0},
        compiler_params=_cparams(("arbitrary", "arbitrary"), VMEM_LIMIT),
        name="even_out",
    )(xs, modtab, o_f, o_b, proj, proj, proj, proj, proj, proj, proj, proj,
      gn_g, conv_w, conv_b, ln_g, ln_b, w_out, n2g, rwt)
    return xs_new, hrow, aff_t


def _odd_layer(xs, modtab, n1g, n2g, rwt, w_in_ext, qag, kvag, w_uq_ext, w_ukv_p, gq, gk, w_out, cos, sin, n_lat):
    bsz, t_all, d = xs.shape
    nt = t_all // TM
    nlt = n_lat // TM
    nh = MLA_HEADS
    st = _tiles_per_step(nt)
    xs_spec, ms_spec = _step_specs(d, st)
    q, kt, v = pl.pallas_call(
        functools.partial(_odd_in_kernel, n_lat_tiles=nlt, s_tiles=st),
        grid=(bsz, nt // st),
        in_specs=[xs_spec, ms_spec, _const_spec((1, d)), _const_spec(w_in_ext.shape), _const_spec(qag.shape),
                  _const_spec(kvag.shape), _const_spec(w_uq_ext.shape), _const_spec(w_ukv_p.shape),
                  _const_spec(gq.shape), _const_spec(gk.shape),
                  pl.BlockSpec((st * TM, 128), lambda b, i: (i, 0)), pl.BlockSpec((st * TM, 128), lambda b, i: (i, 0))],
        out_specs=[pl.BlockSpec((1, nh, st * TM, MLA_QK_PAD), lambda b, i: (b, 0, i, 0)),
                   pl.BlockSpec((1, nh, st, MLA_QK_PAD, TM), lambda b, i: (b, 0, i, 0, 0)),
                   pl.BlockSpec((1, nh, st * TM, 2 * MLA_V), lambda b, i: (b, 0, i, 0))],
        out_shape=[jax.ShapeDtypeStruct((bsz, nh, t_all, MLA_QK_PAD), BF16),
                   jax.ShapeDtypeStruct((bsz, nh, nt, MLA_QK_PAD, TM), BF16),
                   jax.ShapeDtypeStruct((bsz, nh, t_all, 2 * MLA_V), BF16)],
        compiler_params=_cparams(("arbitrary", "arbitrary"), VMEM_LIMIT),
        name="odd_in",
    )(xs, modtab, n1g, w_in_ext, qag, kvag, w_uq_ext, w_ukv_p, gq, gk, cos, sin)

    tps = next(c for c in ATT_KT_PER_STEP if nt % c == 0)
    nsub = ATT_TQ // ATT_SUB
    o = pl.pallas_call(
        functools.partial(_attn_kernel, tps=tps),
        grid=(bsz, nh, n_lat // ATT_TQ),
        in_specs=[pl.BlockSpec((1, 1, ATT_TQ, MLA_QK_PAD), lambda b, h, i: (b, h, i, 0)),
                  pl.BlockSpec((1, 1, nt, MLA_QK_PAD, TM), lambda b, h, i: (b, h, 0, 0, 0)),
                  pl.BlockSpec((1, 1, t_all, 2 * MLA_V), lambda b, h, i: (b, h, 0, 0))],
        out_specs=pl.BlockSpec((1, ATT_TQ, MLA_V), lambda b, h, i: (b, i, h)),
        out_shape=jax.ShapeDtypeStruct((bsz, n_lat, nh * MLA_V), BF16),
        scratch_shapes=[pltpu.VMEM((2, ATT_GROUP, ATT_SUB, tps * TM), F32), pltpu.VMEM((nsub, ATT_SUB, LANES), F32),
                        pltpu.VMEM((nsub, ATT_SUB, 2 * MLA_V), F32)],
        compiler_params=_cparams(("arbitrary", "arbitrary", "arbitrary"), VMEM_LIMIT),
        name="mla_attention",
    )(q, kt, v)
    o_ctx = pl.pallas_call(
        _attn_ctx_kernel,
        grid=(bsz, nh),
        in_specs=[pl.BlockSpec((1, 1, TM, MLA_QK_PAD), lambda b, h: (b, h, nlt, 0)),
                  pl.BlockSpec((1, 1, 1, MLA_QK_PAD, TM), lambda b, h: (b, h, nlt, 0, 0)),
                  pl.BlockSpec((1, 1, TM, 2 * MLA_V), lambda b, h: (b, h, nlt, 0))],
        out_specs=pl.BlockSpec((1, TM, MLA_V), lambda b, h: (b, 0, h)),
        out_shape=jax.ShapeDtypeStruct((bsz, TM, nh * MLA_V), BF16),
        compiler_params=_cparams(("arbitrary", "arbitrary"), VMEM_LIMIT),
        name="mla_attention_ctx",
    )(q, kt, v)

    o_specs = [pl.BlockSpec((1, TM, nh * MLA_V), lambda b, i, j=j: (b, jnp.minimum(i * st + j, nlt - 1), 0))
               for j in range(st)]
    xs_new, hrow, aff_t = pl.pallas_call(
        functools.partial(_odd_out_kernel, n_lat_tiles=nlt, s_tiles=st),
        grid=(bsz, nt // st),
        in_specs=[xs_spec, ms_spec, *o_specs, pl.BlockSpec((1, TM, nh * MLA_V), lambda b, i: (b, 0, 0)),
                  _const_spec(w_out.shape), _const_spec((1, d)), _const_spec(rwt.shape)],
        out_specs=[xs_spec, pl.BlockSpec((1, st * TM, SUBLANES, LANES), lambda b, i: (b, i, 0, 0)),
                   pl.BlockSpec((1, st * TM // LANES, N_EXPERTS, LANES), lambda b, i: (b, i, 0, 0))],
        out_shape=[jax.ShapeDtypeStruct(xs.shape, F32), jax.ShapeDtypeStruct((bsz, t_all, SUBLANES, LANES), F32),
                   jax.ShapeDtypeStruct((bsz, t_all // LANES, N_EXPERTS, LANES), F32)],
        input_output_aliases={0: 0},
        compiler_params=_cparams(("arbitrary", "arbitrary"), VMEM_LIMIT),
        name="odd_out",
    )(xs, modtab, *([o] * st), o_ctx, w_out, n2g, rwt)
    return xs_new, hrow, aff_t


def _rope_tables(n_lat, n_ctx, nf):
    t = jnp.arange(n_lat)
    row = (t // GRID_W).astype(F32)
    col = (t % GRID_W).astype(F32)
    inv = ROPE_BASE ** (-(jnp.arange(nf, dtype=F32) / nf))
    ar = row[:, None] * inv[None, :]
    ac = col[:, None] * inv[None, :]
    cos = jnp.concatenate([jnp.cos(ar), jnp.cos(ar), jnp.cos(ac), jnp.cos(ac)], axis=-1)
    sin = jnp.concatenate([-jnp.sin(ar), jnp.sin(ar), -jnp.sin(ac), jnp.sin(ac)], axis=-1)
    cos = jnp.concatenate([cos, jnp.ones((n_ctx, 4 * nf), F32)], axis=0)
    sin = jnp.concatenate([sin, jnp.zeros((n_ctx, 4 * nf), F32)], axis=0)
    return cos, sin


def _swap_perm(nf):
    a = jnp.arange(nf)
    return jnp.concatenate([a + nf, a, a + 3 * nf, a + 2 * nf])


def kernel(x, c, ctx, c_ctx, mod_w, mod_b, norm1_g, norm2_g, ev_w_in, ret_decay_f, ret_decay_b, ret_gn_g, conv_w, conv_b, conv_ln_g, conv_ln_b, ev_w_out, od_w_in, mla_q_a_g, mla_kv_a_g, mla_w_uq, mla_w_ukv, mla_qk_g_q, mla_qk_g_k, od_w_out, router_w, moe_w_gate, moe_w_up, moe_w_down):
    bsz, n_lat, d = x.shape
    n_ctx = ctx.shape[1]
    depth = mod_w.shape[0]
    assert d == D_MODEL and n_ctx == TM and n_lat % ATT_TQ == 0 and bsz + 1 <= 8

    xs = jnp.concatenate([x, ctx], axis=1)
    cc = jnp.zeros((8, d), F32).at[:bsz].set(c).at[bsz].set(c_ctx)
    mod = _mod_all(cc, mod_w, mod_b)

    cos_e, sin_e = _rope_tables(n_lat, n_ctx, RET_DK // 4)
    cos_e, sin_e = jnp.tile(cos_e, (1, RET_HEADS)), jnp.tile(sin_e, (1, RET_HEADS))
    cos_o, sin_o = _rope_tables(n_lat, n_ctx, MLA_ROPE // 4)
    cos_o, sin_o = jnp.tile(cos_o, (1, 2)), jnp.tile(sin_o, (1, 2))
    perm_o = _swap_perm(MLA_ROPE // 4)

    for l in range(depth):
        i = l // 2
        modtab = jnp.stack([mod[l, :bsz], jnp.broadcast_to(mod[l, bsz], (bsz, 6 * d))], axis=1).reshape(bsz * 2, 1, 6 * d)
        n1g = norm1_g[l][None]
        n2g = norm2_g[l][None]
        rw = router_w[l]
        rw_hi = rw.astype(BF16)
        rw_lo = (rw - rw_hi.astype(F32)).astype(BF16)
        pad = jnp.zeros((d, 128 - N_EXPERTS), BF16)
        rwt = jnp.concatenate([rw_hi, pad, rw_lo, pad], axis=1)
        if l % 2 == 0:
            w_ext = ev_w_in[i].astype(BF16)
            lg = jnp.broadcast_to(jnp.stack([ret_decay_f[i], ret_decay_b[i]])[:, :, None, None], (2, RET_HEADS, 1, 128))
            cw = jnp.concatenate([conv_w[i], jnp.zeros((1, CONV_CH), F32)], axis=0)
            xs, hrow, aff_t = _even_layer(xs, modtab, n1g, n2g, rwt, w_ext, lg, ret_gn_g[i][None], cw, conv_b[i][None],
                                    conv_ln_g[i][None], conv_ln_b[i][None], ev_w_out[i].astype(BF16), cos_e, sin_e, n_lat)
        else:
            w = od_w_in[i]
            kpe = w[:, MLA_Q_LORA + MLA_KV_LORA:]
            w_ext = jnp.concatenate([w[:, :MLA_Q_LORA + MLA_KV_LORA], kpe, kpe, kpe[:, perm_o], kpe[:, perm_o]], axis=1).astype(BF16)
            wq = mla_w_uq[i].reshape(MLA_Q_LORA, MLA_HEADS, MLA_QK)
            wq_r = wq[:, :, MLA_NOPE:]
            w_uq_ext = jnp.concatenate([wq[:, :, :MLA_NOPE].reshape(MLA_Q_LORA, -1), wq_r.reshape(MLA_Q_LORA, -1),
                                        wq_r[:, :, perm_o].reshape(MLA_Q_LORA, -1)], axis=1).astype(BF16)
            wkv = mla_w_ukv[i].reshape(MLA_KV_LORA, MLA_HEADS, MLA_NOPE + MLA_V)
            w_ukv_p = jnp.concatenate([wkv[:, :, :MLA_NOPE].reshape(MLA_KV_LORA, -1),
                                       wkv[:, :, MLA_NOPE:].reshape(MLA_KV_LORA, -1)], axis=1).astype(BF16)

            def gains(g):
                gr = g[MLA_NOPE:]
                return jnp.stack([g[:MLA_NOPE], jnp.tile(gr, 2), jnp.tile(gr[perm_o], 2)])

            xs, hrow, aff_t = _odd_layer(xs, modtab, n1g, n2g, rwt, w_ext, mla_q_a_g[i][None], mla_kv_a_g[i][None], w_uq_ext,
                                   w_ukv_p, gains(mla_qk_g_q[i]), gains(mla_qk_g_k[i]), od_w_out[i].astype(BF16),
                                   cos_o, sin_o, n_lat)
        xs = _moe(xs, hrow, aff_t, modtab, l, moe_w_gate, moe_w_up, moe_w_down, n_lat, latent_only=l == depth - 1)
    return xs
```

```python
import functools

import jax
import jax.numpy as jnp
from jax import lax
from jax.experimental import pallas as pl
from jax.experimental.pallas import tpu as pltpu

F32 = jnp.float32
BF16 = jnp.bfloat16
I32 = jnp.int32

D_MODEL = 1024
GRID_W = 64
RET_HEADS = 4
RET_DK = 128
RET_CHUNK = 256
CONV_CH = 512
CONV_WIDTH = 31
CONV_HALO = 16
MLA_HEADS = 8
MLA_Q_LORA = 512
MLA_KV_LORA = 256
MLA_NOPE = 128
MLA_ROPE = 64
MLA_V = 128
MLA_QK = MLA_NOPE + MLA_ROPE
MLA_QK_PAD = 256
N_EXPERTS = 16
EC_CAPACITY_FACTOR = 2
EXPERT_FF = 1024
ROPE_BASE = 10000.0
LOG2E = 1.4426950408889634
EPS = 1e-6

LANES = 128
SUBLANES = 8
TM = 256
ATT_TQ = 4096
ATT_SUB = 1024
ATT_GROUP = 1
ATT_UNROLL = 4
ATT_KT_PER_STEP = (3, 2, 1)
FF_TILE = 1024
SELECT_REFINE_STEPS = 29
F32_MIN_NORMAL_BITS = 0x00800000
TILES_PER_STEP = (3, 1)
ROW_GROUP = 8
SCATTER_GROUP = 4
VMEM_LIMIT = 56 * 2 ** 20

_NT = (((1,), (1,)), ((), ()))


def _cparams(sem, vmem=None):
    return pltpu.CompilerParams(dimension_semantics=sem, vmem_limit_bytes=vmem)


def _dot(a, b):
    return jnp.dot(a, b, preferred_element_type=F32)


def _split3(a):
    a1 = a.astype(BF16)
    r = a - a1.astype(F32)
    a2 = r.astype(BF16)
    a3 = (r - a2.astype(F32)).astype(BF16)
    return a1, a2, a3


def _dot_hi(a, b, dn):
    a1, a2, a3 = _split3(a)
    b1, b2, b3 = _split3(b)
    d = lambda x, y: lax.dot_general(x, y, dn, preferred_element_type=F32)
    return ((d(a3, b1) + d(a2, b2) + d(a1, b3)) + (d(a2, b1) + d(a1, b2))) + d(a1, b1)


def _silu(a):
    return a * jax.nn.sigmoid(a)


def _rms(x, g):
    return x * lax.rsqrt(jnp.mean(x * x, axis=-1, keepdims=True) + EPS) * g


def _modulate(x, g, shift, scale):
    return _rms(x, g) * (1.0 + scale) + shift


def _mod_kernel(c_ref, w_ref, b_ref, o_ref):
    s = _silu(c_ref[...])
    o_ref[0] = _dot_hi(s, w_ref[0], (((1,), (0,)), ((), ()))) + b_ref[0]


def _mod_all(cc, mod_w, mod_b):
    depth, d, n6 = mod_w.shape
    tn = 1536
    return pl.pallas_call(
        _mod_kernel,
        grid=(depth, n6 // tn),
        in_specs=[pl.BlockSpec((8, d), lambda l, j: (0, 0)),
                  pl.BlockSpec((1, d, tn), lambda l, j: (l, 0, j)),
                  pl.BlockSpec((1, 1, tn), lambda l, j: (l, 0, j))],
        out_specs=pl.BlockSpec((1, 8, tn), lambda l, j: (l, 0, j)),
        out_shape=jax.ShapeDtypeStruct((depth, 8, n6), F32),
        compiler_params=_cparams(("arbitrary", "arbitrary"), VMEM_LIMIT),
        name="mod_all",
    )(cc, mod_w, mod_b.reshape(depth, 1, n6))


def _epilogue(x, out, m, n2g, rwt, xo_ref, hrow_ref, aff_ref, j=0):
    d = D_MODEL
    rs = slice(j * TM, (j + 1) * TM)
    xn = x + m[:, 2 * d:3 * d] * out
    xo_ref[0, rs, :] = xn
    h2 = _modulate(xn, n2g, m[:, 3 * d:4 * d], m[:, 4 * d:5 * d])
    hrow_ref[0, rs] = h2.reshape(TM, SUBLANES, LANES)
    h_hi = h2.astype(BF16)
    h_lo = (h2 - h_hi.astype(F32)).astype(BF16)
    z = _dot(h_hi, rwt)
    logits = (z[:, 0:128] + z[:, 128:256]) + _dot(h_lo, rwt[:, 0:128])
    lane = lax.broadcasted_iota(I32, logits.shape, 1)
    logits = jnp.where(lane < N_EXPERTS, logits, -jnp.inf)
    e = jnp.exp(logits - jnp.max(logits, axis=-1, keepdims=True))
    aff = e / jnp.sum(e, axis=-1, keepdims=True)
    aff_t = aff.T
    for k in range(TM // LANES):
        aff_ref[0, j * (TM // LANES) + k] = aff_t[0:N_EXPERTS, k * LANES:(k + 1) * LANES]


def _step_mod(m_ref, tile, n_lat_tiles):
    return jnp.where(tile == n_lat_tiles, m_ref[1], m_ref[0])


def _even_in_kernel(x_ref, m_ref, g_ref, w_ref, cos_ref, sin_ref, o_ref, *, n_lat_tiles, s_tiles):
    for j in range(s_tiles):
        _even_in_tile(x_ref, _step_mod(m_ref, pl.program_id(1) * s_tiles + j, n_lat_tiles), g_ref, w_ref,
                      cos_ref, sin_ref, o_ref, slice(j * TM, (j + 1) * TM))


def _even_in_tile(x_ref, m, g_ref, w_ref, cos_ref, sin_ref, o_ref, rs):
    d = D_MODEL
    h = _modulate(x_ref[0, rs, :], g_ref[...], m[:, 0:d], m[:, d:2 * d]).astype(BF16)

    def grp(j):
        return _dot(h, w_ref[:, j * 512:(j + 1) * 512])

    c = cos_ref[rs, :]
    s = sin_ref[rs, :]
    nf = RET_DK // 4
    lane = lax.broadcasted_iota(I32, (TM, 512), 1)
    first = (lane % (2 * nf)) < nf

    def partner(v):
        return jnp.where(first, pltpu.roll(v, 512 - nf, axis=1), pltpu.roll(v, nf, axis=1))

    q = grp(0)
    k = grp(1)
    o_ref[0, rs, 0:512] = (q * c + partner(q) * s).astype(BF16)
    o_ref[0, rs, 512:1024] = ((k * c + partner(k) * s) * (RET_DK ** -0.5)).astype(BF16)
    for j in range(2, 7):
        o_ref[0, rs, j * 512:(j + 1) * 512] = grp(j).astype(BF16)


def _ret_kernel(lg_ref, qf_ref, kf_ref, vf_ref, qb_ref, kb_ref, vb_ref, of_ref, ob_ref,
                st_ref, intra_ref, xi_ref, zeta_ref, dc_ref):
    L = RET_CHUNK
    nc = TM // L

    @pl.when(pl.program_id(1) == 0)
    def _():
        st_ref[...] = jnp.zeros(st_ref.shape, F32)
        ii = lax.broadcasted_iota(I32, (L, L), 0).astype(F32)
        jj = lax.broadcasted_iota(I32, (L, L), 1).astype(F32)
        pos = lax.broadcasted_iota(I32, (L, LANES), 0).astype(F32)
        for dr in range(2):
            for h in range(RET_HEADS):
                x = lg_ref[dr, h]
                lg = jnp.minimum(x, 0.0) - jnp.log(1.0 + jnp.exp(-jnp.abs(x)))
                if dr == 0:
                    diff, xpos, zpos = ii - jj, pos + 1.0, (L - 1.0) - pos
                else:
                    diff, xpos, zpos = jj - ii, L - pos, pos
                intra_ref[dr, h] = jnp.where(diff >= 0, jnp.exp(lg[:, 0:1] * jnp.maximum(diff, 0.0)), 0.0)
                xi_ref[dr, h] = jnp.exp(lg * xpos)
                zeta_ref[dr, h] = jnp.exp(lg * zpos)
                dc_ref[dr, h] = jnp.exp(lg * float(L))

    for dr, (q_ref, k_ref, v_ref, o_ref) in enumerate(((qf_ref, kf_ref, vf_ref, of_ref),
                                                      (qb_ref, kb_ref, vb_ref, ob_ref))):
        order = range(nc) if dr == 0 else range(nc - 1, -1, -1)
        for c in order:
            for h in range(RET_HEADS):
                rs = slice(c * L, (c + 1) * L)
                cs = slice(h * 128, (h + 1) * 128)
                qc = q_ref[0, rs, cs]
                kc = k_ref[0, rs, cs]
                vc = v_ref[0, rs, cs]
                st = st_ref[dr, h]
                sc = lax.dot_general(qc, kc, _NT, preferred_element_type=F32) * intra_ref[dr, h]
                o = _dot(sc.astype(BF16), vc) + _dot(qc, st.astype(BF16)) * xi_ref[dr, h]
                kz = (kc.astype(F32) * zeta_ref[dr, h]).T.astype(BF16)
                st_ref[dr, h] = st * dc_ref[dr, h] + _dot(kz, vc)
                o_ref[0, rs, cs] = o.astype(BF16)


def _even_out_kernel(x_ref, m_ref, of_ref, ob_ref, gf_ref, gb_ref, bv_ref, bg_ref,
                     pv_ref, pg_ref, nv_ref, ng_ref, gn_ref, cw_ref, cb_ref, lng_ref, lnb_ref,
                     wo_ref, n2_ref, rwt_ref, xo_ref, hrow_ref, aff_ref, u_scr, ush_scr, *, n_lat_tiles):
    i = pl.program_id(1)
    hl = CONV_HALO

    def glu(v_ref, g_ref):
        return v_ref[0].astype(F32) * jax.nn.sigmoid(g_ref[0].astype(F32))

    prev_ok = jnp.logical_and(i >= 1, i < n_lat_tiles)
    next_ok = i < n_lat_tiles - 1
    u_scr[0:hl, :] = jnp.where(prev_ok, glu(pv_ref, pg_ref), 0.0)
    u_scr[hl:hl + TM, :] = glu(bv_ref, bg_ref)
    u_scr[hl + TM:2 * hl + TM, :] = jnp.where(next_ok, glu(nv_ref, ng_ref), 0.0)
    n_sh = TM + 2 * hl - SUBLANES
    for r in range(1, SUBLANES):
        ush_scr[r - 1] = u_scr[r:r + n_sh, :]
    y = jnp.zeros((TM, CONV_CH), F32) + cb_ref[...]
    off = hl - CONV_WIDTH // 2
    for k in range(CONV_WIDTH):
        r = (k + off) % SUBLANES
        a = k + off - r
        win = u_scr[a:a + TM, :] if r == 0 else ush_scr[r - 1, a:a + TM, :]
        y = y + cw_ref[k:k + 1, :] * win
    mu = jnp.mean(y, axis=-1, keepdims=True)
    yc = y - mu
    var = jnp.mean(yc * yc, axis=-1, keepdims=True)
    conv = _silu(yc * lax.rsqrt(var + EPS) * lng_ref[...] + lnb_ref[...])

    def gnorm(o_ref, g_ref):
        parts = []
        for h in range(RET_HEADS):
            cs = slice(h * 128, (h + 1) * 128)
            o = o_ref[0, :, cs].astype(F32)
            mu_h = jnp.mean(o, axis=-1, keepdims=True)
            oc = o - mu_h
            var_h = jnp.mean(oc * oc, axis=-1, keepdims=True)
            parts.append(oc * lax.rsqrt(var_h + EPS) * gn_ref[:, cs] * _silu(g_ref[0, :, cs].astype(F32)))
        return parts

    pf = gnorm(of_ref, gf_ref)
    pb = gnorm(ob_ref, gb_ref)
    out = _dot(conv.astype(BF16), wo_ref[512:1024, :])
    for h in range(RET_HEADS):
        out = out + _dot((pf[h] + pb[h]).astype(BF16), wo_ref[h * 128:(h + 1) * 128, :])
    _epilogue(x_ref[0], out, m_ref[0], n2_ref[...], rwt_ref[...], xo_ref, hrow_ref, aff_ref)


def _odd_in_kernel(x_ref, m_ref, g_ref, win_ref, qag_ref, kvag_ref, wuq_ref, wukv_ref,
                   gq_ref, gk_ref, cos_ref, sin_ref, q_ref, kt_ref, v_ref, *, n_lat_tiles, s_tiles):
    for j in range(s_tiles):
        _odd_in_tile(x_ref, _step_mod(m_ref, pl.program_id(1) * s_tiles + j, n_lat_tiles), g_ref, win_ref, qag_ref,
                     kvag_ref, wuq_ref, wukv_ref, gq_ref, gk_ref, cos_ref, sin_ref, q_ref, kt_ref, v_ref, j)


def _odd_in_tile(x_ref, m, g_ref, win_ref, qag_ref, kvag_ref, wuq_ref, wukv_ref,
                 gq_ref, gk_ref, cos_ref, sin_ref, q_ref, kt_ref, v_ref, j):
    d = D_MODEL
    nh = MLA_HEADS
    rs = slice(j * TM, (j + 1) * TM)
    h = _modulate(x_ref[0, rs, :], g_ref[...], m[:, 0:d], m[:, d:2 * d]).astype(BF16)
    c = _dot(h, win_ref[...])
    cq = _rms(c[:, 0:512], qag_ref[...]).astype(BF16)
    ckv = _rms(c[:, 512:768], kvag_ref[...]).astype(BF16)
    kpe = c[:, 768:896]
    kpe_sw = c[:, 896:1024]
    qa = _dot(cq, wuq_ref[...])
    kva = _dot(ckv, wukv_ref[...])
    cos = cos_ref[rs, :]
    sin = sin_ref[rs, :]
    gq = gq_ref[...]
    gk = gk_ref[...]
    lane = lax.broadcasted_iota(I32, (TM, 128), 1)
    lo = lane < MLA_ROPE
    inv_d = 1.0 / MLA_QK

    ss_kpe = 0.5 * jnp.sum(kpe * kpe, axis=-1, keepdims=True)
    kr = kpe * gk[1:2] * cos + kpe_sw * gk[2:3] * sin
    for p in range(nh // 2):
        r = qa[:, 1024 + p * 128:1024 + (p + 1) * 128]
        r_sw = qa[:, 1536 + p * 128:1536 + (p + 1) * 128]
        r2 = r * r
        ss_r = (jnp.sum(jnp.where(lo, r2, 0.0), axis=-1, keepdims=True),
                jnp.sum(jnp.where(lo, 0.0, r2), axis=-1, keepdims=True))
        qr = r * gq[1:2] * cos + r_sw * gq[2:3] * sin
        for s in range(2):
            hd = 2 * p + s
            keep = lo if s == 0 else jnp.logical_not(lo)
            qn = qa[:, hd * 128:(hd + 1) * 128]
            nq = lax.rsqrt((jnp.sum(qn * qn, axis=-1, keepdims=True) + ss_r[s]) * inv_d + EPS) * (MLA_QK ** -0.5 * LOG2E)
            q_ref[0, hd, rs, 0:128] = (qn * nq * gq[0:1]).astype(BF16)
            q_ref[0, hd, rs, 128:256] = jnp.where(keep, qr * nq, 0.0).astype(BF16)
            kn = kva[:, hd * 128:(hd + 1) * 128]
            nk = lax.rsqrt((jnp.sum(kn * kn, axis=-1, keepdims=True) + ss_kpe) * inv_d + EPS)
            k_full = jnp.concatenate([kn * nk * gk[0:1], jnp.where(keep, kr * nk, 0.0)], axis=-1)
            kt_ref[0, hd, j] = k_full.T.astype(BF16)
            v_ref[0, hd, rs, 0:MLA_V] = kva[:, 1024 + hd * 128:1024 + (hd + 1) * 128].astype(BF16)
            v_ref[0, hd, rs, MLA_V:2 * MLA_V] = jnp.ones((TM, MLA_V), BF16)


def _attn_kernel(q_ref, kt_ref, v_ref, o_ref, s_scr, m_scr, acc_scr, *, tps):
    tq = q_ref.shape[2]
    nsub = tq // ATT_SUB
    kw = kt_ref.shape[4]
    n_steps = kt_ref.shape[2] // tps
    n_units = (nsub // ATT_GROUP) * n_steps

    def split(u):
        g = u // n_steps
        return g, u - g * n_steps

    def qk(u, buf):
        g, j = split(u)
        for k in range(ATT_GROUP):
            q = q_ref[0, 0, pl.ds(pl.multiple_of((g * ATT_GROUP + k) * ATT_SUB, ATT_SUB), ATT_SUB), :]
            for c in range(tps):
                s_scr[buf, k, :, c * kw:(c + 1) * kw] = _dot(q, kt_ref[0, 0, j * tps + c])

    def smpv(u, buf):
        g, j = split(u)
        vj = v_ref[0, 0, pl.ds(pl.multiple_of(j * (tps * kw), tps * kw), tps * kw), :]
        for k in range(ATT_GROUP):
            s = g * ATT_GROUP + k
            sv = s_scr[buf, k]
            m = m_scr[s]
            tiles = [sv[:, c * LANES:(c + 1) * LANES] for c in range(sv.shape[1] // LANES)]
            mx = functools.reduce(jnp.maximum, tiles)
            m_new = jnp.maximum(m, jnp.broadcast_to(jnp.max(mx, axis=-1, keepdims=True), m.shape))
            alpha = jnp.exp2(m - m_new)
            p = jnp.concatenate([jnp.exp2(t - m_new).astype(BF16) for t in tiles], axis=-1)
            acc = acc_scr[s]
            pv = _dot(p, vj)
            acc_scr[s] = jnp.concatenate([alpha * acc[:, 0:MLA_V] + pv[:, 0:MLA_V],
                                          alpha * acc[:, MLA_V:2 * MLA_V] + pv[:, MLA_V:2 * MLA_V]], axis=-1)
            m_scr[s] = m_new

    m_scr[...] = jnp.full(m_scr.shape, -jnp.inf, F32)
    acc_scr[...] = jnp.zeros(acc_scr.shape, F32)
    qk(0, 0)

    def body(i, carry):
        for r in range(ATT_UNROLL):
            u = ATT_UNROLL * i + r
            qk(u + 1, (r + 1) % 2)
            smpv(u, r % 2)
        return carry

    n_loop = (n_units - 1) // ATT_UNROLL
    lax.fori_loop(0, n_loop, body, 0)
    for u in range(ATT_UNROLL * n_loop, n_units):
        if u + 1 < n_units:
            qk(u + 1, (u + 1) % 2)
        smpv(u, u % 2)
    for s in range(nsub):
        acc = acc_scr[s]
        o_ref[0, s * ATT_SUB:(s + 1) * ATT_SUB, :] = (acc[:, 0:MLA_V] / acc[:, MLA_V:2 * MLA_V]).astype(BF16)


def _attn_ctx_kernel(q_ref, kt_ref, v_ref, o_ref):
    s = _dot(q_ref[0, 0], kt_ref[0, 0, 0])
    p = jnp.exp2(s - jnp.max(s, axis=-1, keepdims=True))
    acc = _dot(p.astype(BF16), v_ref[0, 0])
    o_ref[0] = (acc[:, 0:MLA_V] / acc[:, MLA_V:2 * MLA_V]).astype(BF16)


def _odd_out_kernel(*refs, n_lat_tiles, s_tiles):
    x_ref, m_ref = refs[0:2]
    o_refs = refs[2:2 + s_tiles]
    oc_ref, wo_ref, n2_ref, rwt_ref, xo_ref, hrow_ref, aff_ref = refs[2 + s_tiles:]
    for j in range(s_tiles):
        tile = pl.program_id(1) * s_tiles + j
        o = jnp.where(tile == n_lat_tiles, oc_ref[0], o_refs[j][0])
        out = _dot(o, wo_ref[...])
        _epilogue(x_ref[0, j * TM:(j + 1) * TM, :], out, _step_mod(m_ref, tile, n_lat_tiles), n2_ref[...], rwt_ref[...],
                  xo_ref, hrow_ref, aff_ref, j)


def _ind(c):
    return jnp.where(c, 1.0, 0.0)


def _select_kernel(a_ref, idx_ref, gate_ref, sel_scr, l_scr, a_scr, *, segs):
    ne = N_EXPERTS
    ri = lax.broadcasted_iota(I32, (LANES, LANES), 0)
    ci = lax.broadcasted_iota(I32, (LANES, LANES), 1)
    upper = _ind(ri <= ci).astype(BF16)
    ones8 = jnp.ones((8, LANES), BF16)
    ones_sq = jnp.ones((LANES, LANES), BF16)
    for k0, nt, cap, c0 in segs:
        a = a_ref[0, k0:k0 + nt]
        def count_ge(v, a=a):
            return jnp.sum(jnp.sum(_ind(a >= v[None]), axis=0), axis=-1, keepdims=True)

        def sbody(i, prefix, cap=cap):
            cand = prefix | jnp.left_shift(jnp.int32(1), 30 - i)
            return jnp.where(count_ge(lax.bitcast_convert_type(cand, F32)) >= cap, cand, prefix)

        tbits = lax.fori_loop(0, 31, sbody, jnp.zeros((ne, 1), I32))

        def rbody(i, lohi, cap=cap):
            lo, hi = lohi
            mid = lo + (hi - lo) * 0.5
            ok = count_ge(mid) >= cap
            return jnp.where(ok, mid, lo), jnp.where(ok, hi, mid)

        hbits = jnp.maximum(tbits + 1, F32_MIN_NORMAL_BITS)
        lo, hi = lax.fori_loop(0, SELECT_REFINE_STEPS, rbody,
                               (lax.bitcast_convert_type(tbits, F32), lax.bitcast_convert_type(hbits, F32)))
        gt = a >= hi[None]
        eqf = _ind(a >= lo[None]) - _ind(gt)
        need = cap - jnp.sum(jnp.sum(_ind(gt), axis=0), axis=-1, keepdims=True)
        leq = _dot(eqf.reshape(nt * ne, LANES).astype(BF16), upper).reshape(nt, ne, LANES)
        carry = jnp.zeros((ne, 1), F32)
        sel = []
        for k in range(nt):
            rank = carry + leq[k] - eqf[k]
            sel.append(jnp.where(gt[k], 1.0, jnp.where(rank < need, eqf[k], 0.0)))
            carry = carry + leq[k][:, LANES - 1:LANES]
        sel = jnp.stack(sel)
        def put(scr, val, nt=nt):
            scr[:, 0:nt, :] = jnp.transpose(val, (1, 0, 2))
            scr[:, nt:LANES, :] = jnp.zeros((ne, LANES - nt, LANES), F32)

        put(sel_scr, sel)
        put(l_scr, _dot(sel.reshape(nt * ne, LANES).astype(BF16), upper).reshape(nt, ne, LANES))
        a_hi = a.astype(BF16).astype(F32)
        a_mid = (a - a_hi).astype(BF16).astype(F32)
        put(a_scr.at[0], a_hi)
        put(a_scr.at[1], a_mid)
        put(a_scr.at[2], a - a_hi - a_mid)

        rows = -(-cap // LANES) * LANES
        slot = lax.broadcasted_iota(I32, (rows, LANES), 0).astype(F32)
        lane = lax.broadcasted_iota(I32, (rows, LANES), 1)
        lane_f = lane.astype(F32)
        idx_all = jnp.zeros((rows, LANES), F32)
        for e in range(ne):
            sel_e = sel_scr[e].astype(BF16)
            tot = lax.dot_general(ones8, sel_e, _NT, preferred_element_type=F32)
            pb = _dot(tot.astype(BF16), upper)[0:1]
            le = pb <= slot
            kc = _dot(_ind(le).astype(BF16), ones_sq)
            base = _dot(jnp.where(le, tot[0:1], 0.0).astype(BF16), ones_sq)
            onehot = _ind(lane_f == kc).astype(BF16)
            rhs = jnp.concatenate([l_scr[e], a_scr[0, e], a_scr[1, e], a_scr[2, e]], axis=-1).astype(BF16)
            g = _dot(onehot, rhs)
            off = jnp.sum(_ind(g[:, 0:LANES] <= slot - base), axis=-1, keepdims=True)
            aff = (g[:, LANES:2 * LANES] + g[:, 2 * LANES:3 * LANES]) + g[:, 3 * LANES:4 * LANES]
            gate = jnp.sum(jnp.where(lane_f == off, aff, 0.0), axis=-1, keepdims=True)
            idx_all = jnp.where(lane == e, (kc + k0) * LANES + off, idx_all)
            gate_ref[0, e, c0:c0 + cap, :] = jnp.broadcast_to(gate[0:cap], (cap, LANES))
        idx_ref[0, :, c0:c0 + cap] = idx_all.T[0:ne, 0:cap].astype(I32)


def _gather_kernel(idx_ref, h_ref, o_ref, rows):
    c_tot = rows.shape[0]

    def body(i, carry):
        base = pl.multiple_of(i * ROW_GROUP, ROW_GROUP)
        for k in range(ROW_GROUP):
            rows[base + k] = h_ref[0, idx_ref[0, 0, base + k]]
        return carry

    lax.fori_loop(0, c_tot // ROW_GROUP, body, 0)
    o_ref[0, 0] = rows[...].reshape(c_tot, D_MODEL).astype(BF16)


def _ffn_kernel(x_ref, gate_ref, m_ref, wg_ref, wu_ref, wd_ref, o_ref, *, c_lat):
    d = D_MODEL
    f = pl.program_id(2)
    x = x_ref[0, 0]
    c_tot = x.shape[0]
    a = _dot(x, wg_ref[0, 0].astype(BF16))
    u = _dot(x, wu_ref[0, 0].astype(BF16))
    y = _dot((_silu(a) * u).astype(BF16), wd_ref[0, 0].astype(BF16))
    gate = gate_ref[0, 0]
    for lo_, hi_, mrow in ((0, c_lat, 0), (c_lat, c_tot, 1)):
        scale = jnp.concatenate([gate[lo_:hi_]] * (d // LANES), axis=-1) * m_ref[mrow][:, 5 * d:6 * d]
        part = y[lo_:hi_] * scale

        @pl.when(f == 0)
        def _(part=part, lo_=lo_, hi_=hi_):
            o_ref[0, 0, lo_:hi_, :] = part

        @pl.when(f > 0)
        def _(part=part, lo_=lo_, hi_=hi_):
            o_ref[0, 0, lo_:hi_, :] += part


def _combine_kernel(idx_ref, x_hbm, y_ref, o_hbm, acc, sem):
    b = pl.program_id(0)
    e = pl.program_id(1)
    c_tot = y_ref.shape[2]

    @pl.when(e == 0)
    def _():
        cp = pltpu.make_async_copy(x_hbm.at[b], acc, sem)
        cp.start()
        cp.wait()

    def body(i, carry):
        base = pl.multiple_of(i * SCATTER_GROUP, SCATTER_GROUP)
        ts = [idx_ref[0, 0, base + k] for k in range(SCATTER_GROUP)]
        new = [acc[pl.ds(ts[k], 1), :] + y_ref[0, 0, pl.ds(base + k, 1), :] for k in range(SCATTER_GROUP)]
        for k in range(SCATTER_GROUP):
            acc[pl.ds(ts[k], 1), :] = new[k]
        return carry

    lax.fori_loop(0, c_tot // SCATTER_GROUP, body, 0)

    @pl.when(e == pl.num_programs(1) - 1)
    def _():
        cp = pltpu.make_async_copy(acc.at[pl.ds(0, o_hbm.shape[1])], o_hbm.at[b], sem)
        cp.start()
        cp.wait()


def _moe(xs, hrow, aff_t, modtab, l, w_gate, w_up, w_down, n_lat, latent_only):
    bsz, t_all, d = xs.shape
    n_ctx = t_all - n_lat
    ne = N_EXPERTS
    c_lat = max(1, (EC_CAPACITY_FACTOR * n_lat) // ne)
    c_ctx = max(1, (EC_CAPACITY_FACTOR * n_ctx) // ne)
    c_tot = c_lat + c_ctx
    assert c_lat % ROW_GROUP == 0 and c_ctx % ROW_GROUP == 0
    assert n_lat % LANES == 0 and n_ctx % LANES == 0 and t_all // LANES <= LANES
    segs = ((0, n_lat // LANES, c_lat, 0), (n_lat // LANES, n_ctx // LANES, c_ctx, c_lat))
    idx, gates = pl.pallas_call(
        functools.partial(_select_kernel, segs=segs),
        grid=(bsz,),
        in_specs=[pl.BlockSpec((1, t_all // LANES, ne, LANES), lambda b: (b, 0, 0, 0))],
        out_specs=[pl.BlockSpec((1, ne, c_tot), lambda b: (b, 0, 0)),
                   pl.BlockSpec((1, ne, c_tot, LANES), lambda b: (b, 0, 0, 0))],
        out_shape=[jax.ShapeDtypeStruct((bsz, ne, c_tot), I32), jax.ShapeDtypeStruct((bsz, ne, c_tot, LANES), F32)],
        scratch_shapes=[pltpu.VMEM((ne, LANES, LANES), F32), pltpu.VMEM((ne, LANES, LANES), F32),
                        pltpu.VMEM((3, ne, LANES, LANES), F32)],
        compiler_params=_cparams(("arbitrary",), VMEM_LIMIT),
        name="moe_select",
    )(aff_t)
    idx = idx.reshape(bsz * ne, 1, c_tot)
    smem = functools.partial(pl.BlockSpec, memory_space=pltpu.SMEM)

    xg = pl.pallas_call(
        _gather_kernel,
        grid=(bsz, ne),
        in_specs=[smem((1, 1, c_tot), lambda b, e: (b * ne + e, 0, 0)),
                  pl.BlockSpec((1, t_all, SUBLANES, LANES), lambda b, e: (b, 0, 0, 0), pipeline_mode=pl.Buffered(1))],
        out_specs=pl.BlockSpec((1, 1, c_tot, d), lambda b, e: (b, e, 0, 0)),
        out_shape=jax.ShapeDtypeStruct((bsz, ne, c_tot, d), BF16),
        scratch_shapes=[pltpu.VMEM((c_tot, SUBLANES, LANES), F32)],
        compiler_params=_cparams(("arbitrary", "arbitrary"), VMEM_LIMIT),
        name="moe_gather",
    )(idx, hrow)

    nf = EXPERT_FF // FF_TILE
    y = pl.pallas_call(
        functools.partial(_ffn_kernel, c_lat=c_lat),
        grid=(bsz, ne, nf),
        in_specs=[pl.BlockSpec((1, 1, c_tot, d), lambda b, e, f: (b, e, 0, 0)),
                  pl.BlockSpec((1, 1, c_tot, LANES), lambda b, e, f: (b, e, 0, 0)),
                  pl.BlockSpec((2, 1, 6 * d), lambda b, e, f: (b, 0, 0)),
                  pl.BlockSpec((1, 1, d, FF_TILE), lambda b, e, f: (l, e, 0, f)),
                  pl.BlockSpec((1, 1, d, FF_TILE), lambda b, e, f: (l, e, 0, f)),
                  pl.BlockSpec((1, 1, FF_TILE, d), lambda b, e, f: (l, e, f, 0))],
        out_specs=pl.BlockSpec((1, 1, c_tot, d), lambda b, e, f: (b, e, 0, 0)),
        out_shape=jax.ShapeDtypeStruct((bsz, ne, c_tot, d), F32),
        compiler_params=_cparams(("arbitrary", "arbitrary", "arbitrary"), VMEM_LIMIT),
        name="moe_ffn",
    )(xg, gates, modtab, w_gate, w_up, w_down)

    return pl.pallas_call(
        _combine_kernel,
        grid=(bsz, ne),
        in_specs=[smem((1, 1, c_tot), lambda b, e: (b * ne + e, 0, 0)),
                  pl.BlockSpec(memory_space=pl.ANY),
                  pl.BlockSpec((1, 1, c_tot, d), lambda b, e: (b, e, 0, 0))],
        out_specs=pl.BlockSpec(memory_space=pl.ANY),
        out_shape=jax.ShapeDtypeStruct((bsz, n_lat if latent_only else t_all, d), F32),
        scratch_shapes=[pltpu.VMEM((t_all, d), F32), pltpu.SemaphoreType.DMA(())],
        input_output_aliases={} if latent_only else {1: 0},
        compiler_params=_cparams(("arbitrary", "arbitrary"), VMEM_LIMIT),
        name="moe_combine",
    )(idx, xs, y)


def _tile_specs(d, n_lat_tiles):
    x_spec = pl.BlockSpec((1, TM, d), lambda b, i: (b, i, 0))
    m_spec = pl.BlockSpec((1, 1, 6 * d), lambda b, i: (2 * b + i // n_lat_tiles, 0, 0))
    return x_spec, m_spec


def _step_specs(d, s_tiles):
    x_spec = pl.BlockSpec((1, s_tiles * TM, d), lambda b, i: (b, i, 0))
    m_spec = pl.BlockSpec((2, 1, 6 * d), lambda b, i: (b, 0, 0))
    return x_spec, m_spec


def _tiles_per_step(nt):
    return next(s for s in TILES_PER_STEP if nt % s == 0)


def _const_spec(shape):
    return pl.BlockSpec(shape, lambda b, i: (0,) * len(shape))


def _even_layer(xs, modtab, n1g, n2g, rwt, w_in_ext, lg, gn_g, conv_w, conv_b, ln_g, ln_b, w_out, cos, sin, n_lat):
    bsz, t_all, d = xs.shape
    nt = t_all // TM
    nlt = n_lat // TM
    x_spec, m_spec = _tile_specs(d, nlt)
    n_cols = 7 * 512
    st = _tiles_per_step(nt)
    xs_spec, ms_spec = _step_specs(d, st)
    proj = pl.pallas_call(
        functools.partial(_even_in_kernel, n_lat_tiles=nlt, s_tiles=st),
        grid=(bsz, nt // st),
        in_specs=[xs_spec, ms_spec, _const_spec((1, d)), _const_spec(w_in_ext.shape),
                  pl.BlockSpec((st * TM, 512), lambda b, i: (i, 0)), pl.BlockSpec((st * TM, 512), lambda b, i: (i, 0))],
        out_specs=pl.BlockSpec((1, st * TM, n_cols), lambda b, i: (b, i, 0)),
        out_shape=jax.ShapeDtypeStruct((bsz, t_all, n_cols), BF16),
        compiler_params=_cparams(("arbitrary", "arbitrary"), VMEM_LIMIT),
        name="even_in",
    )(xs, modtab, n1g, w_in_ext, cos, sin)

    fwd = lambda s: jnp.where(s == 0, nlt, s - 1)
    bwd = lambda s: jnp.where(s == 0, nlt, nlt - s)

    def col(order, j):
        return pl.BlockSpec((1, TM, 512), lambda b, s: (b, order(s), j))

    o_f, o_b = pl.pallas_call(
        _ret_kernel,
        grid=(bsz, nt),
        in_specs=[pl.BlockSpec(lg.shape, lambda b, s: (0, 0, 0, 0)),
                  col(fwd, 0), col(fwd, 1), col(fwd, 2), col(bwd, 0), col(bwd, 1), col(bwd, 2)],
        out_specs=[col(fwd, 0), col(bwd, 0)],
        out_shape=[jax.ShapeDtypeStruct((bsz, t_all, 512), BF16)] * 2,
        scratch_shapes=[pltpu.VMEM((2, RET_HEADS, RET_DK, 128), F32),
                        pltpu.VMEM((2, RET_HEADS, RET_CHUNK, RET_CHUNK), F32),
                        pltpu.VMEM((2, RET_HEADS, RET_CHUNK, 128), F32),
                        pltpu.VMEM((2, RET_HEADS, RET_CHUNK, 128), F32),
                        pltpu.VMEM((2, RET_HEADS, 1, 128), F32)],
        compiler_params=_cparams(("arbitrary", "arbitrary"), VMEM_LIMIT),
        name="retention",
    )(lg, proj, proj, proj, proj, proj, proj)

    hpt = TM // CONV_HALO
    n_halo = t_all // CONV_HALO

    def tcol(j):
        return pl.BlockSpec((1, TM, 512), lambda b, i: (b, i, j))

    def halo(j, nxt):
        if nxt:
            return pl.BlockSpec((1, CONV_HALO, 512), lambda b, i: (b, jnp.minimum((i + 1) * hpt, n_halo - 1), j))
        return pl.BlockSpec((1, CONV_HALO, 512), lambda b, i: (b, jnp.maximum(i * hpt - 1, 0), j))

    xs_new, hrow, aff_t = pl.pallas_call(
        functools.partial(_even_out_kernel, n_lat_tiles=nlt),
        grid=(bsz, nt),
        in_specs=[x_spec, m_spec, pl.BlockSpec((1, TM, 512), lambda b, i: (b, i, 0)),
                  pl.BlockSpec((1, TM, 512), lambda b, i: (b, i, 0)),
                  tcol(3), tcol(4), tcol(5), tcol(6), halo(5, False), halo(6, False), halo(5, True), halo(6, True),
                  _const_spec((1, 512)), _const_spec(conv_w.shape), _const_spec((1, 512)), _const_spec((1, 512)),
                  _const_spec((1, 512)), _const_spec(w_out.shape), _const_spec((1, d)), _const_spec(rwt.shape)],
        out_specs=[x_spec, pl.BlockSpec((1, TM, SUBLANES, LANES), lambda b, i: (b, i, 0, 0)),
                   pl.BlockSpec((1, TM // LANES, N_EXPERTS, LANES), lambda b, i: (b, i, 0, 0))],
        out_shape=[jax.ShapeDtypeStruct(xs.shape, F32), jax.ShapeDtypeStruct((bsz, t_all, SUBLANES, LANES), F32),
                   jax.ShapeDtypeStruct((bsz, t_all // LANES, N_EXPERTS, LANES), F32)],
        scratch_shapes=[pltpu.VMEM((TM + 2 * CONV_HALO, CONV_CH), F32),
                        pltpu.VMEM((SUBLANES - 1, TM + 2 * CONV_HALO - SUBLANES, CONV_CH), F32)],
        input_output_aliases={0: 0},
        compiler_params=_cparams(("arbitrary", "arbitrary"), VMEM_LIMIT),
        name="even_out",
    )(xs, modtab, o_f, o_b, proj, proj, proj, proj, proj, proj, proj, proj,
      gn_g, conv_w, conv_b, ln_g, ln_b, w_out, n2g, rwt)
    return xs_new, hrow, aff_t


def _odd_layer(xs, modtab, n1g, n2g, rwt, w_in_ext, qag, kvag, w_uq_ext, w_ukv_p, gq, gk, w_out, cos, sin, n_lat):
    bsz, t_all, d = xs.shape
    nt = t_all // TM
    nlt = n_lat // TM
    nh = MLA_HEADS
    st = _tiles_per_step(nt)
    xs_spec, ms_spec = _step_specs(d, st)
    q, kt, v = pl.pallas_call(
        functools.partial(_odd_in_kernel, n_lat_tiles=nlt, s_tiles=st),
        grid=(bsz, nt // st),
        in_specs=[xs_spec, ms_spec, _const_spec((1, d)), _const_spec(w_in_ext.shape), _const_spec(qag.shape),
                  _const_spec(kvag.shape), _const_spec(w_uq_ext.shape), _const_spec(w_ukv_p.shape),
                  _const_spec(gq.shape), _const_spec(gk.shape),
                  pl.BlockSpec((st * TM, 128), lambda b, i: (i, 0)), pl.BlockSpec((st * TM, 128), lambda b, i: (i, 0))],
        out_specs=[pl.BlockSpec((1, nh, st * TM, MLA_QK_PAD), lambda b, i: (b, 0, i, 0)),
                   pl.BlockSpec((1, nh, st, MLA_QK_PAD, TM), lambda b, i: (b, 0, i, 0, 0)),
                   pl.BlockSpec((1, nh, st * TM, 2 * MLA_V), lambda b, i: (b, 0, i, 0))],
        out_shape=[jax.ShapeDtypeStruct((bsz, nh, t_all, MLA_QK_PAD), BF16),
                   jax.ShapeDtypeStruct((bsz, nh, nt, MLA_QK_PAD, TM), BF16),
                   jax.ShapeDtypeStruct((bsz, nh, t_all, 2 * MLA_V), BF16)],
        compiler_params=_cparams(("arbitrary", "arbitrary"), VMEM_LIMIT),
        name="odd_in",
    )(xs, modtab, n1g, w_in_ext, qag, kvag, w_uq_ext, w_ukv_p, gq, gk, cos, sin)

    tps = next(c for c in ATT_KT_PER_STEP if nt % c == 0)
    nsub = ATT_TQ // ATT_SUB
    o = pl.pallas_call(
        functools.partial(_attn_kernel, tps=tps),
        grid=(bsz, nh, n_lat // ATT_TQ),
        in_specs=[pl.BlockSpec((1, 1, ATT_TQ, MLA_QK_PAD), lambda b, h, i: (b, h, i, 0)),
                  pl.BlockSpec((1, 1, nt, MLA_QK_PAD, TM), lambda b, h, i: (b, h, 0, 0, 0)),
                  pl.BlockSpec((1, 1, t_all, 2 * MLA_V), lambda b, h, i: (b, h, 0, 0))],
        out_specs=pl.BlockSpec((1, ATT_TQ, MLA_V), lambda b, h, i: (b, i, h)),
        out_shape=jax.ShapeDtypeStruct((bsz, n_lat, nh * MLA_V), BF16),
        scratch_shapes=[pltpu.VMEM((2, ATT_GROUP, ATT_SUB, tps * TM), F32), pltpu.VMEM((nsub, ATT_SUB, LANES), F32),
                        pltpu.VMEM((nsub, ATT_SUB, 2 * MLA_V), F32)],
        compiler_params=_cparams(("arbitrary", "arbitrary", "arbitrary"), VMEM_LIMIT),
        name="mla_attention",
    )(q, kt, v)
    o_ctx = pl.pallas_call(
        _attn_ctx_kernel,
        grid=(bsz, nh),
        in_specs=[pl.BlockSpec((1, 1, TM, MLA_QK_PAD), lambda b, h: (b, h, nlt, 0)),
                  pl.BlockSpec((1, 1, 1, MLA_QK_PAD, TM), lambda b, h: (b, h, nlt, 0, 0)),
                  pl.BlockSpec((1, 1, TM, 2 * MLA_V), lambda b, h: (b, h, nlt, 0))],
        out_specs=pl.BlockSpec((1, TM, MLA_V), lambda b, h: (b, 0, h)),
        out_shape=jax.ShapeDtypeStruct((bsz, TM, nh * MLA_V), BF16),
        compiler_params=_cparams(("arbitrary", "arbitrary"), VMEM_LIMIT),
        name="mla_attention_ctx",
    )(q, kt, v)

    o_specs = [pl.BlockSpec((1, TM, nh * MLA_V), lambda b, i, j=j: (b, jnp.minimum(i * st + j, nlt - 1), 0))
               for j in range(st)]
    xs_new, hrow, aff_t = pl.pallas_call(
        functools.partial(_odd_out_kernel, n_lat_tiles=nlt, s_tiles=st),
        grid=(bsz, nt // st),
        in_specs=[xs_spec, ms_spec, *o_specs, pl.BlockSpec((1, TM, nh * MLA_V), lambda b, i: (b, 0, 0)),
                  _const_spec(w_out.shape), _const_spec((1, d)), _const_spec(rwt.shape)],
        out_specs=[xs_spec, pl.BlockSpec((1, st * TM, SUBLANES, LANES), lambda b, i: (b, i, 0, 0)),
                   pl.BlockSpec((1, st * TM // LANES, N_EXPERTS, LANES), lambda b, i: (b, i, 0, 0))],
        out_shape=[jax.ShapeDtypeStruct(xs.shape, F32), jax.ShapeDtypeStruct((bsz, t_all, SUBLANES, LANES), F32),
                   jax.ShapeDtypeStruct((bsz, t_all // LANES, N_EXPERTS, LANES), F32)],
        input_output_aliases={0: 0},
        compiler_params=_cparams(("arbitrary", "arbitrary"), VMEM_LIMIT),
        name="odd_out",
    )(xs, modtab, *([o] * st), o_ctx, w_out, n2g, rwt)
    return xs_new, hrow, aff_t


def _rope_tables(n_lat, n_ctx, nf):
    t = jnp.arange(n_lat)
    row = (t // GRID_W).astype(F32)
    col = (t % GRID_W).astype(F32)
    inv = ROPE_BASE ** (-(jnp.arange(nf, dtype=F32) / nf))
    ar = row[:, None] * inv[None, :]
    ac = col[:, None] * inv[None, :]
    cos = jnp.concatenate([jnp.cos(ar), jnp.cos(ar), jnp.cos(ac), jnp.cos(ac)], axis=-1)
    sin = jnp.concatenate([-jnp.sin(ar), jnp.sin(ar), -jnp.sin(ac), jnp.sin(ac)], axis=-1)
    cos = jnp.concatenate([cos, jnp.ones((n_ctx, 4 * nf), F32)], axis=0)
    sin = jnp.concatenate([sin, jnp.zeros((n_ctx, 4 * nf), F32)], axis=0)
    return cos, sin


def _swap_perm(nf):
    a = jnp.arange(nf)
    return jnp.concatenate([a + nf, a, a + 3 * nf, a + 2 * nf])


def kernel(x, c, ctx, c_ctx, mod_w, mod_b, norm1_g, norm2_g, ev_w_in, ret_decay_f, ret_decay_b, ret_gn_g, conv_w, conv_b, conv_ln_g, conv_ln_b, ev_w_out, od_w_in, mla_q_a_g, mla_kv_a_g, mla_w_uq, mla_w_ukv, mla_qk_g_q, mla_qk_g_k, od_w_out, router_w, moe_w_gate, moe_w_up, moe_w_down):
    bsz, n_lat, d = x.shape
    n_ctx = ctx.shape[1]
    depth = mod_w.shape[0]
    assert d == D_MODEL and n_ctx == TM and n_lat % ATT_TQ == 0 and bsz + 1 <= 8

    xs = jnp.concatenate([x, ctx], axis=1)
    cc = jnp.zeros((8, d), F32).at[:bsz].set(c).at[bsz].set(c_ctx)
    mod = _mod_all(cc, mod_w, mod_b)

    cos_e, sin_e = _rope_tables(n_lat, n_ctx, RET_DK // 4)
    cos_e, sin_e = jnp.tile(cos_e, (1, RET_HEADS)), jnp.tile(sin_e, (1, RET_HEADS))
    cos_o, sin_o = _rope_tables(n_lat, n_ctx, MLA_ROPE // 4)
    cos_o, sin_o = jnp.tile(cos_o, (1, 2)), jnp.tile(sin_o, (1, 2))
    perm_o = _swap_perm(MLA_ROPE // 4)

    for l in range(depth):
        i = l // 2
        modtab = jnp.stack([mod[l, :bsz], jnp.broadcast_to(mod[l, bsz], (bsz, 6 * d))], axis=1).reshape(bsz * 2, 1, 6 * d)
        n1g = norm1_g[l][None]
        n2g = norm2_g[l][None]
        rw = router_w[l]
        rw_hi = rw.astype(BF16)
        rw_lo = (rw - rw_hi.astype(F32)).astype(BF16)
        pad = jnp.zeros((d, 128 - N_EXPERTS), BF16)
        rwt = jnp.concatenate([rw_hi, pad, rw_lo, pad], axis=1)
        if l % 2 == 0:
            w_ext = ev_w_in[i].astype(BF16)
            lg = jnp.broadcast_to(jnp.stack([ret_decay_f[i], ret_decay_b[i]])[:, :, None, None], (2, RET_HEADS, 1, 128))
            cw = jnp.concatenate([conv_w[i], jnp.zeros((1, CONV_CH), F32)], axis=0)
            xs, hrow, aff_t = _even_layer(xs, modtab, n1g, n2g, rwt, w_ext, lg, ret_gn_g[i][None], cw, conv_b[i][None],
                                    conv_ln_g[i][None], conv_ln_b[i][None], ev_w_out[i].astype(BF16), cos_e, sin_e, n_lat)
        else:
            w = od_w_in[i]
            kpe = w[:, MLA_Q_LORA + MLA_KV_LORA:]
            w_ext = jnp.concatenate([w[:, :MLA_Q_LORA + MLA_KV_LORA], kpe, kpe, kpe[:, perm_o], kpe[:, perm_o]], axis=1).astype(BF16)
            wq = mla_w_uq[i].reshape(MLA_Q_LORA, MLA_HEADS, MLA_QK)
            wq_r = wq[:, :, MLA_NOPE:]
            w_uq_ext = jnp.concatenate([wq[:, :, :MLA_NOPE].reshape(MLA_Q_LORA, -1), wq_r.reshape(MLA_Q_LORA, -1),
                                        wq_r[:, :, perm_o].reshape(MLA_Q_LORA, -1)], axis=1).astype(BF16)
            wkv = mla_w_ukv[i].reshape(MLA_KV_LORA, MLA_HEADS, MLA_NOPE + MLA_V)
            w_ukv_p = jnp.concatenate([wkv[:, :, :MLA_NOPE].reshape(MLA_KV_LORA, -1),
                                       wkv[:, :, MLA_NOPE:].reshape(MLA_KV_LORA, -1)], axis=1).astype(BF16)

            def gains(g):
                gr = g[MLA_NOPE:]
                return jnp.stack([g[:MLA_NOPE], jnp.tile(gr, 2), jnp.tile(gr[perm_o], 2)])

            xs, hrow, aff_t = _odd_layer(xs, modtab, n1g, n2g, rwt, w_ext, mla_q_a_g[i][None], mla_kv_a_g[i][None], w_uq_ext,
                                   w_ukv_p, gains(mla_qk_g_q[i]), gains(mla_qk_g_k[i]), od_w_out[i].astype(BF16),
                                   cos_o, sin_o, n_lat)
        xs = _moe(xs, hrow, aff_t, modtab, l, moe_w_gate, moe_w_up, moe_w_down, n_lat, latent_only=l == depth - 1)
    return xs
```
